```python
import math
import jax
import jax.numpy as jnp
from jax import lax
import numpy as np

D_MODEL = 1024
BATCH = 16
SEQ = 2048
DEPTH = 4
DEC_BATCH = 32
DEC_SEQ = 2048
PAST_LEN = 128

DA_HEADS = 8
DA_HEAD_DIM = D_MODEL // (2 * DA_HEADS)
DA_V_DIM = 2 * DA_HEAD_DIM
DA_QK_WIDTH = DA_HEADS * 2 * DA_HEAD_DIM
DA_V_WIDTH = DA_HEADS * DA_V_DIM
ROPE_THETA = 10000.0
Q_BLOCK = 128
ML_HEADS = 4
ML_WIDTH = D_MODEL
ML_HEAD_DIM = ML_WIDTH // ML_HEADS
ML_PROJ_BLOCK = 4
ML_N_BLOCKS = ML_WIDTH // ML_PROJ_BLOCK
CONV_K = 3
CHUNK = 128
IN_WIDTH = 2 * DA_QK_WIDTH + DA_V_WIDTH + 2 * ML_WIDTH + 2 * D_MODEL
N_GROUPS = 4
EXPERTS_PER_GROUP = 8
N_EXPERTS = N_GROUPS * EXPERTS_PER_GROUP
TOP_K = 2
D_EXPERT = D_MODEL // 2
MOE_BLOCK = 128
DN_ALPHA = (2 * DEPTH) ** 0.25
DN_BETA = (8 * DEPTH) ** -0.25
LN_EPS = 1e-5

kernel_name = 'hybrid_diffattn_mlstm_hmoe_encoder'

F32 = jnp.float32


def _layer_norm(x, g, b):
    xf = x.astype(F32)
    mu = jnp.mean(xf, axis=-1, keepdims=True)
    var = jnp.mean(jnp.square(xf - mu), axis=-1, keepdims=True)
    y = (xf - mu) * lax.rsqrt(var + LN_EPS) * g.astype(F32) + b.astype(F32)
    return y.astype(x.dtype)


def _rope(t, seq_len):
    half = DA_HEAD_DIM // 2
    inv = ROPE_THETA ** (-jnp.arange(half, dtype=F32) * (2.0 / DA_HEAD_DIM))
    ang = jnp.arange(seq_len, dtype=F32)[:, None] * inv[None, :]
    ang = jnp.concatenate([ang, ang], axis=-1)[None, :, None, None, :]
    tf = t.astype(F32)
    rot = jnp.concatenate([-tf[..., half:], tf[..., :half]], axis=-1)
    return (tf * jnp.cos(ang) + rot * jnp.sin(ang)).astype(t.dtype)


def _diff_attention(q, k, v, lam, subln_g, lambda_init):
    bsz, seq_len = q.shape[0], q.shape[1]
    n_blocks = seq_len // Q_BLOCK
    qb = jnp.moveaxis(q.reshape(bsz, n_blocks, Q_BLOCK, DA_HEADS, 2, DA_HEAD_DIM), 1, 0)
    scale = DA_HEAD_DIM ** -0.5

    def one_block(q_blk):
        s = jnp.einsum('bqhcd,bkhcd->bhcqk', q_blk, k, preferred_element_type=F32) * scale
        p = jax.nn.softmax(s, axis=-1)
        a = p[:, :, 0] - lam * p[:, :, 1]
        return jnp.einsum('bhqk,bkhe->bqhe', a.astype(v.dtype), v)

    o = lax.map(one_block, qb)
    o = jnp.moveaxis(o, 0, 1).reshape(bsz, seq_len, DA_HEADS, DA_V_DIM).astype(F32)
    o = o * lax.rsqrt(jnp.mean(jnp.square(o), axis=-1, keepdims=True) + LN_EPS)
    o = o * subln_g.astype(F32) * (1.0 - lambda_init)
    return o.reshape(bsz, seq_len, DA_V_WIDTH).astype(v.dtype)


def _mlstm_chunkwise(q, k, v, ig, lf):
    bsz, nh, seq_len, dh = q.shape
    n_chunks = seq_len // CHUNK

    def to_chunks(t):
        return jnp.moveaxis(t.reshape((bsz, nh, n_chunks, CHUNK) + t.shape[3:]), 2, 0)

    mask = jnp.tril(jnp.ones((CHUNK, CHUNK), dtype=bool))

    def step(carry, xs):
        c_st, n_st, m_st = carry
        qj, kj, vj, ij, fj = xs
        b = jnp.cumsum(fj, axis=-1)
        logd = jnp.where(mask, b[..., :, None] - b[..., None, :] + ij[..., None, :], -jnp.inf)
        m_inter = b + m_st[..., None]
        m_j = jnp.maximum(jnp.max(logd, axis=-1), m_inter)
        sw = jnp.einsum('bhqd,bhkd->bhqk', qj, kj) * jnp.exp(logd - m_j[..., None])
        inter = jnp.exp(m_inter - m_j)
        num = jnp.einsum('bhqk,bhkd->bhqd', sw, vj) + inter[..., None] * jnp.einsum('bhqd,bhde->bhqe', qj, c_st)
        den = jnp.sum(sw, axis=-1) + inter * jnp.einsum('bhqd,bhd->bhq', qj, n_st)
        h = num / jnp.maximum(jnp.abs(den), jnp.exp(-m_j))[..., None]
        b_last = b[..., -1]
        logw = b_last[..., None] - b + ij
        m_new = jnp.maximum(b_last + m_st, jnp.max(logw, axis=-1))
        wts = jnp.exp(logw - m_new[..., None])
        keep = jnp.exp(b_last + m_st - m_new)
        c_new = keep[..., None, None] * c_st + jnp.einsum('bhl,bhld,bhle->bhde', wts, kj, vj)
        n_new = keep[..., None] * n_st + jnp.einsum('bhl,bhld->bhd', wts, kj)
        return (c_new, n_new, m_new), h

    init = (jnp.zeros((bsz, nh, dh, dh), F32), jnp.zeros((bsz, nh, dh), F32), jnp.zeros((bsz, nh), F32))
    _, hs = lax.scan(step, init, (to_chunks(q), to_chunks(k), to_chunks(v), to_chunks(ig), to_chunks(lf)))
    return jnp.moveaxis(hs, 0, 2).reshape(bsz, nh, seq_len, dh)


def _mlstm_branch(xm, z, conv_w, conv_b, wq, wk, wv, w_gate, b_gate, skip, gn_g):
    bsz, seq_len, _ = xm.shape
    pad = CONV_K // 2
    xp = jnp.pad(xm, ((0, 0), (pad, pad), (0, 0)))
    xc = conv_b + sum(xp[:, j:j + seq_len] * conv_w[j] for j in range(CONV_K))
    xc = jax.nn.silu(xc)

    def headwise(t, w):
        return jnp.einsum('bsgi,gio->bsgo', t.reshape(bsz, seq_len, ML_N_BLOCKS, ML_PROJ_BLOCK), w).reshape(bsz, seq_len, ML_WIDTH)

    q = headwise(xc, wq)
    k = headwise(xc, wk)
    v = headwise(xm, wv)
    gates = (jnp.concatenate([q, k, v], axis=-1) @ w_gate + b_gate).astype(F32)
    gates = jnp.transpose(gates.reshape(bsz, seq_len, 2, 2, ML_HEADS), (2, 3, 0, 4, 1))

    def heads(t):
        return jnp.transpose(t.astype(F32).reshape(bsz, seq_len, ML_HEADS, ML_HEAD_DIM), (0, 2, 1, 3))

    qh = heads(q) * (ML_HEAD_DIM ** -0.5)
    kh = heads(k)
    vh = heads(v)
    h_f = _mlstm_chunkwise(qh, kh, vh, gates[0, 0], jax.nn.log_sigmoid(gates[0, 1]))

    def flip(t):
        return jnp.flip(t, axis=2)

    h_b = flip(_mlstm_chunkwise(flip(qh), flip(kh), flip(vh), flip(gates[1, 0]), flip(jax.nn.log_sigmoid(gates[1, 1]))))
    h = h_f + h_b
    mu = jnp.mean(h, axis=-1, keepdims=True)
    var = jnp.mean(jnp.square(h - mu), axis=-1, keepdims=True)
    hn = (h - mu) * lax.rsqrt(var + LN_EPS)
    hn = jnp.transpose(hn, (0, 2, 1, 3)).reshape(bsz, seq_len, ML_WIDTH) * gn_g.astype(F32)
    out = (hn + skip.astype(F32) * xc.astype(F32)) * jax.nn.silu(z.astype(F32))
    return out.astype(xm.dtype)


def _hier_moe(x, rg_w, rg_b, re_w, re_b, we_g, we_u, we_d):
    bsz, seq_len, d = x.shape
    n_tok = bsz * seq_len
    n_assign = n_tok * TOP_K
    xf = x.reshape(n_tok, d)
    p_group = jax.nn.softmax((xf @ rg_w + rg_b).astype(F32), axis=-1)
    g_sel = jnp.argmax(p_group, axis=-1)
    g_prob = jnp.take_along_axis(p_group, g_sel[:, None], axis=-1)[:, 0]
    logit_e = (xf @ re_w + re_b).astype(F32).reshape(n_tok, N_GROUPS, EXPERTS_PER_GROUP)
    logit_e = jnp.take_along_axis(logit_e, g_sel[:, None, None], axis=1)[:, 0]
    p_e = jax.nn.softmax(logit_e, axis=-1)
    top_p, top_i = lax.top_k(p_e, TOP_K)
    gate = (g_prob[:, None] * top_p / jnp.sum(top_p, axis=-1, keepdims=True)).reshape(n_assign)
    eid = (g_sel[:, None] * EXPERTS_PER_GROUP + top_i).reshape(n_assign).astype(jnp.int32)
    tok = jnp.repeat(jnp.arange(n_tok, dtype=jnp.int32), TOP_K)
    order = jnp.argsort(eid)
    s_eid = eid[order]
    s_tok = tok[order]
    s_gate = gate[order]
    counts = jnp.zeros((N_EXPERTS,), jnp.int32).at[eid].add(1)
    offsets = jnp.cumsum(counts) - counts
    padded = (counts + MOE_BLOCK - 1) // MOE_BLOCK * MOE_BLOCK
    pad_end = jnp.cumsum(padded)
    pad_start = pad_end - padded
    dest = pad_start[s_eid] + jnp.arange(n_assign, dtype=jnp.int32) - offsets[s_eid]
    n_slots = ((n_assign + MOE_BLOCK - 1) // MOE_BLOCK + N_EXPERTS) * MOE_BLOCK
    n_blk = n_slots // MOE_BLOCK
    slot_tok = jnp.zeros((n_slots,), jnp.int32).at[dest].set(s_tok)
    slot_gate = jnp.zeros((n_slots,), F32).at[dest].set(s_gate)
    blk_start = jnp.arange(n_blk, dtype=jnp.int32) * MOE_BLOCK
    blk_exp = jnp.minimum(jnp.sum(pad_end[None, :] <= blk_start[:, None], axis=1), N_EXPERTS - 1)

    def run_block(args):
        tok_b, e = args
        xb = xf[tok_b]
        h = jax.nn.silu(xb @ we_g[e]) * (xb @ we_u[e])
        return h @ we_d[e]

    out = lax.map(run_block, (slot_tok.reshape(n_blk, MOE_BLOCK), blk_exp))
    out = out.reshape(n_slots, d) * slot_gate[:, None].astype(x.dtype)
    y = jnp.zeros((n_tok, d), x.dtype).at[slot_tok].add(out)
    return y.reshape(bsz, seq_len, d)


def _encoder_layer(x, lambda_init, w_in, lq1, lk1, lq2, lk2, subln_g, conv_w, conv_b, wq, wk, wv,
                   w_gate, b_gate, skip, gn_g, w_pa, w_pb, w_out, ln1_g, ln1_b,
                   rg_w, rg_b, re_w, re_b, we_g, we_u, we_d, ln2_g, ln2_b):
    bsz, seq_len, _ = x.shape
    u = x @ w_in
    cuts = [int(c) for c in np.cumsum([DA_QK_WIDTH, DA_QK_WIDTH, DA_V_WIDTH, ML_WIDTH, ML_WIDTH, D_MODEL])]
    q, k, v, xm, z, ga, gb = jnp.split(u, cuts, axis=-1)
    q = _rope(q.reshape(bsz, seq_len, DA_HEADS, 2, DA_HEAD_DIM), seq_len)
    k = _rope(k.reshape(bsz, seq_len, DA_HEADS, 2, DA_HEAD_DIM), seq_len)
    v = v.reshape(bsz, seq_len, DA_HEADS, DA_V_DIM)
    lam = (jnp.exp(jnp.sum(lq1.astype(F32) * lk1.astype(F32)))
           - jnp.exp(jnp.sum(lq2.astype(F32) * lk2.astype(F32))) + lambda_init)
    a_out = _diff_attention(q, k, v, lam, subln_g, lambda_init) @ w_pa
    m_out = _mlstm_branch(xm, z, conv_w, conv_b, wq, wk, wv, w_gate, b_gate, skip, gn_g) @ w_pb
    mixed = jax.nn.sigmoid(ga) * a_out + jax.nn.sigmoid(gb) * m_out
    x = _layer_norm(DN_ALPHA * x + mixed @ w_out, ln1_g, ln1_b)
    x = _layer_norm(DN_ALPHA * x + _hier_moe(x, rg_w, rg_b, re_w, re_b, we_g, we_u, we_d), ln2_g, ln2_b)
    return x


def setup_inputs(seed: int = 0) -> dict:
    key = jax.random.key(seed)
    ks = jax.random.split(key, 40)

    def nrm(k, shape, scale):
        return jax.random.normal(k, shape, F32) * scale

    ig_b = nrm(ks[15], (DEPTH, 2, 1, ML_HEADS), 0.1)
    fg_b = jnp.linspace(3.0, 6.0, ML_HEADS, dtype=F32)[None, None, None, :] + nrm(ks[16], (DEPTH, 2, 1, ML_HEADS), 0.1)
    ml_b_gate = jnp.concatenate([ig_b, fg_b], axis=2).reshape(DEPTH, 4 * ML_HEADS)
    return {
        'x_prompt': nrm(ks[0], (BATCH, SEQ, D_MODEL), 1.0),
        'x_sample': nrm(ks[1], (DEC_BATCH, DEC_SEQ, D_MODEL), 1.0),
        'w_in': nrm(ks[2], (DEPTH, D_MODEL, IN_WIDTH), D_MODEL ** -0.5),
        'da_lambda_q1': nrm(ks[3], (DEPTH, DA_HEAD_DIM), 0.1),
        'da_lambda_k1': nrm(ks[4], (DEPTH, DA_HEAD_DIM), 0.1),
        'da_lambda_q2': nrm(ks[5], (DEPTH, DA_HEAD_DIM), 0.1),
        'da_lambda_k2': nrm(ks[6], (DEPTH, DA_HEAD_DIM), 0.1),
        'da_subln_g': 1.0 + nrm(ks[7], (DEPTH, DA_V_DIM), 0.05),
        'ml_conv_w': nrm(ks[8], (DEPTH, CONV_K, ML_WIDTH), CONV_K ** -0.5),
        'ml_conv_b': nrm(ks[9], (DEPTH, ML_WIDTH), 0.02),
        'ml_wq': nrm(ks[10], (DEPTH, ML_N_BLOCKS, ML_PROJ_BLOCK, ML_PROJ_BLOCK), ML_PROJ_BLOCK ** -0.5),
        'ml_wk': nrm(ks[11], (DEPTH, ML_N_BLOCKS, ML_PROJ_BLOCK, ML_PROJ_BLOCK), ML_PROJ_BLOCK ** -0.5),
        'ml_wv': nrm(ks[12], (DEPTH, ML_N_BLOCKS, ML_PROJ_BLOCK, ML_PROJ_BLOCK), ML_PROJ_BLOCK ** -0.5),
        'ml_w_gate': nrm(ks[13], (DEPTH, 3 * ML_WIDTH, 4 * ML_HEADS), (3 * ML_WIDTH) ** -0.5),
        'ml_b_gate': ml_b_gate,
        'ml_skip': 1.0 + nrm(ks[14], (DEPTH, ML_WIDTH), 0.05),
        'ml_gn_g': 1.0 + nrm(ks[17], (DEPTH, ML_WIDTH), 0.05),
        'w_pa': nrm(ks[18], (DEPTH, DA_V_WIDTH, D_MODEL), DA_V_WIDTH ** -0.5),
        'w_pb': nrm(ks[19], (DEPTH, ML_WIDTH, D_MODEL), ML_WIDTH ** -0.5),
        'w_out': nrm(ks[20], (DEPTH, D_MODEL, D_MODEL), D_MODEL ** -0.5 * DN_BETA),
        'ln1_g': 1.0 + nrm(ks[21], (DEPTH, D_MODEL), 0.05),
        'ln1_b': nrm(ks[22], (DEPTH, D_MODEL), 0.02),
        'router_group_w': nrm(ks[23], (DEPTH, D_MODEL, N_GROUPS), D_MODEL ** -0.5),
        'router_group_b': nrm(ks[24], (DEPTH, N_GROUPS), 0.01),
        'router_expert_w': nrm(ks[25], (DEPTH, D_MODEL, N_EXPERTS), D_MODEL ** -0.5),
        'router_expert_b': nrm(ks[26], (DEPTH, N_EXPERTS), 0.01),
        'w_e_gate': nrm(ks[27], (DEPTH, N_EXPERTS, D_MODEL, D_EXPERT), D_MODEL ** -0.5),
        'w_e_up': nrm(ks[28], (DEPTH, N_EXPERTS, D_MODEL, D_EXPERT), D_MODEL ** -0.5),
        'w_e_down': nrm(ks[29], (DEPTH, N_EXPERTS, D_EXPERT, D_MODEL), D_EXPERT ** -0.5 * DN_BETA),
        'ln2_g': 1.0 + nrm(ks[30], (DEPTH, D_MODEL), 0.05),
        'ln2_b': nrm(ks[31], (DEPTH, D_MODEL), 0.02),
    }


def reference(x_prompt, x_sample, w_in, da_lambda_q1, da_lambda_k1, da_lambda_q2, da_lambda_k2,
              da_subln_g, ml_conv_w, ml_conv_b, ml_wq, ml_wk, ml_wv, ml_w_gate, ml_b_gate, ml_skip,
              ml_gn_g, w_pa, w_pb, w_out, ln1_g, ln1_b, router_group_w, router_group_b,
              router_expert_w, router_expert_b, w_e_gate, w_e_up, w_e_down, ln2_g, ln2_b):
    y_prompt = x_prompt
    y_sample = x_sample
    for l in range(DEPTH):
        lambda_init = 0.8 - 0.6 * math.exp(-0.3 * l)
        layer_params = (w_in[l], da_lambda_q1[l], da_lambda_k1[l], da_lambda_q2[l], da_lambda_k2[l],
                        da_subln_g[l], ml_conv_w[l], ml_conv_b[l], ml_wq[l], ml_wk[l], ml_wv[l],
                        ml_w_gate[l], ml_b_gate[l], ml_skip[l], ml_gn_g[l], w_pa[l], w_pb[l], w_out[l],
                        ln1_g[l], ln1_b[l], router_group_w[l], router_group_b[l], router_expert_w[l],
                        router_expert_b[l], w_e_gate[l], w_e_up[l], w_e_down[l], ln2_g[l], ln2_b[l])
        y_prompt = _encoder_layer(y_prompt, lambda_init, *layer_params)
        y_sample = _encoder_layer(y_sample, lambda_init, *layer_params)
    return (y_prompt, y_sample)
```

```python
import functools
import math

import jax
import jax.numpy as jnp
from jax import lax
from jax.experimental import pallas as pl
from jax.experimental.pallas import tpu as pltpu

F32 = jnp.float32
BF16 = jnp.bfloat16

D_MODEL = 1024
DEPTH = 4
DA_HEADS = 8
DA_HEAD_DIM = 64
DA_V_DIM = 128
ROPE_THETA = 10000.0
ML_HEADS = 4
ML_HEAD_DIM = 256
ML_PROJ_BLOCK = 4
N_GROUPS = 4
EXPERTS_PER_GROUP = 8
N_EXPERTS = 32
TOP_K = 2
D_EXPERT = 512
DN_ALPHA = (2 * DEPTH) ** 0.25
LN_EPS = 1e-5

LANES = 128
MXU_DIM = 256
CHUNK = 128
MOE_ROWS = 512
VMEM_LIMIT = 56 * 1024 * 1024

NEG_BIG = -1e30


def _cparams(sem):
    return pltpu.CompilerParams(dimension_semantics=sem, vmem_limit_bytes=VMEM_LIMIT)


def _inproj_kernel(x_ref, w_ref, cos_ref, sin_ref, o_ref, *, n_rope, n_scaled, scale, cw):
    xb = x_ref[...].astype(BF16)
    n_cols = o_ref.shape[1]
    tm = x_ref.shape[0]
    if n_rope:
        cos = cos_ref[...]
        sin = sin_ref[...]
        lane = lax.broadcasted_iota(jnp.int32, (tm, LANES), 1)
        first_half = (lane % DA_HEAD_DIM) < (DA_HEAD_DIM // 2)
    for c in range(n_cols // cw):
        acc = jnp.dot(xb, w_ref[:, c * cw:(c + 1) * cw], preferred_element_type=F32)
        for s in range(cw // LANES):
            col = c * cw + s * LANES
            t = acc[:, s * LANES:(s + 1) * LANES]
            if col < n_rope:
                rot = jnp.where(first_half, pltpu.roll(t, LANES - DA_HEAD_DIM // 2, 1),
                                pltpu.roll(t, DA_HEAD_DIM // 2, 1))
                t = t * cos + rot * sin
                if col < n_scaled:
                    t = t * scale
            o_ref[:, col:col + LANES] = t.astype(o_ref.dtype)


def _inproj(x, w, cos, sin, seq_len, *, n_rope, n_scaled, scale, out_dtype):
    n_tok, d = x.shape
    n_cols = w.shape[1]
    tm = min(512, seq_len)
    per_seq = seq_len // tm
    kern = functools.partial(_inproj_kernel, n_rope=n_rope, n_scaled=n_scaled, scale=scale, cw=512)
    return pl.pallas_call(
        kern,
        grid=(n_tok // tm,),
        in_specs=[
            pl.BlockSpec((tm, d), lambda i: (i, 0)),
            pl.BlockSpec((d, n_cols), lambda i: (0, 0)),
            pl.BlockSpec((tm, LANES), lambda i: (i % per_seq, 0)),
            pl.BlockSpec((tm, LANES), lambda i: (i % per_seq, 0)),
        ],
        out_specs=pl.BlockSpec((tm, n_cols), lambda i: (i, 0)),
        out_shape=jax.ShapeDtypeStruct((n_tok, n_cols), out_dtype),
        compiler_params=_cparams(("parallel",)),
    )(x, w, cos, sin)


def _attn_kernel(sc_ref, q_ref, k_ref, v_ref, g_ref, o_ref, *, bq):
    seq_len = q_ref.shape[0]
    lam = sc_ref[0]
    out_scale = sc_ref[1]
    k = k_ref[...]
    v = v_ref[...]
    g = g_ref[...]
    lane = lax.broadcasted_iota(jnp.int32, (bq, LANES), 1)
    is_first = lane < DA_HEAD_DIM
    dn = (((1,), (1,)), ((), ()))

    def body(i, carry):
        r0 = pl.multiple_of(i * bq, bq)
        qb = q_ref[pl.ds(r0, bq), :]
        zero = jnp.zeros_like(qb)
        s1 = lax.dot_general(jnp.where(is_first, qb, zero), k, dn, preferred_element_type=F32)
        s2 = lax.dot_general(jnp.where(is_first, zero, qb), k, dn, preferred_element_type=F32)
        p1 = jnp.exp(s1 - jnp.max(s1, axis=-1, keepdims=True))
        p2 = jnp.exp(s2 - jnp.max(s2, axis=-1, keepdims=True))
        r1 = 1.0 / jnp.sum(p1, axis=-1, keepdims=True)
        r2 = lam / jnp.sum(p2, axis=-1, keepdims=True)
        a = (p1 * r1 - p2 * r2).astype(BF16)
        o = jnp.dot(a, v, preferred_element_type=F32)
        o = o * lax.rsqrt(jnp.mean(o * o, axis=-1, keepdims=True) + LN_EPS)
        o = o * g * out_scale
        o_ref[pl.ds(r0, bq), :] = o.astype(o_ref.dtype)
        return carry

    lax.fori_loop(0, seq_len // bq, body, 0)


def _attention(qkv, scalars, subln_g, n_seq, seq_len):
    n_tok = qkv.shape[0]
    bq = min(256, seq_len)
    return pl.pallas_call(
        functools.partial(_attn_kernel, bq=bq),
        grid=(n_seq, DA_HEADS),
        in_specs=[
            pl.BlockSpec(memory_space=pltpu.SMEM),
            pl.BlockSpec((seq_len, LANES), lambda b, h: (b, h)),
            pl.BlockSpec((seq_len, LANES), lambda b, h: (b, DA_HEADS + h)),
            pl.BlockSpec((seq_len, LANES), lambda b, h: (b, 2 * DA_HEADS + h)),
            pl.BlockSpec((1, DA_V_DIM), lambda b, h: (0, 0)),
        ],
        out_specs=pl.BlockSpec((seq_len, DA_V_DIM), lambda b, h: (b, h)),
        out_shape=jax.ShapeDtypeStruct((n_tok, DA_HEADS * DA_V_DIM), BF16),
        compiler_params=_cparams(("parallel", "parallel")),
    )(scalars, qkv, qkv, qkv, subln_g)


def _split3(x):
    x1 = x.astype(BF16)
    r1 = x - x1.astype(F32)
    x2 = r1.astype(BF16)
    x3 = (r1 - x2.astype(F32)).astype(BF16)
    return x1, x2, x3


def _mlpre_kernel(xm_ref, cw_ref, cb_ref, wq_ref, wk_ref, wv_ref, wg_ref, bg_ref,
                  q_ref, k_ref, v_ref, xc_ref, g_ref, gacc_ref):
    j = pl.program_id(1)
    seq_len = xm_ref.shape[0]
    xm = xm_ref[...]
    row = lax.broadcasted_iota(jnp.int32, xm.shape, 0)
    prev = jnp.where(row == 0, 0.0, pltpu.roll(xm, 1, 0))
    nxt = jnp.where(row == seq_len - 1, 0.0, pltpu.roll(xm, seq_len - 1, 0))
    xc = cb_ref[...] + prev * cw_ref[0:1, :] + xm * cw_ref[1:2, :] + nxt * cw_ref[2:3, :]
    xc = xc * jax.nn.sigmoid(xc)
    xc_ref[...] = xc
    xcb = xc.astype(BF16)
    q = jnp.dot(xcb, wq_ref[...], preferred_element_type=F32)
    k = jnp.dot(xcb, wk_ref[...], preferred_element_type=F32)
    v = jnp.dot(xm.astype(BF16), wv_ref[...], preferred_element_type=F32)
    qb = q.astype(BF16)
    kb = k.astype(BF16)
    vb = v.astype(BF16)
    q_ref[...] = (q * (ML_HEAD_DIM ** -0.5)).astype(BF16)
    k_ref[...] = kb
    v_ref[...] = vb
    part = (jnp.dot(qb, wg_ref[0], preferred_element_type=F32)
            + jnp.dot(kb, wg_ref[1], preferred_element_type=F32)
            + jnp.dot(vb, wg_ref[2], preferred_element_type=F32))

    @pl.when(j == 0)
    def _():
        gacc_ref[...] = part

    @pl.when(j > 0)
    def _():
        gacc_ref[...] += part

    @pl.when(j == pl.num_programs(1) - 1)
    def _():
        r = lax.broadcasted_iota(jnp.int32, (CHUNK, CHUNK), 0)
        c = lax.broadcasted_iota(jnp.int32, (CHUNK, CHUNK), 1)
        tri = jnp.where(c <= r, 1.0, 0.0).astype(BF16)
        lane = lax.broadcasted_iota(jnp.int32, (CHUNK, LANES), 1)
        is_forget = (lane % 2) == 1
        is_bwd = ((lane // 2) % 2) == 1
        bg = bg_ref[...]
        for ci in range(seq_len // CHUNK):
            rows = slice(ci * CHUNK, (ci + 1) * CHUNK)
            pre = gacc_ref[rows, :] + bg
            lf = jnp.minimum(pre, 0.0) - jnp.log1p(jnp.exp(-jnp.abs(pre)))
            l1, l2, l3 = _split3(lf)
            pref = (jnp.dot(tri, l1, preferred_element_type=F32)
                    + jnp.dot(tri, l2, preferred_element_type=F32)
                    + jnp.dot(tri, l3, preferred_element_type=F32))
            total = pref[CHUNK - 1:CHUNK, :]
            suff = total - pref + lf
            cum = jnp.where(is_bwd, suff, pref)
            g_ref[rows, :] = jnp.where(is_forget, cum, pre)


def _mlstm_pre(rest, conv_w, conv_b, wq_bd, wk_bd, wv_bd, wg, bg, n_seq, seq_len):
    n_tok = rest.shape[0]
    n_ct = D_MODEL // MXU_DIM
    tile = pl.BlockSpec((seq_len, MXU_DIM), lambda b, j: (b, j))
    wspec = pl.BlockSpec((None, MXU_DIM, MXU_DIM), lambda b, j: (j, 0, 0))
    act = jax.ShapeDtypeStruct((n_tok, D_MODEL), BF16)
    return pl.pallas_call(
        _mlpre_kernel,
        grid=(n_seq, n_ct),
        in_specs=[
            tile,
            pl.BlockSpec((3, MXU_DIM), lambda b, j: (0, j)),
            pl.BlockSpec((1, MXU_DIM), lambda b, j: (0, j)),
            wspec, wspec, wspec,
            pl.BlockSpec((3, MXU_DIM, LANES), lambda b, j: (0, j, 0)),
            pl.BlockSpec((1, LANES), lambda b, j: (0, 0)),
        ],
        out_specs=[tile, tile, tile, tile,
                   pl.BlockSpec((None, seq_len, LANES), lambda b, j: (b, 0, 0))],
        out_shape=[act, act, act,
                   jax.ShapeDtypeStruct((n_tok, D_MODEL), F32),
                   jax.ShapeDtypeStruct((n_seq, seq_len, LANES), F32)],
        scratch_shapes=[pltpu.VMEM((seq_len, LANES), F32)],
        compiler_params=_cparams(("parallel", "arbitrary")),
    )(rest, conv_w, conv_b, wq_bd, wk_bd, wv_bd, wg, bg)


def _mlstm_kernel(q_ref, k_ref, v_ref, gr_ref, gc_ref, o_ref, c_ref, n_ref, m_ref, h_ref):
    seq_len = q_ref.shape[0]
    n_chunks = seq_len // CHUNK
    r = lax.broadcasted_iota(jnp.int32, (CHUNK, CHUNK), 0)
    c = lax.broadcasted_iota(jnp.int32, (CHUNK, CHUNK), 1)
    dn_t = (((1,), (1,)), ((), ()))

    for direction in (0, 1):
        mask = (c <= r) if direction == 0 else (c >= r)
        last = CHUNK - 1 if direction == 0 else 0
        c_ref[...] = jnp.zeros_like(c_ref)
        n_ref[...] = jnp.zeros_like(n_ref)
        m_ref[...] = jnp.zeros_like(m_ref)

        def body(step, carry, direction=direction, mask=mask, last=last):
            ci = step if direction == 0 else n_chunks - 1 - step
            r0 = pl.multiple_of(ci * CHUNK, CHUNK)
            rows = pl.ds(r0, CHUNK)
            qc = q_ref[rows, :]
            kc = k_ref[rows, :]
            vc = v_ref[rows, :]
            ig_r = gr_ref[2 * direction:2 * direction + 1, rows]
            b_r = gr_ref[2 * direction + 1:2 * direction + 2, rows]
            ig_c = gc_ref[rows, 2 * direction:2 * direction + 1]
            b_c = gc_ref[rows, 2 * direction + 1:2 * direction + 2]
            m_st = m_ref[...]
            c_st = c_ref[...]
            n_st = n_ref[...]
            logd = jnp.where(mask, b_c - b_r + ig_r, -jnp.inf)
            m_inter = b_c + m_st
            m_j = jnp.maximum(jnp.max(logd, axis=-1, keepdims=True), m_inter)
            qk = lax.dot_general(qc, kc, dn_t, preferred_element_type=F32)
            sw = qk * jnp.exp(logd - m_j)
            inter = jnp.exp(m_inter - m_j)
            num = (jnp.dot(sw.astype(BF16), vc, preferred_element_type=F32)
                   + inter * jnp.dot(qc, c_st.astype(BF16), preferred_element_type=F32))
            den = (jnp.sum(sw, axis=-1, keepdims=True)
                   + inter * jnp.sum(qc.astype(F32) * n_st, axis=-1, keepdims=True))
            h = num / jnp.maximum(jnp.abs(den), jnp.exp(-m_j))
            b_last = b_c[last:last + 1, :]
            logw = b_last - b_c + ig_c
            m_new = jnp.maximum(b_last + m_st, jnp.max(logw, axis=0, keepdims=True))
            wts = jnp.exp(logw - m_new)
            keep = jnp.exp(b_last + m_st - m_new)
            kw = kc.astype(F32) * wts
            c_ref[...] = keep * c_st + jnp.dot(kw.T.astype(BF16), vc, preferred_element_type=F32)
            n_ref[...] = keep * n_st + jnp.sum(kw, axis=0, keepdims=True)
            m_ref[...] = m_new
            if direction == 0:
                h_ref[rows, :] = h
            else:
                h = h_ref[rows, :] + h
                mu = jnp.mean(h, axis=-1, keepdims=True)
                hc = h - mu
                var = jnp.mean(hc * hc, axis=-1, keepdims=True)
                o_ref[rows, :] = hc * lax.rsqrt(var + LN_EPS)
            return carry

        lax.fori_loop(0, n_chunks, body, 0)


def _mlstm(q, k, v, g_row, g_col, n_seq, seq_len):
    n_tok = q.shape[0]
    tile = pl.BlockSpec((seq_len, ML_HEAD_DIM), lambda b, h: (b, h))
    return pl.pallas_call(
        _mlstm_kernel,
        grid=(n_seq, ML_HEADS),
        in_specs=[
            tile, tile, tile,
            pl.BlockSpec((None, None, 4, seq_len), lambda b, h: (b, h, 0, 0)),
            pl.BlockSpec((None, None, seq_len, 4), lambda b, h: (b, h, 0, 0)),
        ],
        out_specs=tile,
        out_shape=jax.ShapeDtypeStruct((n_tok, D_MODEL), F32),
        scratch_shapes=[
            pltpu.VMEM((ML_HEAD_DIM, ML_HEAD_DIM), F32),
            pltpu.VMEM((1, ML_HEAD_DIM), F32),
            pltpu.VMEM((1, 1), F32),
            pltpu.VMEM((seq_len, ML_HEAD_DIM), F32),
        ],
        compiler_params=_cparams(("parallel", "parallel")),
    )(q, k, v, g_row, g_col)


def _layer_norm(y, g, b):
    mu = jnp.mean(y, axis=-1, keepdims=True)
    yc = y - mu
    var = jnp.mean(yc * yc, axis=-1, keepdims=True)
    return yc * lax.rsqrt(var + LN_EPS) * g + b


def _post_kernel(attn_ref, hn_ref, xc_ref, z_ref, ga_ref, gb_ref, x_ref,
                 wpa_ref, wpb_ref, wout_ref, gn_ref, skip_ref, lg_ref, lb_ref, wr_ref, br_ref,
                 x1_ref, route_ref):
    z = z_ref[...]
    ml = (hn_ref[...] * gn_ref[...] + skip_ref[...] * xc_ref[...]) * (z * jax.nn.sigmoid(z))
    a_out = jnp.dot(attn_ref[...], wpa_ref[...], preferred_element_type=F32)
    m_out = jnp.dot(ml.astype(BF16), wpb_ref[...], preferred_element_type=F32)
    mixed = jax.nn.sigmoid(ga_ref[...]) * a_out + jax.nn.sigmoid(gb_ref[...]) * m_out
    y = DN_ALPHA * x_ref[...] + jnp.dot(mixed.astype(BF16), wout_ref[...],
                                        preferred_element_type=F32)
    x1 = _layer_norm(y, lg_ref[...], lb_ref[...])
    x1_ref[...] = x1

    logits = br_ref[...]
    for xp in _split3(x1):
        for wi in range(3):
            logits = logits + jnp.dot(xp, wr_ref[wi], preferred_element_type=F32)
    lane = lax.broadcasted_iota(jnp.int32, logits.shape, 1)
    big = jnp.int32(4 * LANES)
    gl = jnp.where(lane < N_GROUPS, logits, NEG_BIG)
    gmax = jnp.max(gl, axis=-1, keepdims=True)
    gsum = jnp.sum(jnp.where(lane < N_GROUPS, jnp.exp(gl - gmax), 0.0), axis=-1, keepdims=True)
    g_sel = jnp.min(jnp.where(gl == gmax, lane, big), axis=-1, keepdims=True)
    g_prob = 1.0 / gsum
    lo = N_GROUPS + EXPERTS_PER_GROUP * g_sel
    in_group = jnp.logical_and(lane >= lo, lane < lo + EXPERTS_PER_GROUP)
    el = jnp.where(in_group, logits, NEG_BIG)
    emax = jnp.max(el, axis=-1, keepdims=True)
    esum = jnp.sum(jnp.where(in_group, jnp.exp(el - emax), 0.0), axis=-1, keepdims=True)
    i1 = jnp.min(jnp.where(el == emax, lane, big), axis=-1, keepdims=True)
    el2 = jnp.where(lane == i1, NEG_BIG, el)
    emax2 = jnp.max(el2, axis=-1, keepdims=True)
    i2 = jnp.min(jnp.where(el2 == emax2, lane, big), axis=-1, keepdims=True)
    p1 = 1.0 / esum
    p2 = jnp.exp(emax2 - emax) / esum
    psum = p1 + p2
    gate1 = g_prob * p1 / psum
    gate2 = g_prob * p2 / psum
    e1 = (i1 - N_GROUPS).astype(F32)
    e2 = (i2 - N_GROUPS).astype(F32)
    route = jnp.where(lane == 0, e1, jnp.where(lane == 1, e2, jnp.where(lane == 2, gate1, gate2)))
    route_ref[...] = route


def _post(attn, hn, xc, rest, x, w_pa, w_pb, w_out, gn_g, skip, ln_g, ln_b, w_route, b_route):
    n_tok = x.shape[0]
    tm = 256
    row = lambda col: pl.BlockSpec((tm, D_MODEL), lambda i, col=col: (i, col))
    full = lambda shape: pl.BlockSpec(shape, lambda i: tuple(0 for _ in shape))
    vec = full((1, D_MODEL))
    return pl.pallas_call(
        _post_kernel,
        grid=(n_tok // tm,),
        in_specs=[row(0), row(0), row(0), row(1), row(2), row(3), row(0),
                  full((D_MODEL, D_MODEL)), full((D_MODEL, D_MODEL)), full((D_MODEL, D_MODEL)),
                  vec, vec, vec, vec,
                  full((3, D_MODEL, LANES)), full((1, LANES))],
        out_specs=[row(0), pl.BlockSpec((tm, LANES), lambda i: (i, 0))],
        out_shape=[jax.ShapeDtypeStruct((n_tok, D_MODEL), F32),
                   jax.ShapeDtypeStruct((n_tok, LANES), F32)],
        compiler_params=_cparams(("parallel",)),
    )(attn, hn, xc, rest, rest, rest, x, w_pa, w_pb, w_out, gn_g, skip, ln_g, ln_b,
      w_route, b_route)


def _moe_kernel(bexp_ref, nval_ref, nused_ref,
                tok_hbm, dst_hbm, gate_ref, x_hbm, wg_ref, wu_ref, wd_ref, y_hbm,
                tok_smem, dst_smem, xbuf, obuf, idx_sem, gat_sem, sca_sem):
    i = pl.program_id(0)
    n_used = nused_ref[0]
    p = i % 2

    def fetch_indices(blk, slot):
        ct = pltpu.make_async_copy(tok_hbm.at[blk], tok_smem.at[slot], idx_sem.at[0])
        cd = pltpu.make_async_copy(dst_hbm.at[blk], dst_smem.at[slot], idx_sem.at[1])
        ct.start()
        cd.start()
        ct.wait()
        cd.wait()

    def gather_copy(slot, r, t):
        return pltpu.make_async_copy(x_hbm.at[pl.ds(t, 1)], xbuf.at[slot, pl.ds(r, 1)],
                                     gat_sem.at[slot])

    def scatter_copy(slot, r, t):
        return pltpu.make_async_copy(obuf.at[slot, pl.ds(r, 1)], y_hbm.at[pl.ds(t, 1)],
                                     sca_sem.at[slot])

    def start_gather(blk, slot):
        fetch_indices(blk, slot)

        def body(r, carry):
            gather_copy(slot, r, tok_smem[slot, r]).start()
            return carry
        lax.fori_loop(0, MOE_ROWS, body, 0, unroll=8)

    def wait_rows(copy_fn, slot, n):
        def body(r, carry):
            copy_fn(slot, 0, 0).wait()
            return carry
        lax.fori_loop(0, n, body, 0)

    @pl.when(i == 0)
    def _():
        start_gather(0, 0)

    @pl.when(i + 1 < n_used)
    def _():
        start_gather(i + 1, 1 - p)

    @pl.when(i < n_used)
    def _():
        wait_rows(gather_copy, p, MOE_ROWS)

        @pl.when(i >= 2)
        def _():
            wait_rows(scatter_copy, p, nval_ref[jnp.maximum(i - 2, 0)])

        xb = xbuf[p].astype(BF16)
        hg = jnp.dot(xb, wg_ref[...], preferred_element_type=F32)
        hu = jnp.dot(xb, wu_ref[...], preferred_element_type=F32)
        hh = (hg * jax.nn.sigmoid(hg) * hu).astype(BF16)
        out = jnp.dot(hh, wd_ref[...], preferred_element_type=F32)
        obuf[p] = out * gate_ref[...]

        def body(r, carry):
            scatter_copy(p, r, dst_smem[p, r]).start()
            return carry
        lax.fori_loop(0, nval_ref[i], body, 0)

        @pl.when(i == n_used - 1)
        def _():
            wait_rows(scatter_copy, p, nval_ref[i])

            @pl.when(i >= 1)
            def _():
                wait_rows(scatter_copy, 1 - p, nval_ref[jnp.maximum(i - 1, 0)])


def _moe(x1, slot_tok, slot_dst, slot_gate, blk_exp, n_valid, n_used, we_g, we_u, we_d):
    n_tok = x1.shape[0]
    n_blk = blk_exp.shape[0]
    wspec_in = pl.BlockSpec((None, D_MODEL, D_EXPERT), lambda i, be, nv, nu: (be[i], 0, 0))
    wspec_out = pl.BlockSpec((None, D_EXPERT, D_MODEL), lambda i, be, nv, nu: (be[i], 0, 0))
    grid_spec = pltpu.PrefetchScalarGridSpec(
        num_scalar_prefetch=3,
        grid=(n_blk,),
        in_specs=[
            pl.BlockSpec(memory_space=pl.ANY),
            pl.BlockSpec(memory_space=pl.ANY),
            pl.BlockSpec((MOE_ROWS, 1), lambda i, be, nv, nu: (i, 0)),
            pl.BlockSpec(memory_space=pl.ANY),
            wspec_in, wspec_in, wspec_out,
        ],
        out_specs=pl.BlockSpec(memory_space=pl.ANY),
        scratch_shapes=[
            pltpu.SMEM((2, MOE_ROWS), jnp.int32),
            pltpu.SMEM((2, MOE_ROWS), jnp.int32),
            pltpu.VMEM((2, MOE_ROWS, D_MODEL), F32),
            pltpu.VMEM((2, MOE_ROWS, D_MODEL), F32),
            pltpu.SemaphoreType.DMA((2,)),
            pltpu.SemaphoreType.DMA((2,)),
            pltpu.SemaphoreType.DMA((2,)),
        ],
    )
    return pl.pallas_call(
        _moe_kernel,
        grid_spec=grid_spec,
        out_shape=jax.ShapeDtypeStruct((TOP_K * n_tok, D_MODEL), F32),
        compiler_params=_cparams(("arbitrary",)),
    )(blk_exp, n_valid, n_used, slot_tok.reshape(n_blk, MOE_ROWS),
      slot_dst.reshape(n_blk, MOE_ROWS), slot_gate.reshape(n_blk * MOE_ROWS, 1), x1,
      we_g, we_u, we_d)


def _route_plan(route, n_tok):
    n_assign = n_tok * TOP_K
    eid = route[:, 0:TOP_K].astype(jnp.int32).reshape(n_assign)
    gate = route[:, TOP_K:2 * TOP_K].reshape(n_assign)
    order = jnp.argsort(eid).astype(jnp.int32)
    s_eid = eid[order]
    counts = jnp.zeros((N_EXPERTS,), jnp.int32).at[eid].add(1)
    offsets = jnp.cumsum(counts) - counts
    padded = (counts + MOE_ROWS - 1) // MOE_ROWS * MOE_ROWS
    pad_end = jnp.cumsum(padded)
    pad_start = pad_end - padded
    dest = pad_start[s_eid] + jnp.arange(n_assign, dtype=jnp.int32) - offsets[s_eid]
    n_blk = (n_assign + MOE_ROWS - 1) // MOE_ROWS + N_EXPERTS
    n_slots = n_blk * MOE_ROWS
    s_tok = order // TOP_K
    s_dst = (order % TOP_K) * n_tok + s_tok
    slot_tok = jnp.zeros((n_slots,), jnp.int32).at[dest].set(s_tok)
    slot_dst = jnp.zeros((n_slots,), jnp.int32).at[dest].set(s_dst)
    slot_gate = jnp.zeros((n_slots,), F32).at[dest].set(gate[order])
    blk_start = jnp.arange(n_blk, dtype=jnp.int32) * MOE_ROWS
    blk_exp = jnp.minimum(jnp.sum(pad_end[None, :] <= blk_start[:, None], axis=1),
                          N_EXPERTS - 1).astype(jnp.int32)
    n_valid = jnp.clip(counts[blk_exp] - (blk_start - pad_start[blk_exp]), 0, MOE_ROWS)
    n_valid = jnp.where(blk_start < pad_end[-1], n_valid, 0).astype(jnp.int32)
    n_used = (pad_end[-1:] // MOE_ROWS).astype(jnp.int32)
    return slot_tok, slot_dst, slot_gate, blk_exp, n_valid, n_used


def _combine_kernel(x_ref, y0_ref, y1_ref, g_ref, b_ref, o_ref):
    y = DN_ALPHA * x_ref[...] + (y0_ref[...] + y1_ref[...])
    o_ref[...] = _layer_norm(y, g_ref[...], b_ref[...])


def _combine(x1, y, ln_g, ln_b):
    n_tok = x1.shape[0]
    tm = 256
    n_t = n_tok // tm
    vec = pl.BlockSpec((1, D_MODEL), lambda i: (0, 0))
    return pl.pallas_call(
        _combine_kernel,
        grid=(n_t,),
        in_specs=[pl.BlockSpec((tm, D_MODEL), lambda i: (i, 0)),
                  pl.BlockSpec((tm, D_MODEL), lambda i: (i, 0)),
                  pl.BlockSpec((tm, D_MODEL), lambda i: (i + n_t, 0)),
                  vec, vec],
        out_specs=pl.BlockSpec((tm, D_MODEL), lambda i: (i, 0)),
        out_shape=jax.ShapeDtypeStruct((n_tok, D_MODEL), F32),
        compiler_params=_cparams(("parallel",)),
    )(x1, y, y, ln_g, ln_b)


def _rope_tables(seq_len):
    half = DA_HEAD_DIM // 2
    inv = ROPE_THETA ** (-jnp.arange(half, dtype=F32) * (2.0 / DA_HEAD_DIM))
    ang = jnp.arange(seq_len, dtype=F32)[:, None] * inv[None, :]
    cos = jnp.tile(jnp.cos(ang), (1, LANES // half))
    sin = jnp.tile(jnp.sin(ang), (1, LANES // half))
    lane = jnp.arange(LANES)
    sign = jnp.where((lane % DA_HEAD_DIM) < half, -1.0, 1.0).astype(F32)
    return cos, sin * sign[None, :]


def _block_diag_tiles(w):
    per_tile = MXU_DIM // ML_PROJ_BLOCK
    n_tiles = w.shape[0] // per_tile
    w4 = w.reshape(n_tiles, per_tile, ML_PROJ_BLOCK, ML_PROJ_BLOCK)
    eye = jnp.eye(per_tile, dtype=w.dtype)
    bd = jnp.einsum('jgio,gh->jgiho', w4, eye)
    return bd.reshape(n_tiles, MXU_DIM, MXU_DIM).astype(BF16)


def _gate_perm():
    idx = []
    for h in range(ML_HEADS):
        for d in range(2):
            for kind in range(2):
                idx.append(d * 2 * ML_HEADS + kind * ML_HEADS + h)
    return jnp.array(idx, dtype=jnp.int32)


def _layer(x, n_seq, seq_len, lambda_init, cos, sin, p):
    n_tok = x.shape[0]
    qk_w = DA_HEADS * 2 * DA_HEAD_DIM
    v_w = DA_HEADS * DA_V_DIM
    w_in = p['w_in'].astype(BF16)
    qkv = _inproj(x, w_in[:, :2 * qk_w + v_w], cos, sin, seq_len, n_rope=2 * qk_w,
                  n_scaled=qk_w, scale=DA_HEAD_DIM ** -0.5, out_dtype=BF16)
    rest = _inproj(x, w_in[:, 2 * qk_w + v_w:], cos, sin, seq_len, n_rope=0, n_scaled=0,
                   scale=1.0, out_dtype=F32)

    lam = (jnp.exp(jnp.sum(p['lq1'] * p['lk1'])) - jnp.exp(jnp.sum(p['lq2'] * p['lk2']))
           + lambda_init)
    scalars = jnp.stack([lam, jnp.asarray(1.0 - lambda_init, F32)]).astype(F32)
    attn = _attention(qkv, scalars, p['subln_g'].reshape(1, DA_V_DIM), n_seq, seq_len)

    perm = _gate_perm()
    n_gate = 4 * ML_HEADS
    wg = p['w_gate'][:, perm].reshape(3, D_MODEL, n_gate)
    wg = jnp.pad(wg, ((0, 0), (0, 0), (0, LANES - n_gate))).astype(BF16)
    bg = jnp.pad(p['b_gate'][perm], (0, LANES - n_gate)).reshape(1, LANES)
    q, k, v, xc, gates = _mlstm_pre(
        rest, p['conv_w'], p['conv_b'].reshape(1, D_MODEL), _block_diag_tiles(p['wq']),
        _block_diag_tiles(p['wk']), _block_diag_tiles(p['wv']), wg, bg, n_seq, seq_len)
    g_col = gates[:, :, :n_gate].reshape(n_seq, seq_len, ML_HEADS, 4)
    g_col = jnp.transpose(g_col, (0, 2, 1, 3))
    g_row = jnp.transpose(g_col, (0, 1, 3, 2))
    hn = _mlstm(q, k, v, g_row, g_col, n_seq, seq_len)

    n_route = N_GROUPS + N_EXPERTS
    w_route = jnp.concatenate([p['rg_w'], p['re_w']], axis=1)
    w_route = jnp.pad(w_route, ((0, 0), (0, LANES - n_route)))
    w_route = jnp.stack(_split3(w_route))
    b_route = jnp.pad(jnp.concatenate([p['rg_b'], p['re_b']]), (0, LANES - n_route))
    vec = lambda a: a.reshape(1, D_MODEL)
    x1, route = _post(attn, hn, xc, rest, x, p['w_pa'].astype(BF16), p['w_pb'].astype(BF16),
                      p['w_out'].astype(BF16), vec(p['gn_g']), vec(p['skip']), vec(p['ln1_g']),
                      vec(p['ln1_b']), w_route, b_route.reshape(1, LANES))

    plan = _route_plan(route, n_tok)
    y = _moe(x1, *plan, p['we_g'].astype(BF16), p['we_u'].astype(BF16), p['we_d'].astype(BF16))
    return _combine(x1, y, vec(p['ln2_g']), vec(p['ln2_b']))


def kernel(x_prompt, x_sample, w_in, da_lambda_q1, da_lambda_k1, da_lambda_q2, da_lambda_k2, da_subln_g, ml_conv_w, ml_conv_b, ml_wq, ml_wk, ml_wv, ml_w_gate, ml_b_gate, ml_skip, ml_gn_g, w_pa, w_pb, w_out, ln1_g, ln1_b, router_group_w, router_group_b, router_expert_w, router_expert_b, w_e_gate, w_e_up, w_e_down, ln2_g, ln2_b):
    n_p, seq_len, d = x_prompt.shape
    n_s = x_sample.shape[0]
    assert x_sample.shape[1] == seq_len and d == D_MODEL and seq_len % CHUNK == 0
    n_seq = n_p + n_s
    x = jnp.concatenate([x_prompt, x_sample], axis=0).reshape(n_seq * seq_len, d)
    cos, sin = _rope_tables(seq_len)
    stacked = dict(w_in=w_in, lq1=da_lambda_q1, lk1=da_lambda_k1, lq2=da_lambda_q2,
                   lk2=da_lambda_k2, subln_g=da_subln_g, conv_w=ml_conv_w, conv_b=ml_conv_b,
                   wq=ml_wq, wk=ml_wk, wv=ml_wv, w_gate=ml_w_gate, b_gate=ml_b_gate,
                   skip=ml_skip, gn_g=ml_gn_g, w_pa=w_pa, w_pb=w_pb, w_out=w_out,
                   ln1_g=ln1_g, ln1_b=ln1_b, rg_w=router_group_w, rg_b=router_group_b,
                   re_w=router_expert_w, re_b=router_expert_b, we_g=w_e_gate, we_u=w_e_up,
                   we_d=w_e_down, ln2_g=ln2_g, ln2_b=ln2_b)
    for l in range(w_in.shape[0]):
        lambda_init = 0.8 - 0.6 * math.exp(-0.3 * l)
        x = _layer(x, n_seq, seq_len, lambda_init, cos, sin, {k: a[l] for k, a in stacked.items()})
    y = x.reshape(n_seq, seq_len, d)
    return (y[:n_p], y[n_p:])
```

```python
import functools
import math

import jax
import jax.numpy as jnp
from jax import lax
from jax.experimental import pallas as pl
from jax.experimental.pallas import tpu as pltpu

F32 = jnp.float32
BF16 = jnp.bfloat16

D_MODEL = 1024
DEPTH = 4
DA_HEADS = 8
DA_HEAD_DIM = 64
DA_V_DIM = 128
ROPE_THETA = 10000.0
ML_HEADS = 4
ML_HEAD_DIM = 256
ML_PROJ_BLOCK = 4
N_GROUPS = 4
EXPERTS_PER_GROUP = 8
N_EXPERTS = 32
TOP_K = 2
D_EXPERT = 512
DN_ALPHA = (2 * DEPTH) ** 0.25
LN_EPS = 1e-5

LANES = 128
MXU_DIM = 256
CHUNK = 128
MOE_ROWS = 512
VMEM_LIMIT = 56 * 1024 * 1024

NEG_BIG = -1e30


def _cparams(sem):
    return pltpu.CompilerParams(dimension_semantics=sem, vmem_limit_bytes=VMEM_LIMIT)


def _inproj_kernel(x_ref, w_ref, cos_ref, sin_ref, o_ref, *, n_rope, n_scaled, scale, cw):
    xb = x_ref[...].astype(BF16)
    n_cols = o_ref.shape[1]
    tm = x_ref.shape[0]
    if n_rope:
        cos = cos_ref[...]
        sin = sin_ref[...]
        lane = lax.broadcasted_iota(jnp.int32, (tm, LANES), 1)
        first_half = (lane % DA_HEAD_DIM) < (DA_HEAD_DIM // 2)
    for c in range(n_cols // cw):
        acc = jnp.dot(xb, w_ref[:, c * cw:(c + 1) * cw], preferred_element_type=F32)
        for s in range(cw // LANES):
            col = c * cw + s * LANES
            t = acc[:, s * LANES:(s + 1) * LANES]
            if col < n_rope:
                rot = jnp.where(first_half, pltpu.roll(t, LANES - DA_HEAD_DIM // 2, 1),
                                pltpu.roll(t, DA_HEAD_DIM // 2, 1))
                t = t * cos + rot * sin
                if col < n_scaled:
                    t = t * scale
            o_ref[:, col:col + LANES] = t.astype(o_ref.dtype)


def _inproj(x, w, cos, sin, seq_len, *, n_rope, n_scaled, scale, out_dtype):
    n_tok, d = x.shape
    n_cols = w.shape[1]
    tm = min(512, seq_len)
    per_seq = seq_len // tm
    kern = functools.partial(_inproj_kernel, n_rope=n_rope, n_scaled=n_scaled, scale=scale, cw=512)
    return pl.pallas_call(
        kern,
        grid=(n_tok // tm,),
        in_specs=[
            pl.BlockSpec((tm, d), lambda i: (i, 0)),
            pl.BlockSpec((d, n_cols), lambda i: (0, 0)),
            pl.BlockSpec((tm, LANES), lambda i: (i % per_seq, 0)),
            pl.BlockSpec((tm, LANES), lambda i: (i % per_seq, 0)),
        ],
        out_specs=pl.BlockSpec((tm, n_cols), lambda i: (i, 0)),
        out_shape=jax.ShapeDtypeStruct((n_tok, n_cols), out_dtype),
        compiler_params=_cparams(("parallel",)),
    )(x, w, cos, sin)


def _attn_kernel(sc_ref, q_ref, k_ref, v_ref, g_ref, o_ref, vx_ref, *, bq):
    seq_len = q_ref.shape[0]
    lam = sc_ref[0]
    out_scale = sc_ref[1]
    k = k_ref[...]
    vx_ref[:, :DA_V_DIM] = v_ref[...]
    vx_ref[:, DA_V_DIM:] = jnp.ones((seq_len, DA_V_DIM), vx_ref.dtype)
    vx = vx_ref[...]
    g = g_ref[...] * out_scale
    lane = lax.broadcasted_iota(jnp.int32, (bq, LANES), 1)
    is_first = lane < DA_HEAD_DIM
    dn = (((1,), (1,)), ((), ()))

    def softmax_av(qm):
        s = lax.dot_general(qm, k, dn, preferred_element_type=F32)
        p = jnp.exp2(s - jnp.max(s, axis=-1, keepdims=True)).astype(BF16)
        ox = jnp.dot(p, vx, preferred_element_type=F32)
        return ox[:, :DA_V_DIM] / ox[:, DA_V_DIM:]

    def body(i, carry):
        r0 = pl.multiple_of(i * bq, bq)
        qb = q_ref[pl.ds(r0, bq), :]
        zero = jnp.zeros_like(qb)
        o = softmax_av(jnp.where(is_first, qb, zero)) - lam * softmax_av(jnp.where(is_first, zero, qb))
        o = o * lax.rsqrt(jnp.mean(o * o, axis=-1, keepdims=True) + LN_EPS)
        o_ref[pl.ds(r0, bq), :] = (o * g).astype(o_ref.dtype)
        return carry

    lax.fori_loop(0, seq_len // bq, body, 0, unroll=2)


def _attention(qkv, scalars, subln_g, n_seq, seq_len):
    n_tok = qkv.shape[0]
    bq = min(256, seq_len)
    return pl.pallas_call(
        functools.partial(_attn_kernel, bq=bq),
        grid=(n_seq, DA_HEADS),
        in_specs=[
            pl.BlockSpec(memory_space=pltpu.SMEM),
            pl.BlockSpec((seq_len, LANES), lambda b, h: (b, h)),
            pl.BlockSpec((seq_len, LANES), lambda b, h: (b, DA_HEADS + h)),
            pl.BlockSpec((seq_len, LANES), lambda b, h: (b, 2 * DA_HEADS + h)),
            pl.BlockSpec((1, DA_V_DIM), lambda b, h: (0, 0)),
        ],
        out_specs=pl.BlockSpec((seq_len, DA_V_DIM), lambda b, h: (b, h)),
        out_shape=jax.ShapeDtypeStruct((n_tok, DA_HEADS * DA_V_DIM), BF16),
        scratch_shapes=[pltpu.VMEM((seq_len, 2 * DA_V_DIM), BF16)],
        compiler_params=_cparams(("parallel", "parallel")),
    )(scalars, qkv, qkv, qkv, subln_g)


def _split3(x):
    x1 = x.astype(BF16)
    r1 = x - x1.astype(F32)
    x2 = r1.astype(BF16)
    x3 = (r1 - x2.astype(F32)).astype(BF16)
    return x1, x2, x3


def _mlpre_kernel(xm_ref, cw_ref, cb_ref, wq_ref, wk_ref, wv_ref, wg_ref, bg_ref,
                  q_ref, k_ref, v_ref, xc_ref, g_ref, gacc_ref):
    j = pl.program_id(1)
    seq_len = xm_ref.shape[0]
    xm = xm_ref[...]
    row = lax.broadcasted_iota(jnp.int32, xm.shape, 0)
    prev = jnp.where(row == 0, 0.0, pltpu.roll(xm, 1, 0))
    nxt = jnp.where(row == seq_len - 1, 0.0, pltpu.roll(xm, seq_len - 1, 0))
    xc = cb_ref[...] + prev * cw_ref[0:1, :] + xm * cw_ref[1:2, :] + nxt * cw_ref[2:3, :]
    xc = xc * jax.nn.sigmoid(xc)
    xc_ref[...] = xc
    xcb = xc.astype(BF16)
    q = jnp.dot(xcb, wq_ref[...], preferred_element_type=F32)
    k = jnp.dot(xcb, wk_ref[...], preferred_element_type=F32)
    v = jnp.dot(xm.astype(BF16), wv_ref[...], preferred_element_type=F32)
    qb = q.astype(BF16)
    kb = k.astype(BF16)
    vb = v.astype(BF16)
    q_ref[...] = (q * (ML_HEAD_DIM ** -0.5)).astype(BF16)
    k_ref[...] = kb
    v_ref[...] = vb
    part = (jnp.dot(qb, wg_ref[0], preferred_element_type=F32)
            + jnp.dot(kb, wg_ref[1], preferred_element_type=F32)
            + jnp.dot(vb, wg_ref[2], preferred_element_type=F32))

    @pl.when(j == 0)
    def _():
        gacc_ref[...] = part

    @pl.when(j > 0)
    def _():
        gacc_ref[...] += part

    @pl.when(j == pl.num_programs(1) - 1)
    def _():
        r = lax.broadcasted_iota(jnp.int32, (CHUNK, CHUNK), 0)
        c = lax.broadcasted_iota(jnp.int32, (CHUNK, CHUNK), 1)
        tri = jnp.where(c <= r, 1.0, 0.0).astype(BF16)
        lane = lax.broadcasted_iota(jnp.int32, (CHUNK, LANES), 1)
        is_forget = (lane % 2) == 1
        is_bwd = ((lane // 2) % 2) == 1
        bg = bg_ref[...]
        for ci in range(seq_len // CHUNK):
            rows = slice(ci * CHUNK, (ci + 1) * CHUNK)
            pre = gacc_ref[rows, :] + bg
            lf = jnp.minimum(pre, 0.0) - jnp.log1p(jnp.exp(-jnp.abs(pre)))
            l1, l2, l3 = _split3(lf)
            pref = (jnp.dot(tri, l1, preferred_element_type=F32)
                    + jnp.dot(tri, l2, preferred_element_type=F32)
                    + jnp.dot(tri, l3, preferred_element_type=F32))
            total = pref[CHUNK - 1:CHUNK, :]
            suff = total - pref + lf
            cum = jnp.where(is_bwd, suff, pref)
            g_ref[rows, :] = jnp.where(is_forget, cum, pre)


def _mlstm_pre(rest, conv_w, conv_b, wq_bd, wk_bd, wv_bd, wg, bg, n_seq, seq_len):
    n_tok = rest.shape[0]
    n_ct = D_MODEL // MXU_DIM
    tile = pl.BlockSpec((seq_len, MXU_DIM), lambda b, j: (b, j))
    wspec = pl.BlockSpec((None, MXU_DIM, MXU_DIM), lambda b, j: (j, 0, 0))
    act = jax.ShapeDtypeStruct((n_tok, D_MODEL), BF16)
    return pl.pallas_call(
        _mlpre_kernel,
        grid=(n_seq, n_ct),
        in_specs=[
            tile,
            pl.BlockSpec((3, MXU_DIM), lambda b, j: (0, j)),
            pl.BlockSpec((1, MXU_DIM), lambda b, j: (0, j)),
            wspec, wspec, wspec,
            pl.BlockSpec((3, MXU_DIM, LANES), lambda b, j: (0, j, 0)),
            pl.BlockSpec((1, LANES), lambda b, j: (0, 0)),
        ],
        out_specs=[tile, tile, tile, tile,
                   pl.BlockSpec((None, seq_len, LANES), lambda b, j: (b, 0, 0))],
        out_shape=[act, act, act,
                   jax.ShapeDtypeStruct((n_tok, D_MODEL), F32),
                   jax.ShapeDtypeStruct((n_seq, seq_len, LANES), F32)],
        scratch_shapes=[pltpu.VMEM((seq_len, LANES), F32)],
        compiler_params=_cparams(("parallel", "arbitrary")),
    )(rest, conv_w, conv_b, wq_bd, wk_bd, wv_bd, wg, bg)


def _mlstm_kernel(q_ref, k_ref, v_ref, gr_ref, gc_ref, o_ref, c_ref, n_ref, m_ref, h_ref):
    seq_len = q_ref.shape[0]
    n_chunks = seq_len // CHUNK
    r = lax.broadcasted_iota(jnp.int32, (CHUNK, CHUNK), 0)
    c = lax.broadcasted_iota(jnp.int32, (CHUNK, CHUNK), 1)
    dn_t = (((1,), (1,)), ((), ()))
    c_ref[...] = jnp.zeros_like(c_ref)
    n_ref[...] = jnp.zeros_like(n_ref)
    m_ref[...] = jnp.zeros_like(m_ref)

    def chunk_step(direction, ci):
        mask = (c <= r) if direction == 0 else (c >= r)
        last = CHUNK - 1 if direction == 0 else 0
        rows = pl.ds(pl.multiple_of(ci * CHUNK, CHUNK), CHUNK)
        qc = q_ref[rows, :]
        kc = k_ref[rows, :]
        vc = v_ref[rows, :]
        ig_r = gr_ref[2 * direction:2 * direction + 1, rows]
        b_r = gr_ref[2 * direction + 1:2 * direction + 2, rows]
        ig_c = gc_ref[rows, 2 * direction:2 * direction + 1]
        b_c = gc_ref[rows, 2 * direction + 1:2 * direction + 2]
        m_st = m_ref[direction]
        c_st = c_ref[direction]
        n_st = n_ref[direction]
        logd = jnp.where(mask, b_c - b_r + ig_r, -jnp.inf)
        m_inter = b_c + m_st
        m_j = jnp.maximum(jnp.max(logd, axis=-1, keepdims=True), m_inter)
        qk = lax.dot_general(qc, kc, dn_t, preferred_element_type=F32)
        sw = qk * jnp.exp(logd - m_j)
        inter = jnp.exp(m_inter - m_j)
        num = (jnp.dot(sw.astype(BF16), vc, preferred_element_type=F32)
               + inter * jnp.dot(qc, c_st.astype(BF16), preferred_element_type=F32))
        den = (jnp.sum(sw, axis=-1, keepdims=True)
               + inter * jnp.sum(qc.astype(F32) * n_st, axis=-1, keepdims=True))
        h = num / jnp.maximum(jnp.abs(den), jnp.exp(-m_j))
        b_last = b_c[last:last + 1, :]
        logw = b_last - b_c + ig_c
        m_new = jnp.maximum(b_last + m_st, jnp.max(logw, axis=0, keepdims=True))
        wts = jnp.exp(logw - m_new)
        keep = jnp.exp(b_last + m_st - m_new)
        kw = kc.astype(F32) * wts
        c_ref[direction] = keep * c_st + jnp.dot(kw.T.astype(BF16), vc,
                                                 preferred_element_type=F32)
        n_ref[direction] = keep * n_st + jnp.sum(kw, axis=0, keepdims=True)
        m_ref[direction] = m_new
        h_ref[direction, rows, :] = h

    def sweep(step, carry):
        chunk_step(0, step)
        chunk_step(1, n_chunks - 1 - step)
        return carry

    lax.fori_loop(0, n_chunks, sweep, 0)

    def normalize(ci, carry):
        rows = pl.ds(pl.multiple_of(ci * CHUNK, CHUNK), CHUNK)
        h = h_ref[0, rows, :] + h_ref[1, rows, :]
        mu = jnp.mean(h, axis=-1, keepdims=True)
        hc = h - mu
        var = jnp.mean(hc * hc, axis=-1, keepdims=True)
        o_ref[rows, :] = hc * lax.rsqrt(var + LN_EPS)
        return carry

    lax.fori_loop(0, n_chunks, normalize, 0)


def _mlstm(q, k, v, g_row, g_col, n_seq, seq_len):
    n_tok = q.shape[0]
    tile = pl.BlockSpec((seq_len, ML_HEAD_DIM), lambda b, h: (b, h))
    return pl.pallas_call(
        _mlstm_kernel,
        grid=(n_seq, ML_HEADS),
        in_specs=[
            tile, tile, tile,
            pl.BlockSpec((None, None, 4, seq_len), lambda b, h: (b, h, 0, 0)),
            pl.BlockSpec((None, None, seq_len, 4), lambda b, h: (b, h, 0, 0)),
        ],
        out_specs=tile,
        out_shape=jax.ShapeDtypeStruct((n_tok, D_MODEL), F32),
        scratch_shapes=[
            pltpu.VMEM((2, ML_HEAD_DIM, ML_HEAD_DIM), F32),
            pltpu.VMEM((2, 1, ML_HEAD_DIM), F32),
            pltpu.VMEM((2, 1, 1), F32),
            pltpu.VMEM((2, seq_len, ML_HEAD_DIM), F32),
        ],
        compiler_params=_cparams(("parallel", "parallel")),
    )(q, k, v, g_row, g_col)


def _layer_norm(y, g, b):
    mu = jnp.mean(y, axis=-1, keepdims=True)
    yc = y - mu
    var = jnp.mean(yc * yc, axis=-1, keepdims=True)
    return yc * lax.rsqrt(var + LN_EPS) * g + b


def _post_kernel(attn_ref, hn_ref, xc_ref, z_ref, ga_ref, gb_ref, x_ref,
                 wpa_ref, wpb_ref, wout_ref, gn_ref, skip_ref, lg_ref, lb_ref, wr_ref, br_ref,
                 x1_ref, route_ref):
    z = z_ref[...]
    ml = (hn_ref[...] * gn_ref[...] + skip_ref[...] * xc_ref[...]) * (z * jax.nn.sigmoid(z))
    a_out = jnp.dot(attn_ref[...], wpa_ref[...], preferred_element_type=F32)
    m_out = jnp.dot(ml.astype(BF16), wpb_ref[...], preferred_element_type=F32)
    mixed = jax.nn.sigmoid(ga_ref[...]) * a_out + jax.nn.sigmoid(gb_ref[...]) * m_out
    y = DN_ALPHA * x_ref[...] + jnp.dot(mixed.astype(BF16), wout_ref[...],
                                        preferred_element_type=F32)
    x1 = _layer_norm(y, lg_ref[...], lb_ref[...])
    x1_ref[...] = x1

    logits = br_ref[...]
    for xp in _split3(x1):
        for wi in range(3):
            logits = logits + jnp.dot(xp, wr_ref[wi], preferred_element_type=F32)
    lane = lax.broadcasted_iota(jnp.int32, logits.shape, 1)
    big = jnp.int32(4 * LANES)
    gl = jnp.where(lane < N_GROUPS, logits, NEG_BIG)
    gmax = jnp.max(gl, axis=-1, keepdims=True)
    gsum = jnp.sum(jnp.where(lane < N_GROUPS, jnp.exp(gl - gmax), 0.0), axis=-1, keepdims=True)
    g_sel = jnp.min(jnp.where(gl == gmax, lane, big), axis=-1, keepdims=True)
    g_prob = 1.0 / gsum
    lo = N_GROUPS + EXPERTS_PER_GROUP * g_sel
    in_group = jnp.logical_and(lane >= lo, lane < lo + EXPERTS_PER_GROUP)
    el = jnp.where(in_group, logits, NEG_BIG)
    emax = jnp.max(el, axis=-1, keepdims=True)
    esum = jnp.sum(jnp.where(in_group, jnp.exp(el - emax), 0.0), axis=-1, keepdims=True)
    i1 = jnp.min(jnp.where(el == emax, lane, big), axis=-1, keepdims=True)
    el2 = jnp.where(lane == i1, NEG_BIG, el)
    emax2 = jnp.max(el2, axis=-1, keepdims=True)
    i2 = jnp.min(jnp.where(el2 == emax2, lane, big), axis=-1, keepdims=True)
    p1 = 1.0 / esum
    p2 = jnp.exp(emax2 - emax) / esum
    psum = p1 + p2
    gate1 = g_prob * p1 / psum
    gate2 = g_prob * p2 / psum
    e1 = (i1 - N_GROUPS).astype(F32)
    e2 = (i2 - N_GROUPS).astype(F32)
    route = jnp.where(lane == 0, e1, jnp.where(lane == 1, e2, jnp.where(lane == 2, gate1, gate2)))
    route_ref[...] = route


def _post(attn, hn, xc, rest, x, w_pa, w_pb, w_out, gn_g, skip, ln_g, ln_b, w_route, b_route):
    n_tok = x.shape[0]
    tm = 256
    row = lambda col: pl.BlockSpec((tm, D_MODEL), lambda i, col=col: (i, col))
    full = lambda shape: pl.BlockSpec(shape, lambda i: tuple(0 for _ in shape))
    vec = full((1, D_MODEL))
    return pl.pallas_call(
        _post_kernel,
        grid=(n_tok // tm,),
        in_specs=[row(0), row(0), row(0), row(1), row(2), row(3), row(0),
                  full((D_MODEL, D_MODEL)), full((D_MODEL, D_MODEL)), full((D_MODEL, D_MODEL)),
                  vec, vec, vec, vec,
                  full((3, D_MODEL, LANES)), full((1, LANES))],
        out_specs=[row(0), pl.BlockSpec((tm, LANES), lambda i: (i, 0))],
        out_shape=[jax.ShapeDtypeStruct((n_tok, D_MODEL), F32),
                   jax.ShapeDtypeStruct((n_tok, LANES), F32)],
        compiler_params=_cparams(("parallel",)),
    )(attn, hn, xc, rest, rest, rest, x, w_pa, w_pb, w_out, gn_g, skip, ln_g, ln_b,
      w_route, b_route)


def _moe_kernel(bexp_ref, nused_ref,
                tok_hbm, dst_hbm, gate_ref, x_hbm, wg_ref, wu_ref, wd_ref, y_hbm,
                tok_smem, dst_smem, xbuf, obuf, idx_sem, gat_sem, sca_sem):
    i = pl.program_id(0)
    n_used = nused_ref[0]
    p = i % 2

    def index_copies(blk):
        slot = blk % 3
        return (pltpu.make_async_copy(tok_hbm.at[blk], tok_smem.at[slot], idx_sem.at[0, slot]),
                pltpu.make_async_copy(dst_hbm.at[blk], dst_smem.at[slot], idx_sem.at[1, slot]))

    def start_indices(blk):
        for cp in index_copies(blk):
            cp.start()

    def wait_indices(blk):
        for cp in index_copies(blk):
            cp.wait()

    def start_gather(blk, slot):
        islot = blk % 3

        def body(r, carry):
            t = tok_smem[islot, r]
            pltpu.make_async_copy(x_hbm.at[pl.ds(t, 1)], xbuf.at[slot, pl.ds(r, 1)],
                                  gat_sem.at[slot]).start()
            return carry
        lax.fori_loop(0, MOE_ROWS, body, 0, unroll=8)

    def wait_gather(slot):
        pltpu.make_async_copy(x_hbm.at[pl.ds(0, MOE_ROWS)], xbuf.at[slot],
                              gat_sem.at[slot]).wait()

    def start_scatter(blk, slot):
        islot = blk % 3

        def body(r, carry):
            t = dst_smem[islot, r]
            pltpu.make_async_copy(obuf.at[slot, pl.ds(r, 1)], y_hbm.at[pl.ds(t, 1)],
                                  sca_sem.at[slot]).start()
            return carry
        lax.fori_loop(0, MOE_ROWS, body, 0, unroll=8)

    def wait_scatter(slot):
        pltpu.make_async_copy(obuf.at[slot], y_hbm.at[pl.ds(0, MOE_ROWS)],
                              sca_sem.at[slot]).wait()

    @pl.when(i == 0)
    def _():
        start_indices(0)
        wait_indices(0)
        start_gather(0, 0)

        @pl.when(1 < n_used)
        def _():
            start_indices(1)

    @pl.when(i + 1 < n_used)
    def _():
        wait_indices(i + 1)
        start_gather(i + 1, 1 - p)

    @pl.when(i + 2 < n_used)
    def _():
        start_indices(i + 2)

    @pl.when(i < n_used)
    def _():
        wait_gather(p)

        @pl.when(i >= 2)
        def _():
            wait_scatter(p)

        xb = xbuf[p].astype(BF16)
        hg = jnp.dot(xb, wg_ref[...], preferred_element_type=F32)
        hu = jnp.dot(xb, wu_ref[...], preferred_element_type=F32)
        hh = (hg * jax.nn.sigmoid(hg) * hu).astype(BF16)
        out = jnp.dot(hh, wd_ref[...], preferred_element_type=F32)
        obuf[p] = out * gate_ref[...]
        start_scatter(i, p)

        @pl.when(i == n_used - 1)
        def _():
            wait_scatter(p)

            @pl.when(i >= 1)
            def _():
                wait_scatter(1 - p)


def _moe(x1, slot_tok, slot_dst, slot_gate, blk_exp, n_used, we_g, we_u, we_d):
    n_tok = x1.shape[0]
    n_blk = blk_exp.shape[0]
    wspec_in = pl.BlockSpec((None, D_MODEL, D_EXPERT), lambda i, be, nu: (be[i], 0, 0))
    wspec_out = pl.BlockSpec((None, D_EXPERT, D_MODEL), lambda i, be, nu: (be[i], 0, 0))
    grid_spec = pltpu.PrefetchScalarGridSpec(
        num_scalar_prefetch=2,
        grid=(n_blk,),
        in_specs=[
            pl.BlockSpec(memory_space=pl.ANY),
            pl.BlockSpec(memory_space=pl.ANY),
            pl.BlockSpec((MOE_ROWS, 1), lambda i, be, nu: (i, 0)),
            pl.BlockSpec(memory_space=pl.ANY),
            wspec_in, wspec_in, wspec_out,
        ],
        out_specs=pl.BlockSpec(memory_space=pl.ANY),
        scratch_shapes=[
            pltpu.SMEM((3, MOE_ROWS), jnp.int32),
            pltpu.SMEM((3, MOE_ROWS), jnp.int32),
            pltpu.VMEM((2, MOE_ROWS, D_MODEL), F32),
            pltpu.VMEM((2, MOE_ROWS, D_MODEL), F32),
            pltpu.SemaphoreType.DMA((2, 3)),
            pltpu.SemaphoreType.DMA((2,)),
            pltpu.SemaphoreType.DMA((2,)),
        ],
    )
    return pl.pallas_call(
        _moe_kernel,
        grid_spec=grid_spec,
        out_shape=jax.ShapeDtypeStruct((TOP_K * n_tok + 2 * MOE_ROWS, D_MODEL), F32),
        compiler_params=_cparams(("arbitrary",)),
    )(blk_exp, n_used, slot_tok.reshape(n_blk, MOE_ROWS),
      slot_dst.reshape(n_blk, MOE_ROWS), slot_gate.reshape(n_blk * MOE_ROWS, 1), x1,
      we_g, we_u, we_d)


def _route_plan(route, n_tok):
    n_assign = n_tok * TOP_K
    eid = route[:, 0:TOP_K].astype(jnp.int32).reshape(n_assign)
    gate = route[:, TOP_K:2 * TOP_K].reshape(n_assign)
    order = jnp.argsort(eid).astype(jnp.int32)
    counts = jnp.zeros((N_EXPERTS,), jnp.int32).at[eid].add(1)
    offsets = jnp.cumsum(counts) - counts
    padded = (counts + MOE_ROWS - 1) // MOE_ROWS * MOE_ROWS
    pad_end = jnp.cumsum(padded)
    pad_start = pad_end - padded
    n_blk = (n_assign + MOE_ROWS - 1) // MOE_ROWS + N_EXPERTS
    blk_start = jnp.arange(n_blk, dtype=jnp.int32) * MOE_ROWS
    blk_exp = jnp.minimum(jnp.sum(pad_end[None, :] <= blk_start[:, None], axis=1),
                          N_EXPERTS - 1).astype(jnp.int32)
    row = jnp.arange(MOE_ROWS, dtype=jnp.int32)[None, :]
    j = (blk_start - pad_start[blk_exp])[:, None] + row
    valid = j < counts[blk_exp][:, None]
    src = jnp.clip(offsets[blk_exp][:, None] + j, 0, n_assign - 1)
    assign = order[src]
    tok = assign // TOP_K
    dummy = TOP_K * n_tok + (jnp.arange(n_blk, dtype=jnp.int32) % 2)[:, None] * MOE_ROWS + row
    slot_tok = jnp.where(valid, tok, 0)
    slot_dst = jnp.where(valid, (assign % TOP_K) * n_tok + tok, dummy)
    slot_gate = jnp.where(valid, gate[assign], 0.0)
    n_used = (pad_end[-1:] // MOE_ROWS).astype(jnp.int32)
    return slot_tok, slot_dst, slot_gate, blk_exp, n_used


def _combine_kernel(x_ref, y0_ref, y1_ref, g_ref, b_ref, o_ref):
    y = DN_ALPHA * x_ref[...] + (y0_ref[...] + y1_ref[...])
    o_ref[...] = _layer_norm(y, g_ref[...], b_ref[...])


def _combine(x1, y, ln_g, ln_b):
    n_tok = x1.shape[0]
    tm = 256
    n_t = n_tok // tm
    vec = pl.BlockSpec((1, D_MODEL), lambda i: (0, 0))
    return pl.pallas_call(
        _combine_kernel,
        grid=(n_t,),
        in_specs=[pl.BlockSpec((tm, D_MODEL), lambda i: (i, 0)),
                  pl.BlockSpec((tm, D_MODEL), lambda i: (i, 0)),
                  pl.BlockSpec((tm, D_MODEL), lambda i: (i + n_t, 0)),
                  vec, vec],
        out_specs=pl.BlockSpec((tm, D_MODEL), lambda i: (i, 0)),
        out_shape=jax.ShapeDtypeStruct((n_tok, D_MODEL), F32),
        compiler_params=_cparams(("parallel",)),
    )(x1, y, y, ln_g, ln_b)


def _rope_tables(seq_len):
    half = DA_HEAD_DIM // 2
    inv = ROPE_THETA ** (-jnp.arange(half, dtype=F32) * (2.0 / DA_HEAD_DIM))
    ang = jnp.arange(seq_len, dtype=F32)[:, None] * inv[None, :]
    cos = jnp.tile(jnp.cos(ang), (1, LANES // half))
    sin = jnp.tile(jnp.sin(ang), (1, LANES // half))
    lane = jnp.arange(LANES)
    sign = jnp.where((lane % DA_HEAD_DIM) < half, -1.0, 1.0).astype(F32)
    return cos, sin * sign[None, :]


def _block_diag_tiles(w):
    per_tile = MXU_DIM // ML_PROJ_BLOCK
    n_tiles = w.shape[0] // per_tile
    w4 = w.reshape(n_tiles, per_tile, ML_PROJ_BLOCK, ML_PROJ_BLOCK)
    eye = jnp.eye(per_tile, dtype=w.dtype)
    bd = jnp.einsum('jgio,gh->jgiho', w4, eye)
    return bd.reshape(n_tiles, MXU_DIM, MXU_DIM).astype(BF16)


def _gate_perm():
    idx = []
    for h in range(ML_HEADS):
        for d in range(2):
            for kind in range(2):
                idx.append(d * 2 * ML_HEADS + kind * ML_HEADS + h)
    return jnp.array(idx, dtype=jnp.int32)


def _layer(x, n_seq, seq_len, lambda_init, cos, sin, p):
    n_tok = x.shape[0]
    qk_w = DA_HEADS * 2 * DA_HEAD_DIM
    v_w = DA_HEADS * DA_V_DIM
    w_in = p['w_in'].astype(BF16)
    qkv = _inproj(x, w_in[:, :2 * qk_w + v_w], cos, sin, seq_len, n_rope=2 * qk_w,
                  n_scaled=qk_w, scale=DA_HEAD_DIM ** -0.5 * math.log2(math.e), out_dtype=BF16)
    rest = _inproj(x, w_in[:, 2 * qk_w + v_w:], cos, sin, seq_len, n_rope=0, n_scaled=0,
                   scale=1.0, out_dtype=F32)

    lam = (jnp.exp(jnp.sum(p['lq1'] * p['lk1'])) - jnp.exp(jnp.sum(p['lq2'] * p['lk2']))
           + lambda_init)
    scalars = jnp.stack([lam, jnp.asarray(1.0 - lambda_init, F32)]).astype(F32)
    attn = _attention(qkv, scalars, p['subln_g'].reshape(1, DA_V_DIM), n_seq, seq_len)

    perm = _gate_perm()
    n_gate = 4 * ML_HEADS
    wg = p['w_gate'][:, perm].reshape(3, D_MODEL, n_gate)
    wg = jnp.pad(wg, ((0, 0), (0, 0), (0, LANES - n_gate))).astype(BF16)
    bg = jnp.pad(p['b_gate'][perm], (0, LANES - n_gate)).reshape(1, LANES)
    q, k, v, xc, gates = _mlstm_pre(
        rest, p['conv_w'], p['conv_b'].reshape(1, D_MODEL), _block_diag_tiles(p['wq']),
        _block_diag_tiles(p['wk']), _block_diag_tiles(p['wv']), wg, bg, n_seq, seq_len)
    g_col = gates[:, :, :n_gate].reshape(n_seq, seq_len, ML_HEADS, 4)
    g_col = jnp.transpose(g_col, (0, 2, 1, 3))
    g_row = jnp.transpose(g_col, (0, 1, 3, 2))
    hn = _mlstm(q, k, v, g_row, g_col, n_seq, seq_len)

    n_route = N_GROUPS + N_EXPERTS
    w_route = jnp.concatenate([p['rg_w'], p['re_w']], axis=1)
    w_route = jnp.pad(w_route, ((0, 0), (0, LANES - n_route)))
    w_route = jnp.stack(_split3(w_route))
    b_route = jnp.pad(jnp.concatenate([p['rg_b'], p['re_b']]), (0, LANES - n_route))
    vec = lambda a: a.reshape(1, D_MODEL)
    x1, route = _post(attn, hn, xc, rest, x, p['w_pa'].astype(BF16), p['w_pb'].astype(BF16),
                      p['w_out'].astype(BF16), vec(p['gn_g']), vec(p['skip']), vec(p['ln1_g']),
                      vec(p['ln1_b']), w_route, b_route.reshape(1, LANES))

    plan = _route_plan(route, n_tok)
    y = _moe(x1, *plan, p['we_g'].astype(BF16), p['we_u'].astype(BF16), p['we_d'].astype(BF16))
    return _combine(x1, y, vec(p['ln2_g']), vec(p['ln2_b']))


def kernel(x_prompt, x_sample, w_in, da_lambda_q1, da_lambda_k1, da_lambda_q2, da_lambda_k2, da_subln_g, ml_conv_w, ml_conv_b, ml_wq, ml_wk, ml_wv, ml_w_gate, ml_b_gate, ml_skip, ml_gn_g, w_pa, w_pb, w_out, ln1_g, ln1_b, router_group_w, router_group_b, router_expert_w, router_expert_b, w_e_gate, w_e_up, w_e_down, ln2_g, ln2_b):
    n_p, seq_len, d = x_prompt.shape
    n_s = x_sample.shape[0]
    assert x_sample.shape[1] == seq_len and d == D_MODEL and seq_len % CHUNK == 0
    n_seq = n_p + n_s
    x = jnp.concatenate([x_prompt, x_sample], axis=0).reshape(n_seq * seq_len, d)
    cos, sin = _rope_tables(seq_len)
    stacked = dict(w_in=w_in, lq1=da_lambda_q1, lk1=da_lambda_k1, lq2=da_lambda_q2,
                   lk2=da_lambda_k2, subln_g=da_subln_g, conv_w=ml_conv_w, conv_b=ml_conv_b,
                   wq=ml_wq, wk=ml_wk, wv=ml_wv, w_gate=ml_w_gate, b_gate=ml_b_gate,
                   skip=ml_skip, gn_g=ml_gn_g, w_pa=w_pa, w_pb=w_pb, w_out=w_out,
                   ln1_g=ln1_g, ln1_b=ln1_b, rg_w=router_group_w, rg_b=router_group_b,
                   re_w=router_expert_w, re_b=router_expert_b, we_g=w_e_gate, we_u=w_e_up,
                   we_d=w_e_down, ln2_g=ln2_g, ln2_b=ln2_b)
    for l in range(w_in.shape[0]):
        lambda_init = 0.8 - 0.6 * math.exp(-0.3 * l)
        x = _layer(x, n_seq, seq_len, lambda_init, cos, sin, {k: a[l] for k, a in stacked.items()})
    y = x.reshape(n_seq, seq_len, d)
    return (y[:n_p], y[n_p:])
```

```python
import functools
import math

import jax
import jax.numpy as jnp
from jax import lax
from jax.experimental import pallas as pl
from jax.experimental.pallas import tpu as pltpu

F32 = jnp.float32
BF16 = jnp.bfloat16

D_MODEL = 1024
DEPTH = 4
DA_HEADS = 8
DA_HEAD_DIM = 64
DA_V_DIM = 128
ROPE_THETA = 10000.0
ML_HEADS = 4
ML_HEAD_DIM = 256
ML_PROJ_BLOCK = 4
N_GROUPS = 4
EXPERTS_PER_GROUP = 8
N_EXPERTS = 32
TOP_K = 2
D_EXPERT = 512
DN_ALPHA = (2 * DEPTH) ** 0.25
LN_EPS = 1e-5

LANES = 128
MXU_DIM = 256
ROW_CHUNKS = D_MODEL // LANES
CHUNK = 128
MOE_ROWS = 512
VMEM_LIMIT = 56 * 1024 * 1024

NEG_BIG = -1e30


def _cparams(sem):
    return pltpu.CompilerParams(dimension_semantics=sem, vmem_limit_bytes=VMEM_LIMIT)


def _inproj_kernel(x_ref, w_ref, cos_ref, sin_ref, o_ref, *, n_rope, n_scaled, scale, cw):
    xb = x_ref[...].astype(BF16)
    n_cols = o_ref.shape[1]
    tm = x_ref.shape[0]
    if n_rope:
        cos = cos_ref[...]
        sin = sin_ref[...]
        lane = lax.broadcasted_iota(jnp.int32, (tm, LANES), 1)
        first_half = (lane % DA_HEAD_DIM) < (DA_HEAD_DIM // 2)
    for c in range(n_cols // cw):
        acc = jnp.dot(xb, w_ref[:, c * cw:(c + 1) * cw], preferred_element_type=F32)
        for s in range(cw // LANES):
            col = c * cw + s * LANES
            t = acc[:, s * LANES:(s + 1) * LANES]
            if col < n_rope:
                rot = jnp.where(first_half, pltpu.roll(t, LANES - DA_HEAD_DIM // 2, 1),
                                pltpu.roll(t, DA_HEAD_DIM // 2, 1))
                t = t * cos + rot * sin
                if col < n_scaled:
                    t = t * scale
            o_ref[:, col:col + LANES] = t.astype(o_ref.dtype)


def _inproj(x, w, cos, sin, seq_len, *, n_rope, n_scaled, scale, out_dtype):
    n_tok, d = x.shape
    n_cols = w.shape[1]
    tm = min(512, seq_len)
    per_seq = seq_len // tm
    kern = functools.partial(_inproj_kernel, n_rope=n_rope, n_scaled=n_scaled, scale=scale, cw=512)
    return pl.pallas_call(
        kern,
        grid=(n_tok // tm,),
        in_specs=[
            pl.BlockSpec((tm, d), lambda i: (i, 0)),
            pl.BlockSpec((d, n_cols), lambda i: (0, 0)),
            pl.BlockSpec((tm, LANES), lambda i: (i % per_seq, 0)),
            pl.BlockSpec((tm, LANES), lambda i: (i % per_seq, 0)),
        ],
        out_specs=pl.BlockSpec((tm, n_cols), lambda i: (i, 0)),
        out_shape=jax.ShapeDtypeStruct((n_tok, n_cols), out_dtype),
        compiler_params=_cparams(("parallel",)),
    )(x, w, cos, sin)


def _attn_kernel(sc_ref, q_ref, k_ref, v_ref, g_ref, o_ref, vx_ref, *, bq):
    seq_len = q_ref.shape[0]
    lam = sc_ref[0]
    out_scale = sc_ref[1]
    k = k_ref[...]
    vx_ref[:, :DA_V_DIM] = v_ref[...]
    vx_ref[:, DA_V_DIM:] = jnp.ones((seq_len, DA_V_DIM), vx_ref.dtype)
    vx = vx_ref[...]
    g = g_ref[...] * out_scale
    lane = lax.broadcasted_iota(jnp.int32, (bq, LANES), 1)
    is_first = lane < DA_HEAD_DIM
    dn = (((1,), (1,)), ((), ()))

    def softmax_av(qm):
        s = lax.dot_general(qm, k, dn, preferred_element_type=F32)
        p = jnp.exp2(s - jnp.max(s, axis=-1, keepdims=True)).astype(BF16)
        ox = jnp.dot(p, vx, preferred_element_type=F32)
        return ox[:, :DA_V_DIM] / ox[:, DA_V_DIM:]

    for i in range(seq_len // bq):
        rows = slice(i * bq, (i + 1) * bq)
        qb = q_ref[rows, :]
        zero = jnp.zeros_like(qb)
        o = (softmax_av(jnp.where(is_first, qb, zero))
             - lam * softmax_av(jnp.where(is_first, zero, qb)))
        o = o * lax.rsqrt(jnp.mean(o * o, axis=-1, keepdims=True) + LN_EPS)
        o_ref[rows, :] = (o * g).astype(o_ref.dtype)


def _attention(qkv, scalars, subln_g, n_seq, seq_len):
    n_tok = qkv.shape[0]
    bq = min(128, seq_len)
    return pl.pallas_call(
        functools.partial(_attn_kernel, bq=bq),
        grid=(n_seq, DA_HEADS),
        in_specs=[
            pl.BlockSpec(memory_space=pltpu.SMEM),
            pl.BlockSpec((seq_len, LANES), lambda b, h: (b, h)),
            pl.BlockSpec((seq_len, LANES), lambda b, h: (b, DA_HEADS + h)),
            pl.BlockSpec((seq_len, LANES), lambda b, h: (b, 2 * DA_HEADS + h)),
            pl.BlockSpec((1, DA_V_DIM), lambda b, h: (0, 0)),
        ],
        out_specs=pl.BlockSpec((seq_len, DA_V_DIM), lambda b, h: (b, h)),
        out_shape=jax.ShapeDtypeStruct((n_tok, DA_HEADS * DA_V_DIM), BF16),
        scratch_shapes=[pltpu.VMEM((seq_len, 2 * DA_V_DIM), BF16)],
        compiler_params=_cparams(("parallel", "parallel")),
    )(scalars, qkv, qkv, qkv, subln_g)


def _split3(x):
    x1 = x.astype(BF16)
    r1 = x - x1.astype(F32)
    x2 = r1.astype(BF16)
    x3 = (r1 - x2.astype(F32)).astype(BF16)
    return x1, x2, x3


def _mlpre_kernel(xm_ref, cw_ref, cb_ref, wq_ref, wk_ref, wv_ref, wg_ref, bg_ref,
                  q_ref, k_ref, v_ref, xc_ref, g_ref, gacc_ref):
    j = pl.program_id(1)
    seq_len = xm_ref.shape[0]
    xm = xm_ref[...].astype(F32)
    row = lax.broadcasted_iota(jnp.int32, xm.shape, 0)
    prev = jnp.where(row == 0, 0.0, pltpu.roll(xm, 1, 0))
    nxt = jnp.where(row == seq_len - 1, 0.0, pltpu.roll(xm, seq_len - 1, 0))
    xc = cb_ref[...] + prev * cw_ref[0:1, :] + xm * cw_ref[1:2, :] + nxt * cw_ref[2:3, :]
    xc = xc * jax.nn.sigmoid(xc)
    xcb = xc.astype(BF16)
    xc_ref[...] = xcb
    q = jnp.dot(xcb, wq_ref[...], preferred_element_type=F32)
    k = jnp.dot(xcb, wk_ref[...], preferred_element_type=F32)
    v = jnp.dot(xm_ref[...], wv_ref[...], preferred_element_type=F32)
    qb = q.astype(BF16)
    kb = k.astype(BF16)
    vb = v.astype(BF16)
    q_ref[...] = (q * (ML_HEAD_DIM ** -0.5)).astype(BF16)
    k_ref[...] = kb
    v_ref[...] = vb
    part = (jnp.dot(qb, wg_ref[0], preferred_element_type=F32)
            + jnp.dot(kb, wg_ref[1], preferred_element_type=F32)
            + jnp.dot(vb, wg_ref[2], preferred_element_type=F32))

    @pl.when(j == 0)
    def _():
        gacc_ref[...] = part

    @pl.when(j > 0)
    def _():
        gacc_ref[...] += part

    @pl.when(j == pl.num_programs(1) - 1)
    def _():
        r = lax.broadcasted_iota(jnp.int32, (CHUNK, CHUNK), 0)
        c = lax.broadcasted_iota(jnp.int32, (CHUNK, CHUNK), 1)
        tri = jnp.where(c <= r, 1.0, 0.0).astype(BF16)
        lane = lax.broadcasted_iota(jnp.int32, (CHUNK, LANES), 1)
        is_forget = (lane % 2) == 1
        is_bwd = ((lane // 2) % 2) == 1
        bg = bg_ref[...]
        for ci in range(seq_len // CHUNK):
            rows = slice(ci * CHUNK, (ci + 1) * CHUNK)
            pre = gacc_ref[rows, :] + bg
            lf = jnp.minimum(pre, 0.0) - jnp.log1p(jnp.exp(-jnp.abs(pre)))
            l1, l2, l3 = _split3(lf)
            pref = (jnp.dot(tri, l1, preferred_element_type=F32)
                    + jnp.dot(tri, l2, preferred_element_type=F32)
                    + jnp.dot(tri, l3, preferred_element_type=F32))
            total = pref[CHUNK - 1:CHUNK, :]
            suff = total - pref + lf
            cum = jnp.where(is_bwd, suff, pref)
            g_ref[rows, :] = jnp.where(is_forget, cum, pre)


def _mlstm_pre(rest, conv_w, conv_b, wq_bd, wk_bd, wv_bd, wg, bg, n_seq, seq_len):
    n_tok = rest.shape[0]
    n_ct = D_MODEL // MXU_DIM
    tile = pl.BlockSpec((seq_len, MXU_DIM), lambda b, j: (b, j))
    wspec = pl.BlockSpec((None, MXU_DIM, MXU_DIM), lambda b, j: (j, 0, 0))
    act = jax.ShapeDtypeStruct((n_tok, D_MODEL), BF16)
    return pl.pallas_call(
        _mlpre_kernel,
        grid=(n_seq, n_ct),
        in_specs=[
            tile,
            pl.BlockSpec((3, MXU_DIM), lambda b, j: (0, j)),
            pl.BlockSpec((1, MXU_DIM), lambda b, j: (0, j)),
            wspec, wspec, wspec,
            pl.BlockSpec((3, MXU_DIM, LANES), lambda b, j: (0, j, 0)),
            pl.BlockSpec((1, LANES), lambda b, j: (0, 0)),
        ],
        out_specs=[tile, tile, tile, tile,
                   pl.BlockSpec((None, seq_len, LANES), lambda b, j: (b, 0, 0))],
        out_shape=[act, act, act, act,
                   jax.ShapeDtypeStruct((n_seq, seq_len, LANES), F32)],
        scratch_shapes=[pltpu.VMEM((seq_len, LANES), F32)],
        compiler_params=_cparams(("parallel", "arbitrary")),
    )(rest, conv_w, conv_b, wq_bd, wk_bd, wv_bd, wg, bg)


def _mlstm_kernel(q_ref, k_ref, v_ref, gr_ref, gc_ref, o_ref, c_ref, n_ref, m_ref, h_ref):
    seq_len = q_ref.shape[0]
    n_chunks = seq_len // CHUNK
    r = lax.broadcasted_iota(jnp.int32, (CHUNK, CHUNK), 0)
    c = lax.broadcasted_iota(jnp.int32, (CHUNK, CHUNK), 1)
    dn_t = (((1,), (1,)), ((), ()))
    c_ref[...] = jnp.zeros_like(c_ref)
    n_ref[...] = jnp.zeros_like(n_ref)
    m_ref[...] = jnp.zeros_like(m_ref)

    def chunk_step(direction, ci):
        mask = (c <= r) if direction == 0 else (c >= r)
        last = CHUNK - 1 if direction == 0 else 0
        rows = pl.ds(pl.multiple_of(ci * CHUNK, CHUNK), CHUNK)
        qc = q_ref[rows, :]
        kc = k_ref[rows, :]
        vc = v_ref[rows, :]
        ig_r = gr_ref[2 * direction:2 * direction + 1, rows]
        b_r = gr_ref[2 * direction + 1:2 * direction + 2, rows]
        ig_c = gc_ref[rows, 2 * direction:2 * direction + 1]
        b_c = gc_ref[rows, 2 * direction + 1:2 * direction + 2]
        m_st = m_ref[direction]
        c_st = c_ref[direction]
        n_st = n_ref[direction]
        logd = jnp.where(mask, b_c - b_r + ig_r, -jnp.inf)
        m_inter = b_c + m_st
        m_j = jnp.maximum(jnp.max(logd, axis=-1, keepdims=True), m_inter)
        qk = lax.dot_general(qc, kc, dn_t, preferred_element_type=F32)
        sw = qk * jnp.exp(logd - m_j)
        inter = jnp.exp(m_inter - m_j)
        num = (jnp.dot(sw.astype(BF16), vc, preferred_element_type=F32)
               + inter * jnp.dot(qc, c_st.astype(BF16), preferred_element_type=F32))
        den = (jnp.sum(sw, axis=-1, keepdims=True)
               + inter * jnp.sum(qc.astype(F32) * n_st, axis=-1, keepdims=True))
        h = num / jnp.maximum(jnp.abs(den), jnp.exp(-m_j))
        b_last = b_c[last:last + 1, :]
        logw = b_last - b_c + ig_c
        m_new = jnp.maximum(b_last + m_st, jnp.max(logw, axis=0, keepdims=True))
        wts = jnp.exp(logw - m_new)
        keep = jnp.exp(b_last + m_st - m_new)
        kw = kc.astype(F32) * wts
        c_ref[direction] = keep * c_st + jnp.dot(kw.T.astype(BF16), vc,
                                                 preferred_element_type=F32)
        n_ref[direction] = keep * n_st + jnp.sum(kw, axis=0, keepdims=True)
        m_ref[direction] = m_new
        h_ref[direction, rows, :] = h

    def sweep(step, carry):
        chunk_step(0, step)
        chunk_step(1, n_chunks - 1 - step)
        return carry

    lax.fori_loop(0, n_chunks, sweep, 0)

    def normalize(ci, carry):
        rows = pl.ds(pl.multiple_of(ci * CHUNK, CHUNK), CHUNK)
        h = h_ref[0, rows, :] + h_ref[1, rows, :]
        mu = jnp.mean(h, axis=-1, keepdims=True)
        hc = h - mu
        var = jnp.mean(hc * hc, axis=-1, keepdims=True)
        o_ref[rows, :] = (hc * lax.rsqrt(var + LN_EPS)).astype(o_ref.dtype)
        return carry

    lax.fori_loop(0, n_chunks, normalize, 0)


def _mlstm(q, k, v, g_row, g_col, n_seq, seq_len):
    n_tok = q.shape[0]
    tile = pl.BlockSpec((seq_len, ML_HEAD_DIM), lambda b, h: (b, h))
    return pl.pallas_call(
        _mlstm_kernel,
        grid=(n_seq, ML_HEADS),
        in_specs=[
            tile, tile, tile,
            pl.BlockSpec((None, None, 4, seq_len), lambda b, h: (b, h, 0, 0)),
            pl.BlockSpec((None, None, seq_len, 4), lambda b, h: (b, h, 0, 0)),
        ],
        out_specs=tile,
        out_shape=jax.ShapeDtypeStruct((n_tok, D_MODEL), BF16),
        scratch_shapes=[
            pltpu.VMEM((2, ML_HEAD_DIM, ML_HEAD_DIM), F32),
            pltpu.VMEM((2, 1, ML_HEAD_DIM), F32),
            pltpu.VMEM((2, 1, 1), F32),
            pltpu.VMEM((2, seq_len, ML_HEAD_DIM), F32),
        ],
        compiler_params=_cparams(("parallel", "parallel")),
    )(q, k, v, g_row, g_col)


def _layer_norm(y, g, b):
    mu = jnp.mean(y, axis=-1, keepdims=True)
    yc = y - mu
    var = jnp.mean(yc * yc, axis=-1, keepdims=True)
    return yc * lax.rsqrt(var + LN_EPS) * g + b


def _route(logits):
    lane = lax.broadcasted_iota(jnp.int32, logits.shape, 1)
    big = jnp.int32(4 * LANES)
    gl = jnp.where(lane < N_GROUPS, logits, NEG_BIG)
    gmax = jnp.max(gl, axis=-1, keepdims=True)
    gsum = jnp.sum(jnp.where(lane < N_GROUPS, jnp.exp(gl - gmax), 0.0), axis=-1, keepdims=True)
    g_sel = jnp.min(jnp.where(gl == gmax, lane, big), axis=-1, keepdims=True)
    g_prob = 1.0 / gsum
    lo = N_GROUPS + EXPERTS_PER_GROUP * g_sel
    in_group = jnp.logical_and(lane >= lo, lane < lo + EXPERTS_PER_GROUP)
    el = jnp.where(in_group, logits, NEG_BIG)
    emax = jnp.max(el, axis=-1, keepdims=True)
    esum = jnp.sum(jnp.where(in_group, jnp.exp(el - emax), 0.0), axis=-1, keepdims=True)
    i1 = jnp.min(jnp.where(el == emax, lane, big), axis=-1, keepdims=True)
    el2 = jnp.where(lane == i1, NEG_BIG, el)
    emax2 = jnp.max(el2, axis=-1, keepdims=True)
    i2 = jnp.min(jnp.where(el2 == emax2, lane, big), axis=-1, keepdims=True)
    p1 = 1.0 / esum
    p2 = jnp.exp(emax2 - emax) / esum
    psum = p1 + p2
    gate1 = g_prob * p1 / psum
    gate2 = g_prob * p2 / psum
    e1 = (i1 - N_GROUPS).astype(F32)
    e2 = (i2 - N_GROUPS).astype(F32)
    return jnp.where(lane == 0, e1, jnp.where(lane == 1, e2, jnp.where(lane == 2, gate1, gate2)))


def _post_kernel(attn_ref, hn_ref, xc_ref, z_ref, ga_ref, gb_ref, x_ref,
                 wpa_ref, wpb_ref, wout_ref, gn_ref, skip_ref, lg_ref, lb_ref, wr_ref, br_ref,
                 x1t_ref, route_ref, *, sub):
    tm = x_ref.shape[0]
    for s in range(tm // sub):
        rows = slice(s * sub, (s + 1) * sub)
        z = z_ref[rows, :].astype(F32)
        ml = ((hn_ref[rows, :].astype(F32) * gn_ref[...]
               + skip_ref[...] * xc_ref[rows, :].astype(F32)) * (z * jax.nn.sigmoid(z)))
        a_out = jnp.dot(attn_ref[rows, :], wpa_ref[...], preferred_element_type=F32)
        m_out = jnp.dot(ml.astype(BF16), wpb_ref[...], preferred_element_type=F32)
        mixed = (jax.nn.sigmoid(ga_ref[rows, :].astype(F32)) * a_out
                 + jax.nn.sigmoid(gb_ref[rows, :].astype(F32)) * m_out)
        y = DN_ALPHA * x_ref[rows, :] + jnp.dot(mixed.astype(BF16), wout_ref[...],
                                               preferred_element_type=F32)
        x1 = _layer_norm(y, lg_ref[...], lb_ref[...])
        for c in range(ROW_CHUNKS):
            x1t_ref[pl.ds(s * sub * ROW_CHUNKS + c, sub, stride=ROW_CHUNKS), :] = (
                x1[:, c * LANES:(c + 1) * LANES])
        xa, xb, _ = _split3(x1)
        logits = (br_ref[...] + jnp.dot(xa, wr_ref[0], preferred_element_type=F32)
                  + jnp.dot(xb, wr_ref[0], preferred_element_type=F32)
                  + jnp.dot(xa, wr_ref[1], preferred_element_type=F32))
        route_ref[rows, :] = _route(logits)


def _post(attn, hn, xc, rest, x, w_pa, w_pb, w_out, gn_g, skip, ln_g, ln_b, w_route, b_route):
    n_tok = x.shape[0]
    sub = 256
    tm = 2 * sub if n_tok % (2 * sub) == 0 else sub
    row = lambda col: pl.BlockSpec((tm, D_MODEL), lambda i, col=col: (i, col))
    full = lambda shape: pl.BlockSpec(shape, lambda i: tuple(0 for _ in shape))
    vec = full((1, D_MODEL))
    return pl.pallas_call(
        functools.partial(_post_kernel, sub=sub),
        grid=(n_tok // tm,),
        in_specs=[row(0), row(0), row(0), row(1), row(2), row(3), row(0),
                  full((D_MODEL, D_MODEL)), full((D_MODEL, D_MODEL)), full((D_MODEL, D_MODEL)),
                  vec, vec, vec, vec,
                  full((2, D_MODEL, LANES)), full((1, LANES))],
        out_specs=[pl.BlockSpec((tm * ROW_CHUNKS, LANES), lambda i: (i, 0)),
                   pl.BlockSpec((tm, LANES), lambda i: (i, 0))],
        out_shape=[jax.ShapeDtypeStruct((n_tok * ROW_CHUNKS, LANES), F32),
                   jax.ShapeDtypeStruct((n_tok, LANES), F32)],
        compiler_params=_cparams(("parallel",)),
    )(attn, hn, xc, rest, rest, rest, x, w_pa, w_pb, w_out, gn_g, skip, ln_g, ln_b,
      w_route, b_route)


def _moe_kernel(bexp_ref, nused_ref,
                tok_hbm, dst_hbm, gate_ref, x_hbm, wg_ref, wu_ref, wd_ref, y_hbm,
                tok_smem, dst_smem, xbuf, obuf, idx_sem, gat_sem, sca_sem):
    i = pl.program_id(0)
    n_used = nused_ref[0]
    p = i % 2

    def index_copies(blk):
        slot = blk % 3
        return (pltpu.make_async_copy(tok_hbm.at[blk], tok_smem.at[slot], idx_sem.at[0, slot]),
                pltpu.make_async_copy(dst_hbm.at[blk], dst_smem.at[slot], idx_sem.at[1, slot]))

    def start_indices(blk):
        for cp in index_copies(blk):
            cp.start()

    def wait_indices(blk):
        for cp in index_copies(blk):
            cp.wait()

    def tile_rows(r):
        return pl.ds(pl.multiple_of(r * ROW_CHUNKS, ROW_CHUNKS), ROW_CHUNKS)

    def start_gather(blk, slot):
        islot = blk % 3

        def body(r, carry):
            t = tok_smem[islot, r]
            pltpu.make_async_copy(x_hbm.at[t], xbuf.at[slot, tile_rows(r)],
                                  gat_sem.at[slot]).start()
            return carry
        lax.fori_loop(0, MOE_ROWS, body, 0, unroll=8)

    def wait_gather(slot):
        pltpu.make_async_copy(xbuf.at[slot], xbuf.at[slot], gat_sem.at[slot]).wait()

    def start_scatter(blk, slot):
        islot = blk % 3

        def body(r, carry):
            t = dst_smem[islot, r]
            pltpu.make_async_copy(obuf.at[slot, tile_rows(r)], y_hbm.at[t],
                                  sca_sem.at[slot]).start()
            return carry
        lax.fori_loop(0, MOE_ROWS, body, 0, unroll=8)

    def wait_scatter(slot):
        pltpu.make_async_copy(obuf.at[slot], obuf.at[slot], sca_sem.at[slot]).wait()

    @pl.when(i == 0)
    def _():
        start_indices(0)
        wait_indices(0)
        start_gather(0, 0)

        @pl.when(1 < n_used)
        def _():
            start_indices(1)

    @pl.when(i + 1 < n_used)
    def _():
        wait_indices(i + 1)
        start_gather(i + 1, 1 - p)

    @pl.when(i + 2 < n_used)
    def _():
        start_indices(i + 2)

    @pl.when(i < n_used)
    def _():
        wait_gather(p)

        @pl.when(i >= 2)
        def _():
            wait_scatter(p)

        xb = jnp.concatenate(
            [xbuf[p, pl.ds(c, MOE_ROWS, stride=ROW_CHUNKS), :] for c in range(ROW_CHUNKS)],
            axis=1).astype(BF16)
        hg = jnp.dot(xb, wg_ref[...], preferred_element_type=F32)
        hu = jnp.dot(xb, wu_ref[...], preferred_element_type=F32)
        hh = (hg * jax.nn.sigmoid(hg) * hu).astype(BF16)
        out = jnp.dot(hh, wd_ref[...], preferred_element_type=F32) * gate_ref[...]
        for c in range(ROW_CHUNKS):
            obuf[p, pl.ds(c, MOE_ROWS, stride=ROW_CHUNKS), :] = out[:, c * LANES:(c + 1) * LANES]
        start_scatter(i, p)

        @pl.when(i == n_used - 1)
        def _():
            wait_scatter(p)

            @pl.when(i >= 1)
            def _():
                wait_scatter(1 - p)


def _moe(x1t, slot_tok, slot_dst, slot_gate, blk_exp, n_used, we_g, we_u, we_d):
    n_tok = x1t.shape[0] // ROW_CHUNKS
    n_blk = blk_exp.shape[0]
    n_out = TOP_K * n_tok + 2 * MOE_ROWS
    wspec_in = pl.BlockSpec((None, D_MODEL, D_EXPERT), lambda i, be, nu: (be[i], 0, 0))
    wspec_out = pl.BlockSpec((None, D_EXPERT, D_MODEL), lambda i, be, nu: (be[i], 0, 0))
    grid_spec = pltpu.PrefetchScalarGridSpec(
        num_scalar_prefetch=2,
        grid=(n_blk,),
        in_specs=[
            pl.BlockSpec(memory_space=pl.ANY),
            pl.BlockSpec(memory_space=pl.ANY),
            pl.BlockSpec((MOE_ROWS, 1), lambda i, be, nu: (i, 0)),
            pl.BlockSpec(memory_space=pl.ANY),
            wspec_in, wspec_in, wspec_out,
        ],
        out_specs=pl.BlockSpec(memory_space=pl.ANY),
        scratch_shapes=[
            pltpu.SMEM((3, MOE_ROWS), jnp.int32),
            pltpu.SMEM((3, MOE_ROWS), jnp.int32),
            pltpu.VMEM((2, MOE_ROWS * ROW_CHUNKS, LANES), F32),
            pltpu.VMEM((2, MOE_ROWS * ROW_CHUNKS, LANES), F32),
            pltpu.SemaphoreType.DMA((2, 3)),
            pltpu.SemaphoreType.DMA((2,)),
            pltpu.SemaphoreType.DMA((2,)),
        ],
    )
    y = pl.pallas_call(
        _moe_kernel,
        grid_spec=grid_spec,
        out_shape=jax.ShapeDtypeStruct((n_out, ROW_CHUNKS, LANES), F32),
        compiler_params=_cparams(("arbitrary",)),
    )(blk_exp, n_used, slot_tok.reshape(n_blk, MOE_ROWS),
      slot_dst.reshape(n_blk, MOE_ROWS), slot_gate.reshape(n_blk * MOE_ROWS, 1),
      x1t.reshape(n_tok, ROW_CHUNKS, LANES), we_g, we_u, we_d)
    return y.reshape(n_out * ROW_CHUNKS, LANES)


def _route_plan(route, n_tok):
    n_assign = n_tok * TOP_K
    eid = route[:, 0:TOP_K].astype(jnp.int32).reshape(n_assign)
    gate = route[:, TOP_K:2 * TOP_K].reshape(n_assign)
    order = jnp.argsort(eid).astype(jnp.int32)
    counts = jnp.zeros((N_EXPERTS,), jnp.int32).at[eid].add(1)
    offsets = jnp.cumsum(counts) - counts
    padded = (counts + MOE_ROWS - 1) // MOE_ROWS * MOE_ROWS
    pad_end = jnp.cumsum(padded)
    pad_start = pad_end - padded
    n_blk = (n_assign + MOE_ROWS - 1) // MOE_ROWS + N_EXPERTS
    blk_start = jnp.arange(n_blk, dtype=jnp.int32) * MOE_ROWS
    blk_exp = jnp.minimum(jnp.sum(pad_end[None, :] <= blk_start[:, None], axis=1),
                          N_EXPERTS - 1).astype(jnp.int32)
    row = jnp.arange(MOE_ROWS, dtype=jnp.int32)[None, :]
    j = (blk_start - pad_start[blk_exp])[:, None] + row
    valid = j < counts[blk_exp][:, None]
    src = jnp.clip(offsets[blk_exp][:, None] + j, 0, n_assign - 1)
    assign = order[src]
    tok = assign // TOP_K
    dummy = TOP_K * n_tok + (jnp.arange(n_blk, dtype=jnp.int32) % 2)[:, None] * MOE_ROWS + row
    slot_tok = jnp.where(valid, tok, 0)
    slot_dst = jnp.where(valid, (assign % TOP_K) * n_tok + tok, dummy)
    slot_gate = jnp.where(valid, gate[assign], 0.0)
    n_used = (pad_end[-1:] // MOE_ROWS).astype(jnp.int32)
    return slot_tok, slot_dst, slot_gate, blk_exp, n_used


def _combine_kernel(x_ref, y0_ref, y1_ref, g_ref, b_ref, o_ref):
    tm = o_ref.shape[0]

    def chunk(ref, c):
        return ref[pl.ds(c, tm, stride=ROW_CHUNKS), :]

    y = jnp.concatenate(
        [DN_ALPHA * chunk(x_ref, c) + (chunk(y0_ref, c) + chunk(y1_ref, c))
         for c in range(ROW_CHUNKS)], axis=1)
    o_ref[...] = _layer_norm(y, g_ref[...], b_ref[...])


def _combine(x1t, y, ln_g, ln_b):
    n_tok = x1t.shape[0] // ROW_CHUNKS
    tm = 256
    n_t = n_tok // tm
    vec = pl.BlockSpec((1, D_MODEL), lambda i: (0, 0))
    tiles = lambda off: pl.BlockSpec((tm * ROW_CHUNKS, LANES), lambda i, off=off: (i + off, 0))
    return pl.pallas_call(
        _combine_kernel,
        grid=(n_t,),
        in_specs=[tiles(0), tiles(0), tiles(n_t), vec, vec],
        out_specs=pl.BlockSpec((tm, D_MODEL), lambda i: (i, 0)),
        out_shape=jax.ShapeDtypeStruct((n_tok, D_MODEL), F32),
        compiler_params=_cparams(("parallel",)),
    )(x1t, y, y, ln_g, ln_b)


def _rope_tables(seq_len):
    half = DA_HEAD_DIM // 2
    inv = ROPE_THETA ** (-jnp.arange(half, dtype=F32) * (2.0 / DA_HEAD_DIM))
    ang = jnp.arange(seq_len, dtype=F32)[:, None] * inv[None, :]
    cos = jnp.tile(jnp.cos(ang), (1, LANES // half))
    sin = jnp.tile(jnp.sin(ang), (1, LANES // half))
    lane = jnp.arange(LANES)
    sign = jnp.where((lane % DA_HEAD_DIM) < half, -1.0, 1.0).astype(F32)
    return cos, sin * sign[None, :]


def _block_diag_tiles(w):
    per_tile = MXU_DIM // ML_PROJ_BLOCK
    n_tiles = w.shape[0] // per_tile
    w4 = w.reshape(n_tiles, per_tile, ML_PROJ_BLOCK, ML_PROJ_BLOCK)
    eye = jnp.eye(per_tile, dtype=w.dtype)
    bd = jnp.einsum('jgio,gh->jgiho', w4, eye)
    return bd.reshape(n_tiles, MXU_DIM, MXU_DIM).astype(BF16)


def _gate_perm():
    idx = []
    for h in range(ML_HEADS):
        for d in range(2):
            for kind in range(2):
                idx.append(d * 2 * ML_HEADS + kind * ML_HEADS + h)
    return jnp.array(idx, dtype=jnp.int32)


def _layer(x, n_seq, seq_len, lambda_init, cos, sin, p):
    n_tok = x.shape[0]
    qk_w = DA_HEADS * 2 * DA_HEAD_DIM
    v_w = DA_HEADS * DA_V_DIM
    w_in = p['w_in'].astype(BF16)
    qkv = _inproj(x, w_in[:, :2 * qk_w + v_w], cos, sin, seq_len, n_rope=2 * qk_w,
                  n_scaled=qk_w, scale=DA_HEAD_DIM ** -0.5 * math.log2(math.e), out_dtype=BF16)
    rest = _inproj(x, w_in[:, 2 * qk_w + v_w:], cos, sin, seq_len, n_rope=0, n_scaled=0,
                   scale=1.0, out_dtype=BF16)

    lam = (jnp.exp(jnp.sum(p['lq1'] * p['lk1'])) - jnp.exp(jnp.sum(p['lq2'] * p['lk2']))
           + lambda_init)
    scalars = jnp.stack([lam, jnp.asarray(1.0 - lambda_init, F32)]).astype(F32)
    attn = _attention(qkv, scalars, p['subln_g'].reshape(1, DA_V_DIM), n_seq, seq_len)

    perm = _gate_perm()
    n_gate = 4 * ML_HEADS
    wg = p['w_gate'][:, perm].reshape(3, D_MODEL, n_gate)
    wg = jnp.pad(wg, ((0, 0), (0, 0), (0, LANES - n_gate))).astype(BF16)
    bg = jnp.pad(p['b_gate'][perm], (0, LANES - n_gate)).reshape(1, LANES)
    q, k, v, xc, gates = _mlstm_pre(
        rest, p['conv_w'], p['conv_b'].reshape(1, D_MODEL), _block_diag_tiles(p['wq']),
        _block_diag_tiles(p['wk']), _block_diag_tiles(p['wv']), wg, bg, n_seq, seq_len)
    g_col = gates[:, :, :n_gate].reshape(n_seq, seq_len, ML_HEADS, 4)
    g_col = jnp.transpose(g_col, (0, 2, 1, 3))
    g_row = jnp.transpose(g_col, (0, 1, 3, 2))
    hn = _mlstm(q, k, v, g_row, g_col, n_seq, seq_len)

    n_route = N_GROUPS + N_EXPERTS
    w_route = jnp.concatenate([p['rg_w'], p['re_w']], axis=1)
    w_route = jnp.pad(w_route, ((0, 0), (0, LANES - n_route)))
    w_route = jnp.stack(_split3(w_route)[:2])
    b_route = jnp.pad(jnp.concatenate([p['rg_b'], p['re_b']]), (0, LANES - n_route))
    vec = lambda a: a.reshape(1, D_MODEL)
    x1, route = _post(attn, hn, xc, rest, x, p['w_pa'].astype(BF16), p['w_pb'].astype(BF16),
                      p['w_out'].astype(BF16), vec(p['gn_g']), vec(p['skip']), vec(p['ln1_g']),
                      vec(p['ln1_b']), w_route, b_route.reshape(1, LANES))

    plan = _route_plan(route, n_tok)
    y = _moe(x1, *plan, p['we_g'].astype(BF16), p['we_u'].astype(BF16), p['we_d'].astype(BF16))
    return _combine(x1, y, vec(p['ln2_g']), vec(p['ln2_b']))


def kernel(x_prompt, x_sample, w_in, da_lambda_q1, da_lambda_k1, da_lambda_q2, da_lambda_k2, da_subln_g, ml_conv_w, ml_conv_b, ml_wq, ml_wk, ml_wv, ml_w_gate, ml_b_gate, ml_skip, ml_gn_g, w_pa, w_pb, w_out, ln1_g, ln1_b, router_group_w, router_group_b, router_expert_w, router_expert_b, w_e_gate, w_e_up, w_e_down, ln2_g, ln2_b):
    n_p, seq_len, d = x_prompt.shape
    n_s = x_sample.shape[0]
    assert x_sample.shape[1] == seq_len and d == D_MODEL and seq_len % CHUNK == 0
    n_seq = n_p + n_s
    x = jnp.concatenate([x_prompt, x_sample], axis=0).reshape(n_seq * seq_len, d)
    cos, sin = _rope_tables(seq_len)
    stacked = dict(w_in=w_in, lq1=da_lambda_q1, lk1=da_lambda_k1, lq2=da_lambda_q2,
                   lk2=da_lambda_k2, subln_g=da_subln_g, conv_w=ml_conv_w, conv_b=ml_conv_b,
                   wq=ml_wq, wk=ml_wk, wv=ml_wv, w_gate=ml_w_gate, b_gate=ml_b_gate,
                   skip=ml_skip, gn_g=ml_gn_g, w_pa=w_pa, w_pb=w_pb, w_out=w_out,
                   ln1_g=ln1_g, ln1_b=ln1_b, rg_w=router_group_w, rg_b=router_group_b,
                   re_w=router_expert_w, re_b=router_expert_b, we_g=w_e_gate, we_u=w_e_up,
                   we_d=w_e_down, ln2_g=ln2_g, ln2_b=ln2_b)
    for l in range(w_in.shape[0]):
        lambda_init = 0.8 - 0.6 * math.exp(-0.3 * l)
        x = _layer(x, n_seq, seq_len, lambda_init, cos, sin, {k: a[l] for k, a in stacked.items()})
    y = x.reshape(n_seq, seq_len, d)
    return (y[:n_p], y[n_p:])
```

```python
import functools
import math

import jax
import jax.numpy as jnp
from jax import lax
from jax.experimental import pallas as pl
from jax.experimental.pallas import tpu as pltpu

F32 = jnp.float32
BF16 = jnp.bfloat16

D_MODEL = 1024
DEPTH = 4
DA_HEADS = 8
DA_HEAD_DIM = 64
DA_V_DIM = 128
ROPE_THETA = 10000.0
ML_HEADS = 4
ML_HEAD_DIM = 256
ML_PROJ_BLOCK = 4
N_GROUPS = 4
EXPERTS_PER_GROUP = 8
N_EXPERTS = 32
TOP_K = 2
D_EXPERT = 512
DN_ALPHA = (2 * DEPTH) ** 0.25
LN_EPS = 1e-5

LANES = 128
MXU_DIM = 256
ROW_CHUNKS = D_MODEL // LANES
CHUNK = 128
MOE_ROWS = 512
VMEM_LIMIT = 56 * 1024 * 1024

NEG_BIG = -1e30


def _cparams(sem):
    return pltpu.CompilerParams(dimension_semantics=sem, vmem_limit_bytes=VMEM_LIMIT)


def _sigmoid(x):
    return 0.5 * jnp.tanh(0.5 * x) + 0.5


def _inproj_kernel(x_ref, w_ref, cos_ref, sin_ref, o_ref, *, n_rope, n_scaled, scale, cw):
    xb = x_ref[...].astype(BF16)
    n_cols = o_ref.shape[1]
    tm = x_ref.shape[0]
    if n_rope:
        cos = cos_ref[...]
        sin = sin_ref[...]
        lane = lax.broadcasted_iota(jnp.int32, (tm, LANES), 1)
        first_half = (lane % DA_HEAD_DIM) < (DA_HEAD_DIM // 2)
    for c in range(n_cols // cw):
        acc = jnp.dot(xb, w_ref[:, c * cw:(c + 1) * cw], preferred_element_type=F32)
        for s in range(cw // LANES):
            col = c * cw + s * LANES
            t = acc[:, s * LANES:(s + 1) * LANES]
            if col < n_rope:
                rot = jnp.where(first_half, pltpu.roll(t, LANES - DA_HEAD_DIM // 2, 1),
                                pltpu.roll(t, DA_HEAD_DIM // 2, 1))
                t = t * cos + rot * sin
                if col < n_scaled:
                    t = t * scale
            o_ref[:, col:col + LANES] = t.astype(o_ref.dtype)


def _inproj(x, w, cos, sin, seq_len, *, n_rope, n_scaled, scale, out_dtype):
    n_tok, d = x.shape
    n_cols = w.shape[1]
    tm = min(512, seq_len)
    per_seq = seq_len // tm
    kern = functools.partial(_inproj_kernel, n_rope=n_rope, n_scaled=n_scaled, scale=scale, cw=512)
    return pl.pallas_call(
        kern,
        grid=(n_tok // tm,),
        in_specs=[
            pl.BlockSpec((tm, d), lambda i: (i, 0)),
            pl.BlockSpec((d, n_cols), lambda i: (0, 0)),
            pl.BlockSpec((tm, LANES), lambda i: (i % per_seq, 0)),
            pl.BlockSpec((tm, LANES), lambda i: (i % per_seq, 0)),
        ],
        out_specs=pl.BlockSpec((tm, n_cols), lambda i: (i, 0)),
        out_shape=jax.ShapeDtypeStruct((n_tok, n_cols), out_dtype),
        compiler_params=_cparams(("parallel",)),
    )(x, w, cos, sin)


def _attn_kernel(sc_ref, q_ref, k_ref, v_ref, g_ref, o_ref, vx_ref, *, bq):
    seq_len = q_ref.shape[0]
    lam = sc_ref[0]
    out_scale = sc_ref[1]
    k = k_ref[...]
    vx_ref[:, :DA_V_DIM] = v_ref[...]
    vx_ref[:, DA_V_DIM:] = jnp.ones((seq_len, DA_V_DIM), vx_ref.dtype)
    vx = vx_ref[...]
    g = g_ref[...] * out_scale
    lane = lax.broadcasted_iota(jnp.int32, (bq, LANES), 1)
    is_first = lane < DA_HEAD_DIM
    dn = (((1,), (1,)), ((), ()))

    def softmax_av(qm):
        s = lax.dot_general(qm, k, dn, preferred_element_type=F32)
        p = jnp.exp2(s - jnp.max(s, axis=-1, keepdims=True)).astype(BF16)
        ox = jnp.dot(p, vx, preferred_element_type=F32)
        return ox[:, :DA_V_DIM] / ox[:, DA_V_DIM:]

    for i in range(seq_len // bq):
        rows = slice(i * bq, (i + 1) * bq)
        qb = q_ref[rows, :]
        zero = jnp.zeros_like(qb)
        o = (softmax_av(jnp.where(is_first, qb, zero))
             - lam * softmax_av(jnp.where(is_first, zero, qb)))
        o = o * lax.rsqrt(jnp.mean(o * o, axis=-1, keepdims=True) + LN_EPS)
        o_ref[rows, :] = (o * g).astype(o_ref.dtype)


def _attention(qkv, scalars, subln_g, n_seq, seq_len):
    n_tok = qkv.shape[0]
    bq = min(128, seq_len)
    return pl.pallas_call(
        functools.partial(_attn_kernel, bq=bq),
        grid=(n_seq, DA_HEADS),
        in_specs=[
            pl.BlockSpec(memory_space=pltpu.SMEM),
            pl.BlockSpec((seq_len, LANES), lambda b, h: (b, h)),
            pl.BlockSpec((seq_len, LANES), lambda b, h: (b, DA_HEADS + h)),
            pl.BlockSpec((seq_len, LANES), lambda b, h: (b, 2 * DA_HEADS + h)),
            pl.BlockSpec((1, DA_V_DIM), lambda b, h: (0, 0)),
        ],
        out_specs=pl.BlockSpec((seq_len, DA_V_DIM), lambda b, h: (b, h)),
        out_shape=jax.ShapeDtypeStruct((n_tok, DA_HEADS * DA_V_DIM), BF16),
        scratch_shapes=[pltpu.VMEM((seq_len, 2 * DA_V_DIM), BF16)],
        compiler_params=_cparams(("parallel", "parallel")),
    )(scalars, qkv, qkv, qkv, subln_g)


def _split3(x):
    x1 = x.astype(BF16)
    r1 = x - x1.astype(F32)
    x2 = r1.astype(BF16)
    x3 = (r1 - x2.astype(F32)).astype(BF16)
    return x1, x2, x3


def _gate_tables(gacc_ref, bg_ref, g1_ref, g2_ref, g3_ref, tmp_ref, last_ref):
    seq_len = gacc_ref.shape[0]
    n_chunks = seq_len // CHUNK
    r = lax.broadcasted_iota(jnp.int32, (CHUNK, CHUNK), 0)
    c = lax.broadcasted_iota(jnp.int32, (CHUNK, CHUNK), 1)
    tri = jnp.where(c <= r, 1.0, 0.0).astype(BF16)
    lane = lax.broadcasted_iota(jnp.int32, (CHUNK, LANES), 1)
    row = lax.broadcasted_iota(jnp.int32, (CHUNK, LANES), 0)
    is_kind0 = (lane % 2) == 0
    is_bwd = ((lane // 2) % 2) == 1
    is_bwd_row = is_bwd[0:1, :]
    bg = bg_ref[...]

    def first_pass(ci, carry):
        rows = pl.ds(pl.multiple_of(ci * CHUNK, CHUNK), CHUNK)
        pre = gacc_ref[rows, :] + bg
        lf = jnp.minimum(pre, 0.0) - jnp.log1p(jnp.exp(-jnp.abs(pre)))
        l1, l2, l3 = _split3(lf)
        pref = (jnp.dot(tri, l1, preferred_element_type=F32)
                + jnp.dot(tri, l2, preferred_element_type=F32)
                + jnp.dot(tri, l3, preferred_element_type=F32))
        suff = pref[CHUNK - 1:CHUNK, :] - pref + lf
        cum = jnp.where(is_bwd, suff, pref)
        b0 = pltpu.roll(cum, LANES - 1, 1)
        g = pre - b0
        mx_f = g
        mx_b = g
        s = 1
        while s < CHUNK:
            mx_f = jnp.maximum(mx_f, jnp.where(row >= s, pltpu.roll(mx_f, s, 0), -jnp.inf))
            mx_b = jnp.maximum(mx_b, jnp.where(row < CHUNK - s,
                                               pltpu.roll(mx_b, CHUNK - s, 0), -jnp.inf))
            s *= 2
        mx = jnp.where(is_bwd, mx_b, mx_f)
        tmp_ref[0, rows, :] = g
        tmp_ref[1, rows, :] = mx
        tmp_ref[2, rows, :] = b0
        last_ref[0, pl.ds(ci, 1), :] = jnp.where(is_bwd_row, b0[0:1, :], b0[CHUNK - 1:CHUNK, :])
        last_ref[1, pl.ds(ci, 1), :] = jnp.where(is_bwd_row, mx[0:1, :], mx[CHUNK - 1:CHUNK, :])
        return carry

    lax.fori_loop(0, n_chunks, first_pass, 0)

    m_f = jnp.zeros((1, LANES), F32)
    m_b = jnp.zeros((1, LANES), F32)
    for t in range(n_chunks):
        cf, cb = t, n_chunks - 1 - t
        last_ref[2, cf:cf + 1, :] = m_f
        last_ref[3, cb:cb + 1, :] = m_b
        m_f = last_ref[0, cf:cf + 1, :] + jnp.maximum(m_f, last_ref[1, cf:cf + 1, :])
        m_b = last_ref[0, cb:cb + 1, :] + jnp.maximum(m_b, last_ref[1, cb:cb + 1, :])

    def second_pass(ci, carry):
        rows = pl.ds(pl.multiple_of(ci * CHUNK, CHUNK), CHUNK)
        g = tmp_ref[0, rows, :]
        mx = tmp_ref[1, rows, :]
        b0 = tmp_ref[2, rows, :]
        m_st = jnp.where(is_bwd_row, last_ref[3, pl.ds(ci, 1), :], last_ref[2, pl.ds(ci, 1), :])
        mm = jnp.maximum(mx, m_st)
        m_up = jnp.maximum(m_st, last_ref[1, pl.ds(ci, 1), :])
        keep = jnp.broadcast_to(jnp.exp(m_st - m_up), (CHUNK, LANES))
        g1_ref[rows, :] = jnp.where(is_kind0, -mm, pltpu.roll(jnp.exp(-mm - b0), 1, 1))
        g2_ref[rows, :] = jnp.where(is_kind0, jnp.exp(m_st - mm), pltpu.roll(keep, 1, 1))
        g3_ref[rows, :] = jnp.where(is_kind0, g, pltpu.roll(jnp.exp(g - m_up), 1, 1))
        return carry

    lax.fori_loop(0, n_chunks, second_pass, 0)


def _mlpre_kernel(xm_ref, cw_ref, cb_ref, wq_ref, wkt_ref, wk_ref, wv_ref, wg_ref, bg_ref,
                  q_ref, kt_ref, v_ref, xc_ref, g1_ref, g2_ref, g3_ref,
                  gacc_ref, tmp_ref, last_ref):
    j = pl.program_id(1)
    seq_len = xm_ref.shape[0]
    xm = xm_ref[...].astype(F32)
    row = lax.broadcasted_iota(jnp.int32, xm.shape, 0)
    prev = jnp.where(row == 0, 0.0, pltpu.roll(xm, 1, 0))
    nxt = jnp.where(row == seq_len - 1, 0.0, pltpu.roll(xm, seq_len - 1, 0))
    xc = cb_ref[...] + prev * cw_ref[0:1, :] + xm * cw_ref[1:2, :] + nxt * cw_ref[2:3, :]
    xc = xc * _sigmoid(xc)
    xcb = xc.astype(BF16)
    xc_ref[...] = xcb
    q = jnp.dot(xcb, wq_ref[...], preferred_element_type=F32)
    k = jnp.dot(xcb, wk_ref[...], preferred_element_type=F32)
    v = jnp.dot(xm_ref[...], wv_ref[...], preferred_element_type=F32)
    qb = q.astype(BF16)
    kb = k.astype(BF16)
    vb = v.astype(BF16)
    q_ref[...] = (q * (ML_HEAD_DIM ** -0.5)).astype(BF16)
    kt_ref[...] = lax.dot_general(wkt_ref[...], xcb, (((1,), (1,)), ((), ())),
                                  preferred_element_type=F32).astype(BF16)
    v_ref[...] = vb
    part = (jnp.dot(qb, wg_ref[0], preferred_element_type=F32)
            + jnp.dot(kb, wg_ref[1], preferred_element_type=F32)
            + jnp.dot(vb, wg_ref[2], preferred_element_type=F32))

    @pl.when(j == 0)
    def _():
        gacc_ref[...] = part

    @pl.when(j > 0)
    def _():
        gacc_ref[...] += part

    @pl.when(j == pl.num_programs(1) - 1)
    def _():
        _gate_tables(gacc_ref, bg_ref, g1_ref, g2_ref, g3_ref, tmp_ref, last_ref)


def _mlstm_pre(rest, conv_w, conv_b, wq_bd, wk_bd, wv_bd, wg, bg, n_seq, seq_len):
    n_tok = rest.shape[0]
    n_ct = D_MODEL // MXU_DIM
    tile = pl.BlockSpec((seq_len, MXU_DIM), lambda b, j: (b, j))
    wspec = pl.BlockSpec((None, MXU_DIM, MXU_DIM), lambda b, j: (j, 0, 0))
    gspec = pl.BlockSpec((None, seq_len, LANES), lambda b, j: (b, 0, 0))
    act = jax.ShapeDtypeStruct((n_tok, D_MODEL), BF16)
    gate = jax.ShapeDtypeStruct((n_seq, seq_len, LANES), F32)
    return pl.pallas_call(
        _mlpre_kernel,
        grid=(n_seq, n_ct),
        in_specs=[
            tile,
            pl.BlockSpec((3, MXU_DIM), lambda b, j: (0, j)),
            pl.BlockSpec((1, MXU_DIM), lambda b, j: (0, j)),
            wspec, wspec, wspec, wspec,
            pl.BlockSpec((3, MXU_DIM, LANES), lambda b, j: (0, j, 0)),
            pl.BlockSpec((1, LANES), lambda b, j: (0, 0)),
        ],
        out_specs=[tile, pl.BlockSpec((MXU_DIM, seq_len), lambda b, j: (j, b)), tile, tile,
                   gspec, gspec, gspec],
        out_shape=[act, jax.ShapeDtypeStruct((D_MODEL, n_tok), BF16), act, act,
                   gate, gate, gate],
        scratch_shapes=[pltpu.VMEM((seq_len, LANES), F32),
                        pltpu.VMEM((3, seq_len, LANES), F32),
                        pltpu.VMEM((4, seq_len // CHUNK, LANES), F32)],
        compiler_params=_cparams(("parallel", "arbitrary")),
    )(rest, conv_w, conv_b, wq_bd, jnp.swapaxes(wk_bd, 1, 2), wk_bd, wv_bd, wg, bg)


def _mlstm_kernel(q_ref, kt_ref, v_ref, gc1_ref, gc2_ref, gr_ref, o_ref,
                  vx_ref, qk_ref, h_ref):
    seq_len = q_ref.shape[0]
    n_chunks = seq_len // CHUNK
    dh = ML_HEAD_DIM
    r = lax.broadcasted_iota(jnp.int32, (CHUNK, CHUNK), 0)
    c = lax.broadcasted_iota(jnp.int32, (CHUNK, CHUNK), 1)
    vx_ref[:, :dh] = v_ref[...]
    vx_ref[:, dh:] = jnp.ones((seq_len, LANES), vx_ref.dtype)

    def lane_replicated(ref, rows, col):
        return jnp.broadcast_to(ref[rows, col:col + 1], (CHUNK, LANES))

    def chunk_step(direction, ci, c_st):
        mask = (c <= r) if direction == 0 else (c >= r)
        rows = slice(ci * CHUNK, (ci + 1) * CHUNK)
        qc = q_ref[rows, :]
        ktc = kt_ref[:, rows]
        vxc = vx_ref[rows, :]
        neg_mm = lane_replicated(gc1_ref, rows, 2 * direction)
        e_mj = lane_replicated(gc1_ref, rows, 2 * direction + 1)
        inter = lane_replicated(gc2_ref, rows, 2 * direction)
        g_row = gr_ref[2 * direction:2 * direction + 1, rows]
        w_row = gr_ref[2 * direction + 1:2 * direction + 2, rows]
        keep = gc2_ref[ci * CHUNK:ci * CHUNK + 1, 2 * direction + 1:2 * direction + 2]
        if ci in first_visit:
            qk = qk_ref[ci]
        else:
            qk = jnp.dot(qc, ktc, preferred_element_type=F32)
            qk_ref[ci] = qk
        sw = qk * jnp.exp(jnp.where(mask, neg_mm + g_row, -jnp.inf))
        intra = jnp.dot(sw.astype(BF16), vxc, preferred_element_type=F32)
        carry_in = jnp.dot(qc, c_st.astype(BF16), preferred_element_type=F32)
        den = intra[:, dh:] + inter * carry_in[:, dh:]
        rdiv = 1.0 / jnp.maximum(jnp.abs(den), e_mj)
        h = jnp.concatenate(
            [(intra[:, s * LANES:(s + 1) * LANES] + inter * carry_in[:, s * LANES:(s + 1) * LANES])
             * rdiv for s in range(dh // LANES)], axis=1)
        if ci in first_visit:
            h = h_ref[rows, :] + h
            mu = jnp.mean(h, axis=-1, keepdims=True)
            hc = h - mu
            var = jnp.mean(hc * hc, axis=-1, keepdims=True)
            o_ref[rows, :] = (hc * lax.rsqrt(var + LN_EPS)).astype(o_ref.dtype)
        else:
            h_ref[rows, :] = h
            first_visit.add(ci)
        kw = (ktc.astype(F32) * w_row).astype(BF16)
        return keep * c_st + jnp.dot(kw, vxc, preferred_element_type=F32)

    first_visit = set()
    c_f = jnp.zeros((dh, dh + LANES), F32)
    c_b = jnp.zeros((dh, dh + LANES), F32)
    for step in range(n_chunks):
        c_f = chunk_step(0, step, c_f)
        c_b = chunk_step(1, n_chunks - 1 - step, c_b)


def _mlstm(q, kt, v, gc1, gc2, gr3, n_seq, seq_len):
    n_tok = q.shape[0]
    tile = pl.BlockSpec((seq_len, ML_HEAD_DIM), lambda b, h: (b, h))
    col = pl.BlockSpec((None, None, seq_len, 4), lambda b, h: (b, h, 0, 0))
    return pl.pallas_call(
        _mlstm_kernel,
        grid=(n_seq, ML_HEADS),
        in_specs=[
            tile,
            pl.BlockSpec((ML_HEAD_DIM, seq_len), lambda b, h: (h, b)),
            tile, col, col,
            pl.BlockSpec((None, None, 4, seq_len), lambda b, h: (b, h, 0, 0)),
        ],
        out_specs=tile,
        out_shape=jax.ShapeDtypeStruct((n_tok, D_MODEL), BF16),
        scratch_shapes=[
            pltpu.VMEM((seq_len, ML_HEAD_DIM + LANES), BF16),
            pltpu.VMEM((seq_len // CHUNK, CHUNK, CHUNK), F32),
            pltpu.VMEM((seq_len, ML_HEAD_DIM), F32),
        ],
        compiler_params=_cparams(("parallel", "parallel")),
    )(q, kt, v, gc1, gc2, gr3)


def _layer_norm(y, g, b):
    mu = jnp.mean(y, axis=-1, keepdims=True)
    yc = y - mu
    var = jnp.mean(yc * yc, axis=-1, keepdims=True)
    return yc * lax.rsqrt(var + LN_EPS) * g + b


def _route(logits):
    lane = lax.broadcasted_iota(jnp.int32, logits.shape, 1)
    big = jnp.int32(4 * LANES)
    gl = jnp.where(lane < N_GROUPS, logits, NEG_BIG)
    gmax = jnp.max(gl, axis=-1, keepdims=True)
    gsum = jnp.sum(jnp.where(lane < N_GROUPS, jnp.exp(gl - gmax), 0.0), axis=-1, keepdims=True)
    g_sel = jnp.min(jnp.where(gl == gmax, lane, big), axis=-1, keepdims=True)
    g_prob = 1.0 / gsum
    lo = N_GROUPS + EXPERTS_PER_GROUP * g_sel
    in_group = jnp.logical_and(lane >= lo, lane < lo + EXPERTS_PER_GROUP)
    el = jnp.where(in_group, logits, NEG_BIG)
    emax = jnp.max(el, axis=-1, keepdims=True)
    esum = jnp.sum(jnp.where(in_group, jnp.exp(el - emax), 0.0), axis=-1, keepdims=True)
    i1 = jnp.min(jnp.where(el == emax, lane, big), axis=-1, keepdims=True)
    el2 = jnp.where(lane == i1, NEG_BIG, el)
    emax2 = jnp.max(el2, axis=-1, keepdims=True)
    i2 = jnp.min(jnp.where(el2 == emax2, lane, big), axis=-1, keepdims=True)
    p1 = 1.0 / esum
    p2 = jnp.exp(emax2 - emax) / esum
    psum = p1 + p2
    gate1 = g_prob * p1 / psum
    gate2 = g_prob * p2 / psum
    e1 = (i1 - N_GROUPS).astype(F32)
    e2 = (i2 - N_GROUPS).astype(F32)
    return jnp.where(lane == 0, e1, jnp.where(lane == 1, e2, jnp.where(lane == 2, gate1, gate2)))


def _post_kernel(attn_ref, hn_ref, xc_ref, z_ref, ga_ref, gb_ref, x_ref,
                 wpa_ref, wpb_ref, wout_ref, gn_ref, skip_ref, lg_ref, lb_ref, wr_ref, br_ref,
                 x1t_ref, route_ref, *, sub):
    tm = x_ref.shape[0]
    for s in range(tm // sub):
        rows = slice(s * sub, (s + 1) * sub)
        z = z_ref[rows, :].astype(F32)
        ml = ((hn_ref[rows, :].astype(F32) * gn_ref[...]
               + skip_ref[...] * xc_ref[rows, :].astype(F32)) * (z * _sigmoid(z)))
        a_out = jnp.dot(attn_ref[rows, :], wpa_ref[...], preferred_element_type=F32)
        m_out = jnp.dot(ml.astype(BF16), wpb_ref[...], preferred_element_type=F32)
        mixed = (_sigmoid(ga_ref[rows, :].astype(F32)) * a_out
                 + _sigmoid(gb_ref[rows, :].astype(F32)) * m_out)
        y = DN_ALPHA * x_ref[rows, :] + jnp.dot(mixed.astype(BF16), wout_ref[...],
                                               preferred_element_type=F32)
        x1 = _layer_norm(y, lg_ref[...], lb_ref[...])
        for c in range(ROW_CHUNKS):
            x1t_ref[pl.ds(s * sub * ROW_CHUNKS + c, sub, stride=ROW_CHUNKS), :] = (
                x1[:, c * LANES:(c + 1) * LANES])
        xa, xb, _ = _split3(x1)
        logits = (br_ref[...] + jnp.dot(xa, wr_ref[0], preferred_element_type=F32)
                  + jnp.dot(xb, wr_ref[0], preferred_element_type=F32)
                  + jnp.dot(xa, wr_ref[1], preferred_element_type=F32))
        route_ref[rows, :] = _route(logits)


def _post(attn, hn, xc, rest, x, w_pa, w_pb, w_out, gn_g, skip, ln_g, ln_b, w_route, b_route):
    n_tok = x.shape[0]
    sub = 256
    tm = 2 * sub if n_tok % (2 * sub) == 0 else sub
    row = lambda col: pl.BlockSpec((tm, D_MODEL), lambda i, col=col: (i, col))
    full = lambda shape: pl.BlockSpec(shape, lambda i: tuple(0 for _ in shape))
    vec = full((1, D_MODEL))
    return pl.pallas_call(
        functools.partial(_post_kernel, sub=sub),
        grid=(n_tok // tm,),
        in_specs=[row(0), row(0), row(0), row(1), row(2), row(3), row(0),
                  full((D_MODEL, D_MODEL)), full((D_MODEL, D_MODEL)), full((D_MODEL, D_MODEL)),
                  vec, vec, vec, vec,
                  full((2, D_MODEL, LANES)), full((1, LANES))],
        out_specs=[pl.BlockSpec((tm * ROW_CHUNKS, LANES), lambda i: (i, 0)),
                   pl.BlockSpec((tm, LANES), lambda i: (i, 0))],
        out_shape=[jax.ShapeDtypeStruct((n_tok * ROW_CHUNKS, LANES), F32),
                   jax.ShapeDtypeStruct((n_tok, LANES), F32)],
        compiler_params=_cparams(("parallel",)),
    )(attn, hn, xc, rest, rest, rest, x, w_pa, w_pb, w_out, gn_g, skip, ln_g, ln_b,
      w_route, b_route)


def _moe_kernel(bexp_ref, nused_ref,
                tok_hbm, dst_hbm, gate_ref, x_hbm, wg_ref, wu_ref, wd_ref, y_hbm,
                tok_smem, dst_smem, xbuf, obuf, idx_sem, gat_sem, sca_sem):
    i = pl.program_id(0)
    n_used = nused_ref[0]
    p = i % 2

    def index_copies(blk):
        slot = blk % 3
        return (pltpu.make_async_copy(tok_hbm.at[blk], tok_smem.at[slot], idx_sem.at[0, slot]),
                pltpu.make_async_copy(dst_hbm.at[blk], dst_smem.at[slot], idx_sem.at[1, slot]))

    def start_indices(blk):
        for cp in index_copies(blk):
            cp.start()

    def wait_indices(blk):
        for cp in index_copies(blk):
            cp.wait()

    def tile_rows(r):
        return pl.ds(pl.multiple_of(r * ROW_CHUNKS, ROW_CHUNKS), ROW_CHUNKS)

    def start_gather(blk, slot):
        islot = blk % 3

        for g in range(MOE_ROWS // LANES):
            def body(r, carry, g=g):
                t = tok_smem[islot, g, r]
                pltpu.make_async_copy(x_hbm.at[t], xbuf.at[slot, tile_rows(g * LANES + r)],
                                      gat_sem.at[slot]).start()
                return carry
            lax.fori_loop(0, LANES, body, 0, unroll=8)

    def wait_gather(slot):
        pltpu.make_async_copy(xbuf.at[slot], xbuf.at[slot], gat_sem.at[slot]).wait()

    def start_scatter(blk, slot):
        islot = blk % 3

        for g in range(MOE_ROWS // LANES):
            def body(r, carry, g=g):
                t = dst_smem[islot, g, r]
                pltpu.make_async_copy(obuf.at[slot, tile_rows(g * LANES + r)], y_hbm.at[t],
                                      sca_sem.at[slot]).start()
                return carry
            lax.fori_loop(0, LANES, body, 0, unroll=8)

    def wait_scatter(slot):
        pltpu.make_async_copy(obuf.at[slot], obuf.at[slot], sca_sem.at[slot]).wait()

    @pl.when(i == 0)
    def _():
        start_indices(0)
        wait_indices(0)
        start_gather(0, 0)

        @pl.when(1 < n_used)
        def _():
            start_indices(1)

    @pl.when(i + 1 < n_used)
    def _():
        wait_indices(i + 1)
        start_gather(i + 1, 1 - p)

    @pl.when(i + 2 < n_used)
    def _():
        start_indices(i + 2)

    @pl.when(i < n_used)
    def _():
        wait_gather(p)

        @pl.when(i >= 2)
        def _():
            wait_scatter(p)

        xb = jnp.concatenate(
            [xbuf[p, pl.ds(c, MOE_ROWS, stride=ROW_CHUNKS), :] for c in range(ROW_CHUNKS)],
            axis=1).astype(BF16)
        hg = jnp.dot(xb, wg_ref[...], preferred_element_type=F32)
        hu = jnp.dot(xb, wu_ref[...], preferred_element_type=F32)
        hh = (hg * _sigmoid(hg) * hu).astype(BF16)
        out = jnp.dot(hh, wd_ref[...], preferred_element_type=F32) * gate_ref[...]
        for c in range(ROW_CHUNKS):
            obuf[p, pl.ds(c, MOE_ROWS, stride=ROW_CHUNKS), :] = out[:, c * LANES:(c + 1) * LANES]
        start_scatter(i, p)

        @pl.when(i == n_used - 1)
        def _():
            wait_scatter(p)

            @pl.when(i >= 1)
            def _():
                wait_scatter(1 - p)


def _moe(x1t, slot_tok, slot_dst, slot_gate, blk_exp, n_used, we_g, we_u, we_d):
    n_tok = x1t.shape[0] // ROW_CHUNKS
    n_blk = blk_exp.shape[0]
    n_out = TOP_K * n_tok + 2 * MOE_ROWS
    wspec_in = pl.BlockSpec((None, D_MODEL, D_EXPERT), lambda i, be, nu: (be[i], 0, 0))
    wspec_out = pl.BlockSpec((None, D_EXPERT, D_MODEL), lambda i, be, nu: (be[i], 0, 0))
    grid_spec = pltpu.PrefetchScalarGridSpec(
        num_scalar_prefetch=2,
        grid=(n_blk,),
        in_specs=[
            pl.BlockSpec(memory_space=pl.ANY),
            pl.BlockSpec(memory_space=pl.ANY),
            pl.BlockSpec((MOE_ROWS, 1), lambda i, be, nu: (i, 0)),
            pl.BlockSpec(memory_space=pl.ANY),
            wspec_in, wspec_in, wspec_out,
        ],
        out_specs=pl.BlockSpec(memory_space=pl.ANY),
        scratch_shapes=[
            pltpu.SMEM((3, MOE_ROWS // LANES, LANES), jnp.int32),
            pltpu.SMEM((3, MOE_ROWS // LANES, LANES), jnp.int32),
            pltpu.VMEM((2, MOE_ROWS * ROW_CHUNKS, LANES), F32),
            pltpu.VMEM((2, MOE_ROWS * ROW_CHUNKS, LANES), F32),
            pltpu.SemaphoreType.DMA((2, 3)),
            pltpu.SemaphoreType.DMA((2,)),
            pltpu.SemaphoreType.DMA((2,)),
        ],
    )
    y = pl.pallas_call(
        _moe_kernel,
        grid_spec=grid_spec,
        out_shape=jax.ShapeDtypeStruct((n_out, ROW_CHUNKS, LANES), F32),
        compiler_params=_cparams(("arbitrary",)),
    )(blk_exp, n_used, slot_tok.reshape(n_blk, MOE_ROWS // LANES, LANES),
      slot_dst.reshape(n_blk, MOE_ROWS // LANES, LANES), slot_gate.reshape(n_blk * MOE_ROWS, 1),
      x1t.reshape(n_tok, ROW_CHUNKS, LANES), we_g, we_u, we_d)
    return y.reshape(n_out * ROW_CHUNKS, LANES)


def _route_plan(route, n_tok):
    n_assign = n_tok * TOP_K
    eid = route[:, 0:TOP_K].astype(jnp.int32).reshape(n_assign)
    gate = route[:, TOP_K:2 * TOP_K].reshape(n_assign)
    order = jnp.argsort(eid).astype(jnp.int32)
    counts = jnp.zeros((N_EXPERTS,), jnp.int32).at[eid].add(1)
    offsets = jnp.cumsum(counts) - counts
    padded = (counts + MOE_ROWS - 1) // MOE_ROWS * MOE_ROWS
    pad_end = jnp.cumsum(padded)
    pad_start = pad_end - padded
    n_blk = (n_assign + MOE_ROWS - 1) // MOE_ROWS + N_EXPERTS
    blk_start = jnp.arange(n_blk, dtype=jnp.int32) * MOE_ROWS
    blk_exp = jnp.minimum(jnp.sum(pad_end[None, :] <= blk_start[:, None], axis=1),
                          N_EXPERTS - 1).astype(jnp.int32)
    row = jnp.arange(MOE_ROWS, dtype=jnp.int32)[None, :]
    j = (blk_start - pad_start[blk_exp])[:, None] + row
    valid = j < counts[blk_exp][:, None]
    src = jnp.clip(offsets[blk_exp][:, None] + j, 0, n_assign - 1)
    assign = order[src]
    tok = assign // TOP_K
    dummy = TOP_K * n_tok + (jnp.arange(n_blk, dtype=jnp.int32) % 2)[:, None] * MOE_ROWS + row
    slot_tok = jnp.where(valid, tok, 0)
    slot_dst = jnp.where(valid, (assign % TOP_K) * n_tok + tok, dummy)
    slot_gate = jnp.where(valid, gate[assign], 0.0)
    n_used = (pad_end[-1:] // MOE_ROWS).astype(jnp.int32)
    return slot_tok, slot_dst, slot_gate, blk_exp, n_used


def _combine_kernel(x_ref, y0_ref, y1_ref, g_ref, b_ref, o_ref):
    tm = o_ref.shape[0]

    def chunk(ref, c):
        return ref[pl.ds(c, tm, stride=ROW_CHUNKS), :]

    y = jnp.concatenate(
        [DN_ALPHA * chunk(x_ref, c) + (chunk(y0_ref, c) + chunk(y1_ref, c))
         for c in range(ROW_CHUNKS)], axis=1)
    o_ref[...] = _layer_norm(y, g_ref[...], b_ref[...])


def _combine(x1t, y, ln_g, ln_b):
    n_tok = x1t.shape[0] // ROW_CHUNKS
    tm = 256
    n_t = n_tok // tm
    vec = pl.BlockSpec((1, D_MODEL), lambda i: (0, 0))
    tiles = lambda off: pl.BlockSpec((tm * ROW_CHUNKS, LANES), lambda i, off=off: (i + off, 0))
    return pl.pallas_call(
        _combine_kernel,
        grid=(n_t,),
        in_specs=[tiles(0), tiles(0), tiles(n_t), vec, vec],
        out_specs=pl.BlockSpec((tm, D_MODEL), lambda i: (i, 0)),
        out_shape=jax.ShapeDtypeStruct((n_tok, D_MODEL), F32),
        compiler_params=_cparams(("parallel",)),
    )(x1t, y, y, ln_g, ln_b)


def _rope_tables(seq_len):
    half = DA_HEAD_DIM // 2
    inv = ROPE_THETA ** (-jnp.arange(half, dtype=F32) * (2.0 / DA_HEAD_DIM))
    ang = jnp.arange(seq_len, dtype=F32)[:, None] * inv[None, :]
    cos = jnp.tile(jnp.cos(ang), (1, LANES // half))
    sin = jnp.tile(jnp.sin(ang), (1, LANES // half))
    lane = jnp.arange(LANES)
    sign = jnp.where((lane % DA_HEAD_DIM) < half, -1.0, 1.0).astype(F32)
    return cos, sin * sign[None, :]


def _block_diag_tiles(w):
    per_tile = MXU_DIM // ML_PROJ_BLOCK
    n_tiles = w.shape[0] // per_tile
    w4 = w.reshape(n_tiles, per_tile, ML_PROJ_BLOCK, ML_PROJ_BLOCK)
    eye = jnp.eye(per_tile, dtype=w.dtype)
    bd = jnp.einsum('jgio,gh->jgiho', w4, eye)
    return bd.reshape(n_tiles, MXU_DIM, MXU_DIM).astype(BF16)


def _gate_perm():
    idx = []
    for h in range(ML_HEADS):
        for d in range(2):
            for kind in range(2):
                idx.append(d * 2 * ML_HEADS + kind * ML_HEADS + h)
    return jnp.array(idx, dtype=jnp.int32)


def _layer(x, n_seq, seq_len, lambda_init, cos, sin, p):
    n_tok = x.shape[0]
    qk_w = DA_HEADS * 2 * DA_HEAD_DIM
    v_w = DA_HEADS * DA_V_DIM
    w_in = p['w_in'].astype(BF16)
    qkv = _inproj(x, w_in[:, :2 * qk_w + v_w], cos, sin, seq_len, n_rope=2 * qk_w,
                  n_scaled=qk_w, scale=DA_HEAD_DIM ** -0.5 * math.log2(math.e), out_dtype=BF16)
    rest = _inproj(x, w_in[:, 2 * qk_w + v_w:], cos, sin, seq_len, n_rope=0, n_scaled=0,
                   scale=1.0, out_dtype=BF16)

    lam = (jnp.exp(jnp.sum(p['lq1'] * p['lk1'])) - jnp.exp(jnp.sum(p['lq2'] * p['lk2']))
           + lambda_init)
    scalars = jnp.stack([lam, jnp.asarray(1.0 - lambda_init, F32)]).astype(F32)
    attn = _attention(qkv, scalars, p['subln_g'].reshape(1, DA_V_DIM), n_seq, seq_len)

    perm = _gate_perm()
    n_gate = 4 * ML_HEADS
    wg = p['w_gate'][:, perm].reshape(3, D_MODEL, n_gate)
    wg = jnp.pad(wg, ((0, 0), (0, 0), (0, LANES - n_gate))).astype(BF16)
    bg = jnp.pad(p['b_gate'][perm], (0, LANES - n_gate)).reshape(1, LANES)
    q, kt, v, xc, g1, g2, g3 = _mlstm_pre(
        rest, p['conv_w'], p['conv_b'].reshape(1, D_MODEL), _block_diag_tiles(p['wq']),
        _block_diag_tiles(p['wk']), _block_diag_tiles(p['wv']), wg, bg, n_seq, seq_len)

    def per_head(t, perm):
        t = t[:, :, :n_gate].reshape(n_seq, seq_len, ML_HEADS, 4)
        return jnp.transpose(t, perm)

    hn = _mlstm(q, kt, v, per_head(g1, (0, 2, 1, 3)), per_head(g2, (0, 2, 1, 3)),
                per_head(g3, (0, 2, 3, 1)), n_seq, seq_len)

    n_route = N_GROUPS + N_EXPERTS
    w_route = jnp.concatenate([p['rg_w'], p['re_w']], axis=1)
    w_route = jnp.pad(w_route, ((0, 0), (0, LANES - n_route)))
    w_route = jnp.stack(_split3(w_route)[:2])
    b_route = jnp.pad(jnp.concatenate([p['rg_b'], p['re_b']]), (0, LANES - n_route))
    vec = lambda a: a.reshape(1, D_MODEL)
    x1, route = _post(attn, hn, xc, rest, x, p['w_pa'].astype(BF16), p['w_pb'].astype(BF16),
                      p['w_out'].astype(BF16), vec(p['gn_g']), vec(p['skip']), vec(p['ln1_g']),
                      vec(p['ln1_b']), w_route, b_route.reshape(1, LANES))

    plan = _route_plan(route, n_tok)
    y = _moe(x1, *plan, p['we_g'].astype(BF16), p['we_u'].astype(BF16), p['we_d'].astype(BF16))
    return _combine(x1, y, vec(p['ln2_g']), vec(p['ln2_b']))


def kernel(x_prompt, x_sample, w_in, da_lambda_q1, da_lambda_k1, da_lambda_q2, da_lambda_k2, da_subln_g, ml_conv_w, ml_conv_b, ml_wq, ml_wk, ml_wv, ml_w_gate, ml_b_gate, ml_skip, ml_gn_g, w_pa, w_pb, w_out, ln1_g, ln1_b, router_group_w, router_group_b, router_expert_w, router_expert_b, w_e_gate, w_e_up, w_e_down, ln2_g, ln2_b):
    n_p, seq_len, d = x_prompt.shape
    n_s = x_sample.shape[0]
    assert x_sample.shape[1] == seq_len and d == D_MODEL and seq_len % CHUNK == 0
    n_seq = n_p + n_s
    x = jnp.concatenate([x_prompt, x_sample], axis=0).reshape(n_seq * seq_len, d)
    cos, sin = _rope_tables(seq_len)
    stacked = dict(w_in=w_in, lq1=da_lambda_q1, lk1=da_lambda_k1, lq2=da_lambda_q2,
                   lk2=da_lambda_k2, subln_g=da_subln_g, conv_w=ml_conv_w, conv_b=ml_conv_b,
                   wq=ml_wq, wk=ml_wk, wv=ml_wv, w_gate=ml_w_gate, b_gate=ml_b_gate,
                   skip=ml_skip, gn_g=ml_gn_g, w_pa=w_pa, w_pb=w_pb, w_out=w_out,
                   ln1_g=ln1_g, ln1_b=ln1_b, rg_w=router_group_w, rg_b=router_group_b,
                   re_w=router_expert_w, re_b=router_expert_b, we_g=w_e_gate, we_u=w_e_up,
                   we_d=w_e_down, ln2_g=ln2_g, ln2_b=ln2_b)
    for l in range(w_in.shape[0]):
        lambda_init = 0.8 - 0.6 * math.exp(-0.3 * l)
        x = _layer(x, n_seq, seq_len, lambda_init, cos, sin, {k: a[l] for k, a in stacked.items()})
    y = x.reshape(n_seq, seq_len, d)
    return (y[:n_p], y[n_p:])
```

```python
import functools
import math

import jax
import jax.numpy as jnp
from jax import lax
from jax.experimental import pallas as pl
from jax.experimental.pallas import tpu as pltpu

F32 = jnp.float32
BF16 = jnp.bfloat16

D_MODEL = 1024
DEPTH = 4
DA_HEADS = 8
DA_HEAD_DIM = 64
DA_V_DIM = 128
ROPE_THETA = 10000.0
ML_HEADS = 4
ML_HEAD_DIM = 256
ML_PROJ_BLOCK = 4
N_GROUPS = 4
EXPERTS_PER_GROUP = 8
N_EXPERTS = 32
TOP_K = 2
D_EXPERT = 512
DN_ALPHA = (2 * DEPTH) ** 0.25
LN_EPS = 1e-5

LANES = 128
MXU_DIM = 256
ROW_CHUNKS = D_MODEL // LANES
CHUNK = 128
MOE_ROWS = 512
DMA_PRIORITIES = 2
VMEM_LIMIT = 56 * 1024 * 1024

NEG_BIG = -1e30


def _cparams(sem):
    return pltpu.CompilerParams(dimension_semantics=sem, vmem_limit_bytes=VMEM_LIMIT)


def _sigmoid(x):
    return 0.5 * jnp.tanh(0.5 * x) + 0.5


def _inproj_kernel(*refs, fused, n_rope, n_scaled, n_qkv, scale, cw):
    if fused:
        (x1t_ref, y0_ref, y1_ref, g_ref, b_ref, w_ref, cos_ref, sin_ref,
         x_out_ref, qkv_ref, rest_ref) = refs
        x = _combine_rows(x1t_ref, y0_ref, y1_ref, g_ref, b_ref, qkv_ref.shape[0])
        x_out_ref[...] = x
    else:
        x_ref, w_ref, cos_ref, sin_ref, qkv_ref, rest_ref = refs
        x = x_ref[...]
    xb = x.astype(BF16)
    tm = qkv_ref.shape[0]
    n_cols = w_ref.shape[1]
    cos = cos_ref[...]
    sin = sin_ref[...]
    lane = lax.broadcasted_iota(jnp.int32, (tm, LANES), 1)
    first_half = (lane % DA_HEAD_DIM) < (DA_HEAD_DIM // 2)
    for c in range(n_cols // cw):
        acc = jnp.dot(xb, w_ref[:, c * cw:(c + 1) * cw], preferred_element_type=F32)
        for s in range(cw // LANES):
            col = c * cw + s * LANES
            t = acc[:, s * LANES:(s + 1) * LANES]
            if col < n_rope:
                rot = jnp.where(first_half, pltpu.roll(t, LANES - DA_HEAD_DIM // 2, 1),
                                pltpu.roll(t, DA_HEAD_DIM // 2, 1))
                t = t * cos + rot * sin
                if col < n_scaled:
                    t = t * scale
            if col < n_qkv:
                qkv_ref[:, col:col + LANES] = t.astype(qkv_ref.dtype)
            else:
                rest_ref[:, col - n_qkv:col - n_qkv + LANES] = t.astype(rest_ref.dtype)


def _inproj(src, w, cos, sin, seq_len, *, n_rope, n_scaled, n_qkv, scale):
    fused = isinstance(src, tuple)
    d, n_cols = w.shape
    n_tok = src[0].shape[0] // ROW_CHUNKS if fused else src.shape[0]
    tm = min(512, seq_len)
    n_t = n_tok // tm
    per_seq = seq_len // tm
    kern = functools.partial(_inproj_kernel, fused=fused, n_rope=n_rope, n_scaled=n_scaled,
                             n_qkv=n_qkv, scale=scale, cw=512)
    table = pl.BlockSpec((tm, LANES), lambda i: (i % per_seq, 0))
    wspec = pl.BlockSpec((d, n_cols), lambda i: (0, 0), pipeline_mode=pl.Buffered(1))
    rows = lambda n: pl.BlockSpec((tm, n), lambda i: (i, 0))
    outs = [rows(n_qkv), rows(n_cols - n_qkv)]
    out_shapes = [jax.ShapeDtypeStruct((n_tok, n_qkv), BF16),
                  jax.ShapeDtypeStruct((n_tok, n_cols - n_qkv), BF16)]
    if fused:
        x1t, y, ln_g, ln_b = src
        tiles = lambda off: pl.BlockSpec((tm * ROW_CHUNKS, LANES), lambda i, off=off: (i + off, 0))
        vec = pl.BlockSpec((1, d), lambda i: (0, 0))
        x, qkv, rest = pl.pallas_call(
            kern, grid=(n_t,),
            in_specs=[tiles(0), tiles(0), tiles(n_t), vec, vec, wspec, table, table],
            out_specs=[rows(d)] + outs,
            out_shape=[jax.ShapeDtypeStruct((n_tok, d), F32)] + out_shapes,
            compiler_params=_cparams(("parallel",)),
        )(x1t, y, y, ln_g, ln_b, w, cos, sin)
        return x, qkv, rest
    qkv, rest = pl.pallas_call(
        kern, grid=(n_t,),
        in_specs=[rows(d), wspec, table, table],
        out_specs=outs, out_shape=out_shapes,
        compiler_params=_cparams(("parallel",)),
    )(src, w, cos, sin)
    return src, qkv, rest


def _attn_kernel(sc_ref, q_ref, k_ref, v_ref, g_ref, o_ref, vx_ref, *, bq):
    seq_len = q_ref.shape[0]
    lam = sc_ref[0]
    out_scale = sc_ref[1]
    k = k_ref[...]
    vx_ref[:, :DA_V_DIM] = v_ref[...]
    vx_ref[:, DA_V_DIM:] = jnp.ones((seq_len, DA_V_DIM), vx_ref.dtype)
    vx = vx_ref[...]
    g = g_ref[...] * out_scale
    lane = lax.broadcasted_iota(jnp.int32, (bq, LANES), 1)
    is_first = lane < DA_HEAD_DIM
    dn = (((1,), (1,)), ((), ()))

    def softmax_av(qm):
        s = lax.dot_general(qm, k, dn, preferred_element_type=F32)
        p = jnp.exp2(s - jnp.max(s, axis=-1, keepdims=True)).astype(BF16)
        ox = jnp.dot(p, vx, preferred_element_type=F32)
        return ox[:, :DA_V_DIM] / ox[:, DA_V_DIM:]

    for i in range(seq_len // bq):
        rows = slice(i * bq, (i + 1) * bq)
        qb = q_ref[rows, :]
        zero = jnp.zeros_like(qb)
        o = (softmax_av(jnp.where(is_first, qb, zero))
             - lam * softmax_av(jnp.where(is_first, zero, qb)))
        o = o * lax.rsqrt(jnp.mean(o * o, axis=-1, keepdims=True) + LN_EPS)
        o_ref[rows, :] = (o * g).astype(o_ref.dtype)


def _attention(qkv, scalars, subln_g, n_seq, seq_len):
    n_tok = qkv.shape[0]
    bq = min(128, seq_len)
    return pl.pallas_call(
        functools.partial(_attn_kernel, bq=bq),
        grid=(n_seq, DA_HEADS),
        in_specs=[
            pl.BlockSpec(memory_space=pltpu.SMEM),
            pl.BlockSpec((seq_len, LANES), lambda b, h: (b, h)),
            pl.BlockSpec((seq_len, LANES), lambda b, h: (b, DA_HEADS + h)),
            pl.BlockSpec((seq_len, LANES), lambda b, h: (b, 2 * DA_HEADS + h)),
            pl.BlockSpec((1, DA_V_DIM), lambda b, h: (0, 0)),
        ],
        out_specs=pl.BlockSpec((seq_len, DA_V_DIM), lambda b, h: (b, h)),
        out_shape=jax.ShapeDtypeStruct((n_tok, DA_HEADS * DA_V_DIM), BF16),
        scratch_shapes=[pltpu.VMEM((seq_len, 2 * DA_V_DIM), BF16)],
        compiler_params=_cparams(("parallel", "parallel")),
    )(scalars, qkv, qkv, qkv, subln_g)


def _split3(x):
    x1 = x.astype(BF16)
    r1 = x - x1.astype(F32)
    x2 = r1.astype(BF16)
    x3 = (r1 - x2.astype(F32)).astype(BF16)
    return x1, x2, x3


def _gate_tables(gacc_ref, bg_ref, g1_ref, g2_ref, g3_ref, tmp_ref, last_ref):
    seq_len = gacc_ref.shape[0]
    n_chunks = seq_len // CHUNK
    r = lax.broadcasted_iota(jnp.int32, (CHUNK, CHUNK), 0)
    c = lax.broadcasted_iota(jnp.int32, (CHUNK, CHUNK), 1)
    tri = jnp.where(c <= r, 1.0, 0.0).astype(BF16)
    lane = lax.broadcasted_iota(jnp.int32, (CHUNK, LANES), 1)
    row = lax.broadcasted_iota(jnp.int32, (CHUNK, LANES), 0)
    is_kind0 = (lane % 2) == 0
    is_bwd = ((lane // 2) % 2) == 1
    is_bwd_row = is_bwd[0:1, :]
    bg = bg_ref[...]

    def first_pass(ci, carry):
        rows = pl.ds(pl.multiple_of(ci * CHUNK, CHUNK), CHUNK)
        pre = gacc_ref[rows, :] + bg
        lf = jnp.minimum(pre, 0.0) - jnp.log1p(jnp.exp(-jnp.abs(pre)))
        l1, l2, l3 = _split3(lf)
        pref = (jnp.dot(tri, l1, preferred_element_type=F32)
                + jnp.dot(tri, l2, preferred_element_type=F32)
                + jnp.dot(tri, l3, preferred_element_type=F32))
        suff = pref[CHUNK - 1:CHUNK, :] - pref + lf
        cum = jnp.where(is_bwd, suff, pref)
        b0 = pltpu.roll(cum, LANES - 1, 1)
        g = pre - b0
        mx_f = g
        mx_b = g
        s = 1
        while s < CHUNK:
            mx_f = jnp.maximum(mx_f, jnp.where(row >= s, pltpu.roll(mx_f, s, 0), -jnp.inf))
            mx_b = jnp.maximum(mx_b, jnp.where(row < CHUNK - s,
                                               pltpu.roll(mx_b, CHUNK - s, 0), -jnp.inf))
            s *= 2
        mx = jnp.where(is_bwd, mx_b, mx_f)
        tmp_ref[0, rows, :] = g
        tmp_ref[1, rows, :] = mx
        tmp_ref[2, rows, :] = b0
        last_ref[0, pl.ds(ci, 1), :] = jnp.where(is_bwd_row, b0[0:1, :], b0[CHUNK - 1:CHUNK, :])
        last_ref[1, pl.ds(ci, 1), :] = jnp.where(is_bwd_row, mx[0:1, :], mx[CHUNK - 1:CHUNK, :])
        return carry

    lax.fori_loop(0, n_chunks, first_pass, 0)

    m_f = jnp.zeros((1, LANES), F32)
    m_b = jnp.zeros((1, LANES), F32)
    for t in range(n_chunks):
        cf, cb = t, n_chunks - 1 - t
        last_ref[2, cf:cf + 1, :] = m_f
        last_ref[3, cb:cb + 1, :] = m_b
        m_f = last_ref[0, cf:cf + 1, :] + jnp.maximum(m_f, last_ref[1, cf:cf + 1, :])
        m_b = last_ref[0, cb:cb + 1, :] + jnp.maximum(m_b, last_ref[1, cb:cb + 1, :])

    def second_pass(ci, carry):
        rows = pl.ds(pl.multiple_of(ci * CHUNK, CHUNK), CHUNK)
        g = tmp_ref[0, rows, :]
        mx = tmp_ref[1, rows, :]
        b0 = tmp_ref[2, rows, :]
        m_st = jnp.where(is_bwd_row, last_ref[3, pl.ds(ci, 1), :], last_ref[2, pl.ds(ci, 1), :])
        mm = jnp.maximum(mx, m_st)
        m_up = jnp.maximum(m_st, last_ref[1, pl.ds(ci, 1), :])
        keep = jnp.broadcast_to(jnp.exp(m_st - m_up), (CHUNK, LANES))
        g1_ref[rows, :] = jnp.where(is_kind0, -mm, pltpu.roll(jnp.exp(-mm - b0), 1, 1))
        g2_ref[rows, :] = jnp.where(is_kind0, jnp.exp(m_st - mm), pltpu.roll(keep, 1, 1))
        g3_ref[rows, :] = jnp.where(is_kind0, g, pltpu.roll(jnp.exp(g - m_up), 1, 1))
        return carry

    lax.fori_loop(0, n_chunks, second_pass, 0)


def _mlpre_kernel(xm_ref, cw_ref, cb_ref, wq_ref, wkt_ref, wk_ref, wv_ref, wg_ref, bg_ref,
                  q_ref, kt_ref, v_ref, xc_ref, g1_ref, g2_ref, g3_ref,
                  gacc_ref, tmp_ref, last_ref):
    j = pl.program_id(1)
    seq_len = xm_ref.shape[0]
    xm = xm_ref[...].astype(F32)
    row = lax.broadcasted_iota(jnp.int32, xm.shape, 0)
    prev = jnp.where(row == 0, 0.0, pltpu.roll(xm, 1, 0))
    nxt = jnp.where(row == seq_len - 1, 0.0, pltpu.roll(xm, seq_len - 1, 0))
    xc = cb_ref[...] + prev * cw_ref[0:1, :] + xm * cw_ref[1:2, :] + nxt * cw_ref[2:3, :]
    xc = xc * _sigmoid(xc)
    xcb = xc.astype(BF16)
    xc_ref[...] = xcb
    q = jnp.dot(xcb, wq_ref[...], preferred_element_type=F32)
    k = jnp.dot(xcb, wk_ref[...], preferred_element_type=F32)
    v = jnp.dot(xm_ref[...], wv_ref[...], preferred_element_type=F32)
    qb = q.astype(BF16)
    kb = k.astype(BF16)
    vb = v.astype(BF16)
    q_ref[...] = (q * (ML_HEAD_DIM ** -0.5)).astype(BF16)
    kt_ref[...] = lax.dot_general(wkt_ref[...], xcb, (((1,), (1,)), ((), ())),
                                  preferred_element_type=F32).astype(BF16)
    v_ref[...] = vb
    part = (jnp.dot(qb, wg_ref[0], preferred_element_type=F32)
            + jnp.dot(kb, wg_ref[1], preferred_element_type=F32)
            + jnp.dot(vb, wg_ref[2], preferred_element_type=F32))

    @pl.when(j == 0)
    def _():
        gacc_ref[...] = part

    @pl.when(j > 0)
    def _():
        gacc_ref[...] += part

    @pl.when(j == pl.num_programs(1) - 1)
    def _():
        _gate_tables(gacc_ref, bg_ref, g1_ref, g2_ref, g3_ref, tmp_ref, last_ref)


def _mlstm_pre(rest, conv_w, conv_b, wq_bd, wk_bd, wv_bd, wg, bg, n_seq, seq_len):
    n_tok = rest.shape[0]
    n_ct = D_MODEL // MXU_DIM
    tile = pl.BlockSpec((seq_len, MXU_DIM), lambda b, j: (b, j))
    wspec = pl.BlockSpec((None, MXU_DIM, MXU_DIM), lambda b, j: (j, 0, 0))
    gspec = pl.BlockSpec((None, seq_len, LANES), lambda b, j: (b, 0, 0))
    act = jax.ShapeDtypeStruct((n_tok, D_MODEL), BF16)
    gate = jax.ShapeDtypeStruct((n_seq, seq_len, LANES), F32)
    return pl.pallas_call(
        _mlpre_kernel,
        grid=(n_seq, n_ct),
        in_specs=[
            tile,
            pl.BlockSpec((3, MXU_DIM), lambda b, j: (0, j)),
            pl.BlockSpec((1, MXU_DIM), lambda b, j: (0, j)),
            wspec, wspec, wspec, wspec,
            pl.BlockSpec((3, MXU_DIM, LANES), lambda b, j: (0, j, 0)),
            pl.BlockSpec((1, LANES), lambda b, j: (0, 0)),
        ],
        out_specs=[tile, pl.BlockSpec((MXU_DIM, seq_len), lambda b, j: (j, b)), tile, tile,
                   gspec, gspec, gspec],
        out_shape=[act, jax.ShapeDtypeStruct((D_MODEL, n_tok), BF16), act, act,
                   gate, gate, gate],
        scratch_shapes=[pltpu.VMEM((seq_len, LANES), F32),
                        pltpu.VMEM((3, seq_len, LANES), F32),
                        pltpu.VMEM((4, seq_len // CHUNK, LANES), F32)],
        compiler_params=_cparams(("parallel", "arbitrary")),
    )(rest, conv_w, conv_b, wq_bd, jnp.swapaxes(wk_bd, 1, 2), wk_bd, wv_bd, wg, bg)


def _mlstm_kernel(q_ref, kt_ref, v_ref, gc1_ref, gc2_ref, gr_ref, o_ref,
                  vx_ref, qk_ref, h_ref):
    seq_len = q_ref.shape[0]
    n_chunks = seq_len // CHUNK
    dh = ML_HEAD_DIM
    r = lax.broadcasted_iota(jnp.int32, (CHUNK, CHUNK), 0)
    c = lax.broadcasted_iota(jnp.int32, (CHUNK, CHUNK), 1)
    vx_ref[:, :dh] = v_ref[...]
    vx_ref[:, dh:] = jnp.ones((seq_len, LANES), vx_ref.dtype)

    def lane_replicated(ref, rows, col):
        return jnp.broadcast_to(ref[rows, col:col + 1], (CHUNK, LANES))

    def chunk_step(direction, ci, c_st):
        mask = (c <= r) if direction == 0 else (c >= r)
        rows = slice(ci * CHUNK, (ci + 1) * CHUNK)
        qc = q_ref[rows, :]
        ktc = kt_ref[:, rows]
        vxc = vx_ref[rows, :]
        neg_mm = lane_replicated(gc1_ref, rows, 2 * direction)
        e_mj = lane_replicated(gc1_ref, rows, 2 * direction + 1)
        inter = lane_replicated(gc2_ref, rows, 2 * direction)
        g_row = gr_ref[2 * direction:2 * direction + 1, rows]
        w_row = gr_ref[2 * direction + 1:2 * direction + 2, rows]
        keep = gc2_ref[ci * CHUNK:ci * CHUNK + 1, 2 * direction + 1:2 * direction + 2]
        if ci in first_visit:
            qk = qk_ref[ci]
        else:
            qk = jnp.dot(qc, ktc, preferred_element_type=F32)
            qk_ref[ci] = qk
        sw = qk * jnp.exp(jnp.where(mask, neg_mm + g_row, -jnp.inf))
        intra = jnp.dot(sw.astype(BF16), vxc, preferred_element_type=F32)
        carry_in = jnp.dot(qc, c_st.astype(BF16), preferred_element_type=F32)
        den = intra[:, dh:] + inter * carry_in[:, dh:]
        rdiv = 1.0 / jnp.maximum(jnp.abs(den), e_mj)
        h = jnp.concatenate(
            [(intra[:, s * LANES:(s + 1) * LANES] + inter * carry_in[:, s * LANES:(s + 1) * LANES])
             * rdiv for s in range(dh // LANES)], axis=1)
        if ci in first_visit:
            h = h_ref[rows, :] + h
            mu = jnp.mean(h, axis=-1, keepdims=True)
            hc = h - mu
            var = jnp.mean(hc * hc, axis=-1, keepdims=True)
            o_ref[rows, :] = (hc * lax.rsqrt(var + LN_EPS)).astype(o_ref.dtype)
        else:
            h_ref[rows, :] = h
            first_visit.add(ci)
        kw = (ktc.astype(F32) * w_row).astype(BF16)
        return keep * c_st + jnp.dot(kw, vxc, preferred_element_type=F32)

    first_visit = set()
    c_f = jnp.zeros((dh, dh + LANES), F32)
    c_b = jnp.zeros((dh, dh + LANES), F32)
    for step in range(n_chunks):
        c_f = chunk_step(0, step, c_f)
        c_b = chunk_step(1, n_chunks - 1 - step, c_b)


def _mlstm(q, kt, v, gc1, gc2, gr3, n_seq, seq_len):
    n_tok = q.shape[0]
    tile = pl.BlockSpec((seq_len, ML_HEAD_DIM), lambda b, h: (b, h))
    col = pl.BlockSpec((None, None, seq_len, 4), lambda b, h: (b, h, 0, 0))
    return pl.pallas_call(
        _mlstm_kernel,
        grid=(n_seq, ML_HEADS),
        in_specs=[
            tile,
            pl.BlockSpec((ML_HEAD_DIM, seq_len), lambda b, h: (h, b)),
            tile, col, col,
            pl.BlockSpec((None, None, 4, seq_len), lambda b, h: (b, h, 0, 0)),
        ],
        out_specs=tile,
        out_shape=jax.ShapeDtypeStruct((n_tok, D_MODEL), BF16),
        scratch_shapes=[
            pltpu.VMEM((seq_len, ML_HEAD_DIM + LANES), BF16),
            pltpu.VMEM((seq_len // CHUNK, CHUNK, CHUNK), F32),
            pltpu.VMEM((seq_len, ML_HEAD_DIM), F32),
        ],
        compiler_params=_cparams(("parallel", "parallel")),
    )(q, kt, v, gc1, gc2, gr3)


def _layer_norm(y, g, b):
    mu = jnp.mean(y, axis=-1, keepdims=True)
    yc = y - mu
    var = jnp.mean(yc * yc, axis=-1, keepdims=True)
    return yc * lax.rsqrt(var + LN_EPS) * g + b


def _route(logits):
    lane = lax.broadcasted_iota(jnp.int32, logits.shape, 1)
    big = jnp.int32(4 * LANES)
    gl = jnp.where(lane < N_GROUPS, logits, NEG_BIG)
    gmax = jnp.max(gl, axis=-1, keepdims=True)
    gsum = jnp.sum(jnp.where(lane < N_GROUPS, jnp.exp(gl - gmax), 0.0), axis=-1, keepdims=True)
    g_sel = jnp.min(jnp.where(gl == gmax, lane, big), axis=-1, keepdims=True)
    g_prob = 1.0 / gsum
    lo = N_GROUPS + EXPERTS_PER_GROUP * g_sel
    in_group = jnp.logical_and(lane >= lo, lane < lo + EXPERTS_PER_GROUP)
    el = jnp.where(in_group, logits, NEG_BIG)
    emax = jnp.max(el, axis=-1, keepdims=True)
    esum = jnp.sum(jnp.where(in_group, jnp.exp(el - emax), 0.0), axis=-1, keepdims=True)
    i1 = jnp.min(jnp.where(el == emax, lane, big), axis=-1, keepdims=True)
    el2 = jnp.where(lane == i1, NEG_BIG, el)
    emax2 = jnp.max(el2, axis=-1, keepdims=True)
    i2 = jnp.min(jnp.where(el2 == emax2, lane, big), axis=-1, keepdims=True)
    p1 = 1.0 / esum
    p2 = jnp.exp(emax2 - emax) / esum
    psum = p1 + p2
    gate1 = g_prob * p1 / psum
    gate2 = g_prob * p2 / psum
    e1 = (i1 - N_GROUPS).astype(F32)
    e2 = (i2 - N_GROUPS).astype(F32)
    return jnp.where(lane == 0, e1, jnp.where(lane == 1, e2, jnp.where(lane == 2, gate1, gate2)))


def _post_kernel(attn_ref, hn_ref, xc_ref, z_ref, ga_ref, gb_ref, x_ref,
                 wpa_ref, wpb_ref, wout_ref, gn_ref, skip_ref, lg_ref, lb_ref, wr_ref, br_ref,
                 x1t_ref, route_ref, *, sub):
    tm = x_ref.shape[0]
    for s in range(tm // sub):
        rows = slice(s * sub, (s + 1) * sub)
        z = z_ref[rows, :].astype(F32)
        ml = ((hn_ref[rows, :].astype(F32) * gn_ref[...]
               + skip_ref[...] * xc_ref[rows, :].astype(F32)) * (z * _sigmoid(z)))
        a_out = jnp.dot(attn_ref[rows, :], wpa_ref[...], preferred_element_type=F32)
        m_out = jnp.dot(ml.astype(BF16), wpb_ref[...], preferred_element_type=F32)
        mixed = (_sigmoid(ga_ref[rows, :].astype(F32)) * a_out
                 + _sigmoid(gb_ref[rows, :].astype(F32)) * m_out)
        y = DN_ALPHA * x_ref[rows, :] + jnp.dot(mixed.astype(BF16), wout_ref[...],
                                               preferred_element_type=F32)
        x1 = _layer_norm(y, lg_ref[...], lb_ref[...])
        for c in range(ROW_CHUNKS):
            x1t_ref[pl.ds(s * sub * ROW_CHUNKS + c, sub, stride=ROW_CHUNKS), :] = (
                x1[:, c * LANES:(c + 1) * LANES])
        xa, xb, _ = _split3(x1)
        logits = (br_ref[...] + jnp.dot(xa, wr_ref[0], preferred_element_type=F32)
                  + jnp.dot(xb, wr_ref[0], preferred_element_type=F32)
                  + jnp.dot(xa, wr_ref[1], preferred_element_type=F32))
        route_ref[rows, :] = _route(logits)


def _post(attn, hn, xc, rest, x, w_pa, w_pb, w_out, gn_g, skip, ln_g, ln_b, w_route, b_route):
    n_tok = x.shape[0]
    sub = 256
    tm = 2 * sub if n_tok % (2 * sub) == 0 else sub
    row = lambda col: pl.BlockSpec((tm, D_MODEL), lambda i, col=col: (i, col))
    full = lambda shape: pl.BlockSpec(shape, lambda i: tuple(0 for _ in shape))
    vec = full((1, D_MODEL))
    return pl.pallas_call(
        functools.partial(_post_kernel, sub=sub),
        grid=(n_tok // tm,),
        in_specs=[row(0), row(0), row(0), row(1), row(2), row(3), row(0),
                  full((D_MODEL, D_MODEL)), full((D_MODEL, D_MODEL)), full((D_MODEL, D_MODEL)),
                  vec, vec, vec, vec,
                  full((2, D_MODEL, LANES)), full((1, LANES))],
        out_specs=[pl.BlockSpec((tm * ROW_CHUNKS, LANES), lambda i: (i, 0)),
                   pl.BlockSpec((tm, LANES), lambda i: (i, 0))],
        out_shape=[jax.ShapeDtypeStruct((n_tok * ROW_CHUNKS, LANES), F32),
                   jax.ShapeDtypeStruct((n_tok, LANES), F32)],
        compiler_params=_cparams(("parallel",)),
    )(attn, hn, xc, rest, rest, rest, x, w_pa, w_pb, w_out, gn_g, skip, ln_g, ln_b,
      w_route, b_route)


def _moe_kernel(bexp_ref, nused_ref,
                tok_hbm, dst_hbm, gate_ref, x_hbm, wg_ref, wu_ref, wd_ref, y_hbm,
                tok_smem, dst_smem, xbuf, obuf, idx_sem, gat_sem, sca_sem):
    i = pl.program_id(0)
    n_used = nused_ref[0]
    p = i % 2

    def index_copies(blk):
        slot = blk % 3
        return (pltpu.make_async_copy(tok_hbm.at[blk], tok_smem.at[slot], idx_sem.at[0, slot]),
                pltpu.make_async_copy(dst_hbm.at[blk], dst_smem.at[slot], idx_sem.at[1, slot]))

    def start_indices(blk):
        for cp in index_copies(blk):
            cp.start()

    def wait_indices(blk):
        for cp in index_copies(blk):
            cp.wait()

    def tile_rows(r):
        return pl.ds(pl.multiple_of(r * ROW_CHUNKS, ROW_CHUNKS), ROW_CHUNKS)

    def start_gather(blk, slot):
        islot = blk % 3

        for g in range(MOE_ROWS // LANES):
            def body(r2, carry, g=g):
                for prio in range(DMA_PRIORITIES):
                    r = r2 * DMA_PRIORITIES + prio
                    t = tok_smem[islot, g, r]
                    pltpu.make_async_copy(x_hbm.at[t], xbuf.at[slot, tile_rows(g * LANES + r)],
                                          gat_sem.at[slot]).start(priority=prio)
                return carry
            lax.fori_loop(0, LANES // DMA_PRIORITIES, body, 0, unroll=4)

    def wait_gather(slot):
        pltpu.make_async_copy(xbuf.at[slot], xbuf.at[slot], gat_sem.at[slot]).wait()

    def start_scatter(blk, slot):
        islot = blk % 3

        for g in range(MOE_ROWS // LANES):
            def body(r2, carry, g=g):
                for prio in range(DMA_PRIORITIES):
                    r = r2 * DMA_PRIORITIES + prio
                    t = dst_smem[islot, g, r]
                    pltpu.make_async_copy(obuf.at[slot, tile_rows(g * LANES + r)], y_hbm.at[t],
                                          sca_sem.at[slot]).start(priority=prio)
                return carry
            lax.fori_loop(0, LANES // DMA_PRIORITIES, body, 0, unroll=4)

    def wait_scatter(slot):
        pltpu.make_async_copy(obuf.at[slot], obuf.at[slot], sca_sem.at[slot]).wait()

    @pl.when(i == 0)
    def _():
        start_indices(0)
        wait_indices(0)
        start_gather(0, 0)

        @pl.when(1 < n_used)
        def _():
            start_indices(1)

    @pl.when(i + 1 < n_used)
    def _():
        wait_indices(i + 1)
        start_gather(i + 1, 1 - p)

    @pl.when(i + 2 < n_used)
    def _():
        start_indices(i + 2)

    @pl.when(i < n_used)
    def _():
        wait_gather(p)

        @pl.when(i >= 2)
        def _():
            wait_scatter(p)

        xb = jnp.concatenate(
            [xbuf[p, pl.ds(c, MOE_ROWS, stride=ROW_CHUNKS), :] for c in range(ROW_CHUNKS)],
            axis=1).astype(BF16)
        hg = jnp.dot(xb, wg_ref[...], preferred_element_type=F32)
        hu = jnp.dot(xb, wu_ref[...], preferred_element_type=F32)
        hh = (hg * _sigmoid(hg) * hu).astype(BF16)
        out = jnp.dot(hh, wd_ref[...], preferred_element_type=F32) * gate_ref[...]
        for c in range(ROW_CHUNKS):
            obuf[p, pl.ds(c, MOE_ROWS, stride=ROW_CHUNKS), :] = out[:, c * LANES:(c + 1) * LANES]
        start_scatter(i, p)

        @pl.when(i == n_used - 1)
        def _():
            wait_scatter(p)

            @pl.when(i >= 1)
            def _():
                wait_scatter(1 - p)


def _moe(x1t, slot_tok, slot_dst, slot_gate, blk_exp, n_used, we_g, we_u, we_d):
    n_tok = x1t.shape[0] // ROW_CHUNKS
    n_blk = blk_exp.shape[0]
    n_out = TOP_K * n_tok + 2 * MOE_ROWS
    wspec_in = pl.BlockSpec((None, D_MODEL, D_EXPERT), lambda i, be, nu: (be[i], 0, 0))
    wspec_out = pl.BlockSpec((None, D_EXPERT, D_MODEL), lambda i, be, nu: (be[i], 0, 0))
    grid_spec = pltpu.PrefetchScalarGridSpec(
        num_scalar_prefetch=2,
        grid=(n_blk,),
        in_specs=[
            pl.BlockSpec(memory_space=pl.ANY),
            pl.BlockSpec(memory_space=pl.ANY),
            pl.BlockSpec((MOE_ROWS, 1), lambda i, be, nu: (i, 0)),
            pl.BlockSpec(memory_space=pl.ANY),
            wspec_in, wspec_in, wspec_out,
        ],
        out_specs=pl.BlockSpec(memory_space=pl.ANY),
        scratch_shapes=[
            pltpu.SMEM((3, MOE_ROWS // LANES, LANES), jnp.int32),
            pltpu.SMEM((3, MOE_ROWS // LANES, LANES), jnp.int32),
            pltpu.VMEM((2, MOE_ROWS * ROW_CHUNKS, LANES), F32),
            pltpu.VMEM((2, MOE_ROWS * ROW_CHUNKS, LANES), F32),
            pltpu.SemaphoreType.DMA((2, 3)),
            pltpu.SemaphoreType.DMA((2,)),
            pltpu.SemaphoreType.DMA((2,)),
        ],
    )
    y = pl.pallas_call(
        _moe_kernel,
        grid_spec=grid_spec,
        out_shape=jax.ShapeDtypeStruct((n_out, ROW_CHUNKS, LANES), F32),
        compiler_params=_cparams(("arbitrary",)),
    )(blk_exp, n_used, slot_tok.reshape(n_blk, MOE_ROWS // LANES, LANES),
      slot_dst.reshape(n_blk, MOE_ROWS // LANES, LANES), slot_gate.reshape(n_blk * MOE_ROWS, 1),
      x1t.reshape(n_tok, ROW_CHUNKS, LANES), we_g, we_u, we_d)
    return y.reshape(n_out * ROW_CHUNKS, LANES)


def _route_plan(route, n_tok):
    n_assign = n_tok * TOP_K
    eid = route[:, 0:TOP_K].astype(jnp.int32).reshape(n_assign)
    gate = route[:, TOP_K:2 * TOP_K].reshape(n_assign)
    order = jnp.argsort(eid).astype(jnp.int32)
    counts = jnp.zeros((N_EXPERTS,), jnp.int32).at[eid].add(1)
    offsets = jnp.cumsum(counts) - counts
    padded = (counts + MOE_ROWS - 1) // MOE_ROWS * MOE_ROWS
    pad_end = jnp.cumsum(padded)
    pad_start = pad_end - padded
    n_blk = (n_assign + MOE_ROWS - 1) // MOE_ROWS + N_EXPERTS
    blk_start = jnp.arange(n_blk, dtype=jnp.int32) * MOE_ROWS
    blk_exp = jnp.minimum(jnp.sum(pad_end[None, :] <= blk_start[:, None], axis=1),
                          N_EXPERTS - 1).astype(jnp.int32)
    row = jnp.arange(MOE_ROWS, dtype=jnp.int32)[None, :]
    j = (blk_start - pad_start[blk_exp])[:, None] + row
    valid = j < counts[blk_exp][:, None]
    src = jnp.clip(offsets[blk_exp][:, None] + j, 0, n_assign - 1)
    assign = order[src]
    tok = assign // TOP_K
    dummy = TOP_K * n_tok + (jnp.arange(n_blk, dtype=jnp.int32) % 2)[:, None] * MOE_ROWS + row
    slot_tok = jnp.where(valid, tok, 0)
    slot_dst = jnp.where(valid, (assign % TOP_K) * n_tok + tok, dummy)
    slot_gate = jnp.where(valid, gate[assign], 0.0)
    n_used = (pad_end[-1:] // MOE_ROWS).astype(jnp.int32)
    return slot_tok, slot_dst, slot_gate, blk_exp, n_used


def _combine_rows(x_ref, y0_ref, y1_ref, g_ref, b_ref, tm):
    def chunk(ref, c):
        return ref[pl.ds(c, tm, stride=ROW_CHUNKS), :]

    y = jnp.concatenate(
        [DN_ALPHA * chunk(x_ref, c) + (chunk(y0_ref, c) + chunk(y1_ref, c))
         for c in range(ROW_CHUNKS)], axis=1)
    return _layer_norm(y, g_ref[...], b_ref[...])


def _combine_kernel(x_ref, y0_ref, y1_ref, g_ref, b_ref, o_ref):
    o_ref[...] = _combine_rows(x_ref, y0_ref, y1_ref, g_ref, b_ref, o_ref.shape[0])


def _combine(x1t, y, ln_g, ln_b):
    n_tok = x1t.shape[0] // ROW_CHUNKS
    tm = 256
    n_t = n_tok // tm
    vec = pl.BlockSpec((1, D_MODEL), lambda i: (0, 0))
    tiles = lambda off: pl.BlockSpec((tm * ROW_CHUNKS, LANES), lambda i, off=off: (i + off, 0))
    return pl.pallas_call(
        _combine_kernel,
        grid=(n_t,),
        in_specs=[tiles(0), tiles(0), tiles(n_t), vec, vec],
        out_specs=pl.BlockSpec((tm, D_MODEL), lambda i: (i, 0)),
        out_shape=jax.ShapeDtypeStruct((n_tok, D_MODEL), F32),
        compiler_params=_cparams(("parallel",)),
    )(x1t, y, y, ln_g, ln_b)


def _rope_tables(seq_len):
    half = DA_HEAD_DIM // 2
    inv = ROPE_THETA ** (-jnp.arange(half, dtype=F32) * (2.0 / DA_HEAD_DIM))
    ang = jnp.arange(seq_len, dtype=F32)[:, None] * inv[None, :]
    cos = jnp.tile(jnp.cos(ang), (1, LANES // half))
    sin = jnp.tile(jnp.sin(ang), (1, LANES // half))
    lane = jnp.arange(LANES)
    sign = jnp.where((lane % DA_HEAD_DIM) < half, -1.0, 1.0).astype(F32)
    return cos, sin * sign[None, :]


def _block_diag_tiles(w):
    per_tile = MXU_DIM // ML_PROJ_BLOCK
    n_tiles = w.shape[0] // per_tile
    w4 = w.reshape(n_tiles, per_tile, ML_PROJ_BLOCK, ML_PROJ_BLOCK)
    eye = jnp.eye(per_tile, dtype=w.dtype)
    bd = jnp.einsum('jgio,gh->jgiho', w4, eye)
    return bd.reshape(n_tiles, MXU_DIM, MXU_DIM).astype(BF16)


def _gate_perm():
    idx = []
    for h in range(ML_HEADS):
        for d in range(2):
            for kind in range(2):
                idx.append(d * 2 * ML_HEADS + kind * ML_HEADS + h)
    return jnp.array(idx, dtype=jnp.int32)


def _layer(src, n_seq, seq_len, lambda_init, cos, sin, p):
    qk_w = DA_HEADS * 2 * DA_HEAD_DIM
    v_w = DA_HEADS * DA_V_DIM
    x, qkv, rest = _inproj(src, p['w_in'].astype(BF16), cos, sin, seq_len, n_rope=2 * qk_w,
                           n_scaled=qk_w, n_qkv=2 * qk_w + v_w,
                           scale=DA_HEAD_DIM ** -0.5 * math.log2(math.e))
    n_tok = x.shape[0]

    lam = (jnp.exp(jnp.sum(p['lq1'] * p['lk1'])) - jnp.exp(jnp.sum(p['lq2'] * p['lk2']))
           + lambda_init)
    scalars = jnp.stack([lam, jnp.asarray(1.0 - lambda_init, F32)]).astype(F32)
    attn = _attention(qkv, scalars, p['subln_g'].reshape(1, DA_V_DIM), n_seq, seq_len)

    perm = _gate_perm()
    n_gate = 4 * ML_HEADS
    wg = p['w_gate'][:, perm].reshape(3, D_MODEL, n_gate)
    wg = jnp.pad(wg, ((0, 0), (0, 0), (0, LANES - n_gate))).astype(BF16)
    bg = jnp.pad(p['b_gate'][perm], (0, LANES - n_gate)).reshape(1, LANES)
    q, kt, v, xc, g1, g2, g3 = _mlstm_pre(
        rest, p['conv_w'], p['conv_b'].reshape(1, D_MODEL), _block_diag_tiles(p['wq']),
        _block_diag_tiles(p['wk']), _block_diag_tiles(p['wv']), wg, bg, n_seq, seq_len)

    def per_head(t, perm):
        t = t[:, :, :n_gate].reshape(n_seq, seq_len, ML_HEADS, 4)
        return jnp.transpose(t, perm)

    hn = _mlstm(q, kt, v, per_head(g1, (0, 2, 1, 3)), per_head(g2, (0, 2, 1, 3)),
                per_head(g3, (0, 2, 3, 1)), n_seq, seq_len)

    n_route = N_GROUPS + N_EXPERTS
    w_route = jnp.concatenate([p['rg_w'], p['re_w']], axis=1)
    w_route = jnp.pad(w_route, ((0, 0), (0, LANES - n_route)))
    w_route = jnp.stack(_split3(w_route)[:2])
    b_route = jnp.pad(jnp.concatenate([p['rg_b'], p['re_b']]), (0, LANES - n_route))
    vec = lambda a: a.reshape(1, D_MODEL)
    x1, route = _post(attn, hn, xc, rest, x, p['w_pa'].astype(BF16), p['w_pb'].astype(BF16),
                      p['w_out'].astype(BF16), vec(p['gn_g']), vec(p['skip']), vec(p['ln1_g']),
                      vec(p['ln1_b']), w_route, b_route.reshape(1, LANES))

    plan = _route_plan(route, n_tok)
    y = _moe(x1, *plan, p['we_g'].astype(BF16), p['we_u'].astype(BF16), p['we_d'].astype(BF16))
    return x1, y, vec(p['ln2_g']), vec(p['ln2_b'])


def kernel(x_prompt, x_sample, w_in, da_lambda_q1, da_lambda_k1, da_lambda_q2, da_lambda_k2, da_subln_g, ml_conv_w, ml_conv_b, ml_wq, ml_wk, ml_wv, ml_w_gate, ml_b_gate, ml_skip, ml_gn_g, w_pa, w_pb, w_out, ln1_g, ln1_b, router_group_w, router_group_b, router_expert_w, router_expert_b, w_e_gate, w_e_up, w_e_down, ln2_g, ln2_b):
    n_p, seq_len, d = x_prompt.shape
    n_s = x_sample.shape[0]
    assert x_sample.shape[1] == seq_len and d == D_MODEL and seq_len % CHUNK == 0
    n_seq = n_p + n_s
    x = jnp.concatenate([x_prompt, x_sample], axis=0).reshape(n_seq * seq_len, d)
    cos, sin = _rope_tables(seq_len)
    stacked = dict(w_in=w_in, lq1=da_lambda_q1, lk1=da_lambda_k1, lq2=da_lambda_q2,
                   lk2=da_lambda_k2, subln_g=da_subln_g, conv_w=ml_conv_w, conv_b=ml_conv_b,
                   wq=ml_wq, wk=ml_wk, wv=ml_wv, w_gate=ml_w_gate, b_gate=ml_b_gate,
                   skip=ml_skip, gn_g=ml_gn_g, w_pa=w_pa, w_pb=w_pb, w_out=w_out,
                   ln1_g=ln1_g, ln1_b=ln1_b, rg_w=router_group_w, rg_b=router_group_b,
                   re_w=router_expert_w, re_b=router_expert_b, we_g=w_e_gate, we_u=w_e_up,
                   we_d=w_e_down, ln2_g=ln2_g, ln2_b=ln2_b)
    src = x
    for l in range(w_in.shape[0]):
        lambda_init = 0.8 - 0.6 * math.exp(-0.3 * l)
        src = _layer(src, n_seq, seq_len, lambda_init, cos, sin,
                     {k: a[l] for k, a in stacked.items()})
    y = _combine(*src).reshape(n_seq, seq_len, d)
    return (y[:n_p], y[n_p:])
```

```python
import functools
import math

import jax
import jax.numpy as jnp
from jax import lax
from jax.experimental import pallas as pl
from jax.experimental.pallas import tpu as pltpu

F32 = jnp.float32
BF16 = jnp.bfloat16

D_MODEL = 1024
DEPTH = 4
DA_HEADS = 8
DA_HEAD_DIM = 64
DA_V_DIM = 128
ROPE_THETA = 10000.0
ML_HEADS = 4
ML_HEAD_DIM = 256
ML_PROJ_BLOCK = 4
N_GROUPS = 4
EXPERTS_PER_GROUP = 8
N_EXPERTS = 32
TOP_K = 2
D_EXPERT = 512
DN_ALPHA = (2 * DEPTH) ** 0.25
LN_EPS = 1e-5

LANES = 128
MXU_DIM = 256
ROW_CHUNKS = D_MODEL // LANES
CHUNK = 128
MOE_ROWS = 512
DMA_PRIORITIES = 2
VMEM_LIMIT = 56 * 1024 * 1024

NEG_BIG = -1e30


def _cparams(sem):
    return pltpu.CompilerParams(dimension_semantics=sem, vmem_limit_bytes=VMEM_LIMIT)


def _sigmoid(x):
    return 0.5 * jnp.tanh(0.5 * x) + 0.5


def _inproj_kernel(*refs, fused, n_rope, n_scaled, n_qkv, scale, cw):
    if fused:
        (x1t_ref, y0_ref, y1_ref, g_ref, b_ref, w_ref, cos_ref, sin_ref,
         x_out_ref, qkv_ref, rest_ref) = refs
        x = _combine_rows(x1t_ref, y0_ref, y1_ref, g_ref, b_ref, qkv_ref.shape[0])
        x_out_ref[...] = x
    else:
        x_ref, w_ref, cos_ref, sin_ref, qkv_ref, rest_ref = refs
        x = x_ref[...]
    xb = x.astype(BF16)
    tm = qkv_ref.shape[0]
    n_cols = w_ref.shape[1]
    cos = cos_ref[...]
    sin = sin_ref[...]
    lane = lax.broadcasted_iota(jnp.int32, (tm, LANES), 1)
    first_half = (lane % DA_HEAD_DIM) < (DA_HEAD_DIM // 2)
    for c in range(n_cols // cw):
        acc = jnp.dot(xb, w_ref[:, c * cw:(c + 1) * cw], preferred_element_type=F32)
        for s in range(cw // LANES):
            col = c * cw + s * LANES
            t = acc[:, s * LANES:(s + 1) * LANES]
            if col < n_rope:
                rot = jnp.where(first_half, pltpu.roll(t, LANES - DA_HEAD_DIM // 2, 1),
                                pltpu.roll(t, DA_HEAD_DIM // 2, 1))
                t = t * cos + rot * sin
                if col < n_scaled:
                    t = t * scale
            if col < n_qkv:
                qkv_ref[:, col:col + LANES] = t.astype(qkv_ref.dtype)
            else:
                rest_ref[:, col - n_qkv:col - n_qkv + LANES] = t.astype(rest_ref.dtype)


def _inproj(src, w, cos, sin, seq_len, *, n_rope, n_scaled, n_qkv, scale):
    fused = isinstance(src, tuple)
    d, n_cols = w.shape
    n_tok = src[0].shape[0] // ROW_CHUNKS if fused else src.shape[0]
    tm = min(512, seq_len)
    n_t = n_tok // tm
    per_seq = seq_len // tm
    kern = functools.partial(_inproj_kernel, fused=fused, n_rope=n_rope, n_scaled=n_scaled,
                             n_qkv=n_qkv, scale=scale, cw=512)
    table = pl.BlockSpec((tm, LANES), lambda i: (i % per_seq, 0))
    wspec = pl.BlockSpec((d, n_cols), lambda i: (0, 0), pipeline_mode=pl.Buffered(1))
    rows = lambda n: pl.BlockSpec((tm, n), lambda i: (i, 0))
    outs = [rows(n_qkv), rows(n_cols - n_qkv)]
    out_shapes = [jax.ShapeDtypeStruct((n_tok, n_qkv), BF16),
                  jax.ShapeDtypeStruct((n_tok, n_cols - n_qkv), BF16)]
    if fused:
        x1t, y, ln_g, ln_b = src
        tiles = lambda off: pl.BlockSpec((tm * ROW_CHUNKS, LANES), lambda i, off=off: (i + off, 0))
        vec = pl.BlockSpec((1, d), lambda i: (0, 0))
        x, qkv, rest = pl.pallas_call(
            kern, grid=(n_t,),
            in_specs=[tiles(0), tiles(0), tiles(n_t), vec, vec, wspec, table, table],
            out_specs=[rows(d)] + outs,
            out_shape=[jax.ShapeDtypeStruct((n_tok, d), F32)] + out_shapes,
            compiler_params=_cparams(("parallel",)),
        )(x1t, y, y, ln_g, ln_b, w, cos, sin)
        return x, qkv, rest
    qkv, rest = pl.pallas_call(
        kern, grid=(n_t,),
        in_specs=[rows(d), wspec, table, table],
        out_specs=outs, out_shape=out_shapes,
        compiler_params=_cparams(("parallel",)),
    )(src, w, cos, sin)
    return src, qkv, rest


def _attn_kernel(sc_ref, q_ref, k_ref, v_ref, g_ref, o_ref, vx_ref, *, bq):
    seq_len = q_ref.shape[0]
    lam = sc_ref[0]
    out_scale = sc_ref[1]
    k = k_ref[...]
    vx_ref[:, :DA_V_DIM] = v_ref[...]
    vx_ref[:, DA_V_DIM:] = jnp.ones((seq_len, DA_V_DIM), vx_ref.dtype)
    vx = vx_ref[...]
    g = g_ref[...] * out_scale
    lane = lax.broadcasted_iota(jnp.int32, (bq, LANES), 1)
    is_first = lane < DA_HEAD_DIM
    dn = (((1,), (1,)), ((), ()))

    def softmax_av(qm):
        s = lax.dot_general(qm, k, dn, preferred_element_type=F32)
        p = jnp.exp2(s - jnp.max(s, axis=-1, keepdims=True)).astype(BF16)
        ox = jnp.dot(p, vx, preferred_element_type=F32)
        return ox[:, :DA_V_DIM] / ox[:, DA_V_DIM:]

    for i in range(seq_len // bq):
        rows = slice(i * bq, (i + 1) * bq)
        qb = q_ref[rows, :]
        zero = jnp.zeros_like(qb)
        o = (softmax_av(jnp.where(is_first, qb, zero))
             - lam * softmax_av(jnp.where(is_first, zero, qb)))
        o = o * lax.rsqrt(jnp.mean(o * o, axis=-1, keepdims=True) + LN_EPS)
        o_ref[rows, :] = (o * g).astype(o_ref.dtype)


def _attention(qkv, scalars, subln_g, n_seq, seq_len):
    n_tok = qkv.shape[0]
    bq = min(128, seq_len)
    return pl.pallas_call(
        functools.partial(_attn_kernel, bq=bq),
        grid=(n_seq, DA_HEADS),
        in_specs=[
            pl.BlockSpec(memory_space=pltpu.SMEM),
            pl.BlockSpec((seq_len, LANES), lambda b, h: (b, h)),
            pl.BlockSpec((seq_len, LANES), lambda b, h: (b, DA_HEADS + h)),
            pl.BlockSpec((seq_len, LANES), lambda b, h: (b, 2 * DA_HEADS + h)),
            pl.BlockSpec((1, DA_V_DIM), lambda b, h: (0, 0)),
        ],
        out_specs=pl.BlockSpec((seq_len, DA_V_DIM), lambda b, h: (b, h)),
        out_shape=jax.ShapeDtypeStruct((n_tok, DA_HEADS * DA_V_DIM), BF16),
        scratch_shapes=[pltpu.VMEM((seq_len, 2 * DA_V_DIM), BF16)],
        compiler_params=_cparams(("parallel", "parallel")),
    )(scalars, qkv, qkv, qkv, subln_g)


def _split3(x):
    x1 = x.astype(BF16)
    r1 = x - x1.astype(F32)
    x2 = r1.astype(BF16)
    x3 = (r1 - x2.astype(F32)).astype(BF16)
    return x1, x2, x3


def _gate_tables(gacc_ref, bg_ref, g1_ref, g2_ref, g3_ref, tmp_ref, last_ref):
    seq_len = gacc_ref.shape[0]
    n_chunks = seq_len // CHUNK
    r = lax.broadcasted_iota(jnp.int32, (CHUNK, CHUNK), 0)
    c = lax.broadcasted_iota(jnp.int32, (CHUNK, CHUNK), 1)
    tri = jnp.where(c <= r, 1.0, 0.0).astype(BF16)
    lane = lax.broadcasted_iota(jnp.int32, (CHUNK, LANES), 1)
    row = lax.broadcasted_iota(jnp.int32, (CHUNK, LANES), 0)
    is_kind0 = (lane % 2) == 0
    is_bwd = ((lane // 2) % 2) == 1
    is_bwd_row = is_bwd[0:1, :]
    bg = bg_ref[...]

    def first_pass(ci, carry):
        rows = pl.ds(pl.multiple_of(ci * CHUNK, CHUNK), CHUNK)
        pre = gacc_ref[rows, :] + bg
        lf = jnp.minimum(pre, 0.0) - jnp.log1p(jnp.exp(-jnp.abs(pre)))
        l1, l2, l3 = _split3(lf)
        pref = (jnp.dot(tri, l1, preferred_element_type=F32)
                + jnp.dot(tri, l2, preferred_element_type=F32)
                + jnp.dot(tri, l3, preferred_element_type=F32))
        suff = pref[CHUNK - 1:CHUNK, :] - pref + lf
        cum = jnp.where(is_bwd, suff, pref)
        b0 = pltpu.roll(cum, LANES - 1, 1)
        g = pre - b0
        mx_f = g
        mx_b = g
        s = 1
        while s < CHUNK:
            mx_f = jnp.maximum(mx_f, jnp.where(row >= s, pltpu.roll(mx_f, s, 0), -jnp.inf))
            mx_b = jnp.maximum(mx_b, jnp.where(row < CHUNK - s,
                                               pltpu.roll(mx_b, CHUNK - s, 0), -jnp.inf))
            s *= 2
        mx = jnp.where(is_bwd, mx_b, mx_f)
        tmp_ref[0, rows, :] = g
        tmp_ref[1, rows, :] = mx
        tmp_ref[2, rows, :] = b0
        last_ref[0, pl.ds(ci, 1), :] = jnp.where(is_bwd_row, b0[0:1, :], b0[CHUNK - 1:CHUNK, :])
        last_ref[1, pl.ds(ci, 1), :] = jnp.where(is_bwd_row, mx[0:1, :], mx[CHUNK - 1:CHUNK, :])
        return carry

    lax.fori_loop(0, n_chunks, first_pass, 0)

    m_f = jnp.zeros((1, LANES), F32)
    m_b = jnp.zeros((1, LANES), F32)
    for t in range(n_chunks):
        cf, cb = t, n_chunks - 1 - t
        last_ref[2, cf:cf + 1, :] = m_f
        last_ref[3, cb:cb + 1, :] = m_b
        m_f = last_ref[0, cf:cf + 1, :] + jnp.maximum(m_f, last_ref[1, cf:cf + 1, :])
        m_b = last_ref[0, cb:cb + 1, :] + jnp.maximum(m_b, last_ref[1, cb:cb + 1, :])

    def second_pass(ci, carry):
        rows = pl.ds(pl.multiple_of(ci * CHUNK, CHUNK), CHUNK)
        g = tmp_ref[0, rows, :]
        mx = tmp_ref[1, rows, :]
        b0 = tmp_ref[2, rows, :]
        m_st = jnp.where(is_bwd_row, last_ref[3, pl.ds(ci, 1), :], last_ref[2, pl.ds(ci, 1), :])
        mm = jnp.maximum(mx, m_st)
        m_up = jnp.maximum(m_st, last_ref[1, pl.ds(ci, 1), :])
        keep = jnp.broadcast_to(jnp.exp(m_st - m_up), (CHUNK, LANES))
        g1_ref[rows, :] = jnp.where(is_kind0, -mm, pltpu.roll(jnp.exp(-mm - b0), 1, 1))
        g2_ref[rows, :] = jnp.where(is_kind0, jnp.exp(m_st - mm), pltpu.roll(keep, 1, 1))
        g3_ref[rows, :] = jnp.where(is_kind0, g, pltpu.roll(jnp.exp(g - m_up), 1, 1))
        return carry

    lax.fori_loop(0, n_chunks, second_pass, 0)


def _mlpre_kernel(xm_ref, cw_ref, cb_ref, wq_ref, wkt_ref, wk_ref, wv_ref, wg_ref, bg_ref,
                  q_ref, kt_ref, v_ref, xc_ref, g1_ref, g2_ref, g3_ref,
                  gacc_ref, tmp_ref, last_ref):
    j = pl.program_id(1)
    seq_len = xm_ref.shape[0]
    xm = xm_ref[...].astype(F32)
    row = lax.broadcasted_iota(jnp.int32, xm.shape, 0)
    prev = jnp.where(row == 0, 0.0, pltpu.roll(xm, 1, 0))
    nxt = jnp.where(row == seq_len - 1, 0.0, pltpu.roll(xm, seq_len - 1, 0))
    xc = cb_ref[...] + prev * cw_ref[0:1, :] + xm * cw_ref[1:2, :] + nxt * cw_ref[2:3, :]
    xc = xc * _sigmoid(xc)
    xcb = xc.astype(BF16)
    xc_ref[...] = xcb
    q = jnp.dot(xcb, wq_ref[...], preferred_element_type=F32)
    k = jnp.dot(xcb, wk_ref[...], preferred_element_type=F32)
    v = jnp.dot(xm_ref[...], wv_ref[...], preferred_element_type=F32)
    qb = q.astype(BF16)
    kb = k.astype(BF16)
    vb = v.astype(BF16)
    q_ref[...] = (q * (ML_HEAD_DIM ** -0.5)).astype(BF16)
    kt_ref[...] = lax.dot_general(wkt_ref[...], xcb, (((1,), (1,)), ((), ())),
                                  preferred_element_type=F32).astype(BF16)
    v_ref[...] = vb
    part = (jnp.dot(qb, wg_ref[0], preferred_element_type=F32)
            + jnp.dot(kb, wg_ref[1], preferred_element_type=F32)
            + jnp.dot(vb, wg_ref[2], preferred_element_type=F32))

    @pl.when(j == 0)
    def _():
        gacc_ref[...] = part

    @pl.when(j > 0)
    def _():
        gacc_ref[...] += part

    @pl.when(j == pl.num_programs(1) - 1)
    def _():
        _gate_tables(gacc_ref, bg_ref, g1_ref, g2_ref, g3_ref, tmp_ref, last_ref)


def _mlstm_pre(rest, conv_w, conv_b, wq_bd, wk_bd, wv_bd, wg, bg, n_seq, seq_len):
    n_tok = rest.shape[0]
    n_ct = D_MODEL // MXU_DIM
    tile = pl.BlockSpec((seq_len, MXU_DIM), lambda b, j: (b, j))
    wspec = pl.BlockSpec((None, MXU_DIM, MXU_DIM), lambda b, j: (j, 0, 0))
    gspec = pl.BlockSpec((None, seq_len, LANES), lambda b, j: (b, 0, 0))
    act = jax.ShapeDtypeStruct((n_tok, D_MODEL), BF16)
    gate = jax.ShapeDtypeStruct((n_seq, seq_len, LANES), F32)
    return pl.pallas_call(
        _mlpre_kernel,
        grid=(n_seq, n_ct),
        in_specs=[
            tile,
            pl.BlockSpec((3, MXU_DIM), lambda b, j: (0, j)),
            pl.BlockSpec((1, MXU_DIM), lambda b, j: (0, j)),
            wspec, wspec, wspec, wspec,
            pl.BlockSpec((3, MXU_DIM, LANES), lambda b, j: (0, j, 0)),
            pl.BlockSpec((1, LANES), lambda b, j: (0, 0)),
        ],
        out_specs=[tile, pl.BlockSpec((MXU_DIM, seq_len), lambda b, j: (j, b)), tile, tile,
                   gspec, gspec, gspec],
        out_shape=[act, jax.ShapeDtypeStruct((D_MODEL, n_tok), BF16), act, act,
                   gate, gate, gate],
        scratch_shapes=[pltpu.VMEM((seq_len, LANES), F32),
                        pltpu.VMEM((3, seq_len, LANES), F32),
                        pltpu.VMEM((4, seq_len // CHUNK, LANES), F32)],
        compiler_params=_cparams(("parallel", "arbitrary")),
    )(rest, conv_w, conv_b, wq_bd, jnp.swapaxes(wk_bd, 1, 2), wk_bd, wv_bd, wg, bg)


def _mlstm_kernel(q_ref, kt_ref, v_ref, gc1_ref, gc2_ref, gr_ref, o_ref,
                  vx_ref, qk_ref, h_ref):
    seq_len = q_ref.shape[0]
    n_chunks = seq_len // CHUNK
    dh = ML_HEAD_DIM
    r = lax.broadcasted_iota(jnp.int32, (CHUNK, CHUNK), 0)
    c = lax.broadcasted_iota(jnp.int32, (CHUNK, CHUNK), 1)
    vx_ref[:, :dh] = v_ref[...]
    vx_ref[:, dh:] = jnp.ones((seq_len, LANES), vx_ref.dtype)

    def lane_replicated(ref, rows, col):
        return jnp.broadcast_to(ref[rows, col:col + 1], (CHUNK, LANES))

    def chunk_step(direction, ci, c_st):
        mask = (c <= r) if direction == 0 else (c >= r)
        rows = slice(ci * CHUNK, (ci + 1) * CHUNK)
        qc = q_ref[rows, :]
        ktc = kt_ref[:, rows]
        vxc = vx_ref[rows, :]
        neg_mm = lane_replicated(gc1_ref, rows, 2 * direction)
        e_mj = lane_replicated(gc1_ref, rows, 2 * direction + 1)
        inter = lane_replicated(gc2_ref, rows, 2 * direction)
        g_row = gr_ref[2 * direction:2 * direction + 1, rows]
        w_row = gr_ref[2 * direction + 1:2 * direction + 2, rows]
        keep = gc2_ref[ci * CHUNK:ci * CHUNK + 1, 2 * direction + 1:2 * direction + 2]
        if ci in first_visit:
            qk = qk_ref[ci]
        else:
            qk = jnp.dot(qc, ktc, preferred_element_type=F32)
            qk_ref[ci] = qk
        sw = qk * jnp.exp(jnp.where(mask, neg_mm + g_row, -jnp.inf))
        intra = jnp.dot(sw.astype(BF16), vxc, preferred_element_type=F32)
        carry_in = jnp.dot(qc, c_st.astype(BF16), preferred_element_type=F32)
        den = intra[:, dh:] + inter * carry_in[:, dh:]
        rdiv = 1.0 / jnp.maximum(jnp.abs(den), e_mj)
        h = jnp.concatenate(
            [(intra[:, s * LANES:(s + 1) * LANES] + inter * carry_in[:, s * LANES:(s + 1) * LANES])
             * rdiv for s in range(dh // LANES)], axis=1)
        if ci in first_visit:
            h = h_ref[rows, :] + h
            mu = jnp.mean(h, axis=-1, keepdims=True)
            hc = h - mu
            var = jnp.mean(hc * hc, axis=-1, keepdims=True)
            o_ref[rows, :] = (hc * lax.rsqrt(var + LN_EPS)).astype(o_ref.dtype)
        else:
            h_ref[rows, :] = h
            first_visit.add(ci)
        kw = (ktc.astype(F32) * w_row).astype(BF16)
        return keep * c_st + jnp.dot(kw, vxc, preferred_element_type=F32)

    first_visit = set()
    c_f = jnp.zeros((dh, dh + LANES), F32)
    c_b = jnp.zeros((dh, dh + LANES), F32)
    for step in range(n_chunks):
        c_f = chunk_step(0, step, c_f)
        c_b = chunk_step(1, n_chunks - 1 - step, c_b)


def _mlstm(q, kt, v, gc1, gc2, gr3, n_seq, seq_len):
    n_tok = q.shape[0]
    tile = pl.BlockSpec((seq_len, ML_HEAD_DIM), lambda b, h: (b, h))
    col = pl.BlockSpec((None, None, seq_len, 4), lambda b, h: (b, h, 0, 0))
    return pl.pallas_call(
        _mlstm_kernel,
        grid=(n_seq, ML_HEADS),
        in_specs=[
            tile,
            pl.BlockSpec((ML_HEAD_DIM, seq_len), lambda b, h: (h, b)),
            tile, col, col,
            pl.BlockSpec((None, None, 4, seq_len), lambda b, h: (b, h, 0, 0)),
        ],
        out_specs=tile,
        out_shape=jax.ShapeDtypeStruct((n_tok, D_MODEL), BF16),
        scratch_shapes=[
            pltpu.VMEM((seq_len, ML_HEAD_DIM + LANES), BF16),
            pltpu.VMEM((seq_len // CHUNK, CHUNK, CHUNK), F32),
            pltpu.VMEM((seq_len, ML_HEAD_DIM), F32),
        ],
        compiler_params=_cparams(("parallel", "parallel")),
    )(q, kt, v, gc1, gc2, gr3)


def _layer_norm(y, g, b):
    mu = jnp.mean(y, axis=-1, keepdims=True)
    yc = y - mu
    var = jnp.mean(yc * yc, axis=-1, keepdims=True)
    return yc * lax.rsqrt(var + LN_EPS) * g + b


def _route(logits):
    lane = lax.broadcasted_iota(jnp.int32, logits.shape, 1)
    big = jnp.int32(4 * LANES)
    gl = jnp.where(lane < N_GROUPS, logits, NEG_BIG)
    gmax = jnp.max(gl, axis=-1, keepdims=True)
    gsum = jnp.sum(jnp.where(lane < N_GROUPS, jnp.exp(gl - gmax), 0.0), axis=-1, keepdims=True)
    g_sel = jnp.min(jnp.where(gl == gmax, lane, big), axis=-1, keepdims=True)
    g_prob = 1.0 / gsum
    lo = N_GROUPS + EXPERTS_PER_GROUP * g_sel
    in_group = jnp.logical_and(lane >= lo, lane < lo + EXPERTS_PER_GROUP)
    el = jnp.where(in_group, logits, NEG_BIG)
    emax = jnp.max(el, axis=-1, keepdims=True)
    esum = jnp.sum(jnp.where(in_group, jnp.exp(el - emax), 0.0), axis=-1, keepdims=True)
    i1 = jnp.min(jnp.where(el == emax, lane, big), axis=-1, keepdims=True)
    el2 = jnp.where(lane == i1, NEG_BIG, el)
    emax2 = jnp.max(el2, axis=-1, keepdims=True)
    i2 = jnp.min(jnp.where(el2 == emax2, lane, big), axis=-1, keepdims=True)
    p1 = 1.0 / esum
    p2 = jnp.exp(emax2 - emax) / esum
    psum = p1 + p2
    gate1 = g_prob * p1 / psum
    gate2 = g_prob * p2 / psum
    e1 = (i1 - N_GROUPS).astype(F32)
    e2 = (i2 - N_GROUPS).astype(F32)
    return jnp.where(lane == 0, e1, jnp.where(lane == 1, e2, jnp.where(lane == 2, gate1, gate2)))


def _post_kernel(attn_ref, hn_ref, xc_ref, z_ref, ga_ref, gb_ref, x_ref,
                 wpa_ref, wpb_ref, wout_ref, gn_ref, skip_ref, lg_ref, lb_ref, wr_ref, br_ref,
                 x1t_ref, route_ref, *, sub):
    tm = x_ref.shape[0]
    for s in range(tm // sub):
        rows = slice(s * sub, (s + 1) * sub)
        z = z_ref[rows, :].astype(F32)
        ml = ((hn_ref[rows, :].astype(F32) * gn_ref[...]
               + skip_ref[...] * xc_ref[rows, :].astype(F32)) * (z * _sigmoid(z)))
        a_out = jnp.dot(attn_ref[rows, :], wpa_ref[...], preferred_element_type=F32)
        m_out = jnp.dot(ml.astype(BF16), wpb_ref[...], preferred_element_type=F32)
        mixed = (_sigmoid(ga_ref[rows, :].astype(F32)) * a_out
                 + _sigmoid(gb_ref[rows, :].astype(F32)) * m_out)
        y = DN_ALPHA * x_ref[rows, :] + jnp.dot(mixed.astype(BF16), wout_ref[...],
                                               preferred_element_type=F32)
        x1 = _layer_norm(y, lg_ref[...], lb_ref[...])
        for c in range(ROW_CHUNKS):
            x1t_ref[pl.ds(s * sub * ROW_CHUNKS + c, sub, stride=ROW_CHUNKS), :] = (
                x1[:, c * LANES:(c + 1) * LANES])
        xa, xb, _ = _split3(x1)
        logits = (br_ref[...] + jnp.dot(xa, wr_ref[0], preferred_element_type=F32)
                  + jnp.dot(xb, wr_ref[0], preferred_element_type=F32)
                  + jnp.dot(xa, wr_ref[1], preferred_element_type=F32))
        route_ref[rows, :] = _route(logits)


def _post(attn, hn, xc, rest, x, w_pa, w_pb, w_out, gn_g, skip, ln_g, ln_b, w_route, b_route):
    n_tok = x.shape[0]
    sub = 256
    tm = 2 * sub if n_tok % (2 * sub) == 0 else sub
    row = lambda col: pl.BlockSpec((tm, D_MODEL), lambda i, col=col: (i, col))
    full = lambda shape: pl.BlockSpec(shape, lambda i: tuple(0 for _ in shape))
    vec = full((1, D_MODEL))
    return pl.pallas_call(
        functools.partial(_post_kernel, sub=sub),
        grid=(n_tok // tm,),
        in_specs=[row(0), row(0), row(0), row(1), row(2), row(3), row(0),
                  full((D_MODEL, D_MODEL)), full((D_MODEL, D_MODEL)), full((D_MODEL, D_MODEL)),
                  vec, vec, vec, vec,
                  full((2, D_MODEL, LANES)), full((1, LANES))],
        out_specs=[pl.BlockSpec((tm * ROW_CHUNKS, LANES), lambda i: (i, 0)),
                   pl.BlockSpec((tm, LANES), lambda i: (i, 0))],
        out_shape=[jax.ShapeDtypeStruct((n_tok * ROW_CHUNKS, LANES), F32),
                   jax.ShapeDtypeStruct((n_tok, LANES), F32)],
        compiler_params=_cparams(("parallel",)),
    )(attn, hn, xc, rest, rest, rest, x, w_pa, w_pb, w_out, gn_g, skip, ln_g, ln_b,
      w_route, b_route)


def _moe_kernel(bexp_ref, nused_ref,
                tok_hbm, dst_hbm, gate_ref, x_hbm, wg_ref, wu_ref, wd_ref, y_hbm,
                tok_smem, dst_smem, xbuf, obuf, idx_sem, gat_sem, sca_sem):
    i = pl.program_id(0)
    n_used = nused_ref[0]
    p = i % 2

    def index_copies(blk):
        slot = blk % 3
        return (pltpu.make_async_copy(tok_hbm.at[blk], tok_smem.at[slot], idx_sem.at[0, slot]),
                pltpu.make_async_copy(dst_hbm.at[blk], dst_smem.at[slot], idx_sem.at[1, slot]))

    def start_indices(blk):
        for cp in index_copies(blk):
            cp.start()

    def wait_indices(blk):
        for cp in index_copies(blk):
            cp.wait()

    def tile_rows(r):
        return pl.ds(r * ROW_CHUNKS, ROW_CHUNKS)

    def start_gather(blk, slot):
        islot = blk % 3

        for row in range(MOE_ROWS):
            t = tok_smem[islot, row // LANES, row % LANES]
            pltpu.make_async_copy(x_hbm.at[t], xbuf.at[slot, tile_rows(row)],
                                  gat_sem.at[slot]).start(priority=row % DMA_PRIORITIES)

    def wait_gather(slot):
        pltpu.make_async_copy(xbuf.at[slot], xbuf.at[slot], gat_sem.at[slot]).wait()

    def start_scatter(blk, slot):
        islot = blk % 3

        for row in range(MOE_ROWS):
            t = dst_smem[islot, row // LANES, row % LANES]
            pltpu.make_async_copy(obuf.at[slot, tile_rows(row)], y_hbm.at[t],
                                  sca_sem.at[slot]).start(priority=row % DMA_PRIORITIES)

    def wait_scatter(slot):
        pltpu.make_async_copy(obuf.at[slot], obuf.at[slot], sca_sem.at[slot]).wait()

    @pl.when(i == 0)
    def _():
        start_indices(0)
        wait_indices(0)
        start_gather(0, 0)

        @pl.when(1 < n_used)
        def _():
            start_indices(1)

    @pl.when(i + 1 < n_used)
    def _():
        wait_indices(i + 1)
        start_gather(i + 1, 1 - p)

    @pl.when(i + 2 < n_used)
    def _():
        start_indices(i + 2)

    @pl.when(i < n_used)
    def _():
        wait_gather(p)

        @pl.when(i >= 2)
        def _():
            wait_scatter(p)

        xb = jnp.concatenate(
            [xbuf[p, pl.ds(c, MOE_ROWS, stride=ROW_CHUNKS), :] for c in range(ROW_CHUNKS)],
            axis=1).astype(BF16)
        hg = jnp.dot(xb, wg_ref[...], preferred_element_type=F32)
        hu = jnp.dot(xb, wu_ref[...], preferred_element_type=F32)
        hh = (hg * _sigmoid(hg) * hu).astype(BF16)
        out = jnp.dot(hh, wd_ref[...], preferred_element_type=F32) * gate_ref[...]
        for c in range(ROW_CHUNKS):
            obuf[p, pl.ds(c, MOE_ROWS, stride=ROW_CHUNKS), :] = out[:, c * LANES:(c + 1) * LANES]
        start_scatter(i, p)

        @pl.when(i == n_used - 1)
        def _():
            wait_scatter(p)

            @pl.when(i >= 1)
            def _():
                wait_scatter(1 - p)


def _moe(x1t, slot_tok, slot_dst, slot_gate, blk_exp, n_used, we_g, we_u, we_d):
    n_tok = x1t.shape[0] // ROW_CHUNKS
    n_blk = blk_exp.shape[0]
    n_out = TOP_K * n_tok + 2 * MOE_ROWS
    wspec_in = pl.BlockSpec((None, D_MODEL, D_EXPERT), lambda i, be, nu: (be[i], 0, 0))
    wspec_out = pl.BlockSpec((None, D_EXPERT, D_MODEL), lambda i, be, nu: (be[i], 0, 0))
    grid_spec = pltpu.PrefetchScalarGridSpec(
        num_scalar_prefetch=2,
        grid=(n_blk,),
        in_specs=[
            pl.BlockSpec(memory_space=pl.ANY),
            pl.BlockSpec(memory_space=pl.ANY),
            pl.BlockSpec((MOE_ROWS, 1), lambda i, be, nu: (i, 0)),
            pl.BlockSpec(memory_space=pl.ANY),
            wspec_in, wspec_in, wspec_out,
        ],
        out_specs=pl.BlockSpec(memory_space=pl.ANY),
        scratch_shapes=[
            pltpu.SMEM((3, MOE_ROWS // LANES, LANES), jnp.int32),
            pltpu.SMEM((3, MOE_ROWS // LANES, LANES), jnp.int32),
            pltpu.VMEM((2, MOE_ROWS * ROW_CHUNKS, LANES), F32),
            pltpu.VMEM((2, MOE_ROWS * ROW_CHUNKS, LANES), F32),
            pltpu.SemaphoreType.DMA((2, 3)),
            pltpu.SemaphoreType.DMA((2,)),
            pltpu.SemaphoreType.DMA((2,)),
        ],
    )
    y = pl.pallas_call(
        _moe_kernel,
        grid_spec=grid_spec,
        out_shape=jax.ShapeDtypeStruct((n_out, ROW_CHUNKS, LANES), F32),
        compiler_params=_cparams(("arbitrary",)),
    )(blk_exp, n_used, slot_tok.reshape(n_blk, MOE_ROWS // LANES, LANES),
      slot_dst.reshape(n_blk, MOE_ROWS // LANES, LANES), slot_gate.reshape(n_blk * MOE_ROWS, 1),
      x1t.reshape(n_tok, ROW_CHUNKS, LANES), we_g, we_u, we_d)
    return y.reshape(n_out * ROW_CHUNKS, LANES)


def _route_plan(route, n_tok):
    n_assign = n_tok * TOP_K
    eid = route[:, 0:TOP_K].astype(jnp.int32).reshape(n_assign)
    gate = route[:, TOP_K:2 * TOP_K].reshape(n_assign)
    order = jnp.argsort(eid).astype(jnp.int32)
    counts = jnp.zeros((N_EXPERTS,), jnp.int32).at[eid].add(1)
    offsets = jnp.cumsum(counts) - counts
    padded = (counts + MOE_ROWS - 1) // MOE_ROWS * MOE_ROWS
    pad_end = jnp.cumsum(padded)
    pad_start = pad_end - padded
    n_blk = (n_assign + MOE_ROWS - 1) // MOE_ROWS + N_EXPERTS
    blk_start = jnp.arange(n_blk, dtype=jnp.int32) * MOE_ROWS
    blk_exp = jnp.minimum(jnp.sum(pad_end[None, :] <= blk_start[:, None], axis=1),
                          N_EXPERTS - 1).astype(jnp.int32)
    row = jnp.arange(MOE_ROWS, dtype=jnp.int32)[None, :]
    j = (blk_start - pad_start[blk_exp])[:, None] + row
    valid = j < counts[blk_exp][:, None]
    src = jnp.clip(offsets[blk_exp][:, None] + j, 0, n_assign - 1)
    assign = order[src]
    tok = assign // TOP_K
    dummy = TOP_K * n_tok + (jnp.arange(n_blk, dtype=jnp.int32) % 2)[:, None] * MOE_ROWS + row
    slot_tok = jnp.where(valid, tok, 0)
    slot_dst = jnp.where(valid, (assign % TOP_K) * n_tok + tok, dummy)
    slot_gate = jnp.where(valid, gate[assign], 0.0)
    n_used = (pad_end[-1:] // MOE_ROWS).astype(jnp.int32)
    return slot_tok, slot_dst, slot_gate, blk_exp, n_used


def _combine_rows(x_ref, y0_ref, y1_ref, g_ref, b_ref, tm):
    def chunk(ref, c):
        return ref[pl.ds(c, tm, stride=ROW_CHUNKS), :]

    y = jnp.concatenate(
        [DN_ALPHA * chunk(x_ref, c) + (chunk(y0_ref, c) + chunk(y1_ref, c))
         for c in range(ROW_CHUNKS)], axis=1)
    return _layer_norm(y, g_ref[...], b_ref[...])


def _combine_kernel(x_ref, y0_ref, y1_ref, g_ref, b_ref, o_ref):
    o_ref[...] = _combine_rows(x_ref, y0_ref, y1_ref, g_ref, b_ref, o_ref.shape[0])


def _combine(x1t, y, ln_g, ln_b):
    n_tok = x1t.shape[0] // ROW_CHUNKS
    tm = 256
    n_t = n_tok // tm
    vec = pl.BlockSpec((1, D_MODEL), lambda i: (0, 0))
    tiles = lambda off: pl.BlockSpec((tm * ROW_CHUNKS, LANES), lambda i, off=off: (i + off, 0))
    return pl.pallas_call(
        _combine_kernel,
        grid=(n_t,),
        in_specs=[tiles(0), tiles(0), tiles(n_t), vec, vec],
        out_specs=pl.BlockSpec((tm, D_MODEL), lambda i: (i, 0)),
        out_shape=jax.ShapeDtypeStruct((n_tok, D_MODEL), F32),
        compiler_params=_cparams(("parallel",)),
    )(x1t, y, y, ln_g, ln_b)


def _rope_tables(seq_len):
    half = DA_HEAD_DIM // 2
    inv = ROPE_THETA ** (-jnp.arange(half, dtype=F32) * (2.0 / DA_HEAD_DIM))
    ang = jnp.arange(seq_len, dtype=F32)[:, None] * inv[None, :]
    cos = jnp.tile(jnp.cos(ang), (1, LANES // half))
    sin = jnp.tile(jnp.sin(ang), (1, LANES // half))
    lane = jnp.arange(LANES)
    sign = jnp.where((lane % DA_HEAD_DIM) < half, -1.0, 1.0).astype(F32)
    return cos, sin * sign[None, :]


def _block_diag_tiles(w):
    per_tile = MXU_DIM // ML_PROJ_BLOCK
    n_tiles = w.shape[0] // per_tile
    w4 = w.reshape(n_tiles, per_tile, ML_PROJ_BLOCK, ML_PROJ_BLOCK)
    eye = jnp.eye(per_tile, dtype=w.dtype)
    bd = jnp.einsum('jgio,gh->jgiho', w4, eye)
    return bd.reshape(n_tiles, MXU_DIM, MXU_DIM).astype(BF16)


def _gate_perm():
    idx = []
    for h in range(ML_HEADS):
        for d in range(2):
            for kind in range(2):
                idx.append(d * 2 * ML_HEADS + kind * ML_HEADS + h)
    return jnp.array(idx, dtype=jnp.int32)


def _layer(src, n_seq, seq_len, lambda_init, cos, sin, p):
    qk_w = DA_HEADS * 2 * DA_HEAD_DIM
    v_w = DA_HEADS * DA_V_DIM
    x, qkv, rest = _inproj(src, p['w_in'].astype(BF16), cos, sin, seq_len, n_rope=2 * qk_w,
                           n_scaled=qk_w, n_qkv=2 * qk_w + v_w,
                           scale=DA_HEAD_DIM ** -0.5 * math.log2(math.e))
    n_tok = x.shape[0]

    lam = (jnp.exp(jnp.sum(p['lq1'] * p['lk1'])) - jnp.exp(jnp.sum(p['lq2'] * p['lk2']))
           + lambda_init)
    scalars = jnp.stack([lam, jnp.asarray(1.0 - lambda_init, F32)]).astype(F32)
    attn = _attention(qkv, scalars, p['subln_g'].reshape(1, DA_V_DIM), n_seq, seq_len)

    perm = _gate_perm()
    n_gate = 4 * ML_HEADS
    wg = p['w_gate'][:, perm].reshape(3, D_MODEL, n_gate)
    wg = jnp.pad(wg, ((0, 0), (0, 0), (0, LANES - n_gate))).astype(BF16)
    bg = jnp.pad(p['b_gate'][perm], (0, LANES - n_gate)).reshape(1, LANES)
    q, kt, v, xc, g1, g2, g3 = _mlstm_pre(
        rest, p['conv_w'], p['conv_b'].reshape(1, D_MODEL), _block_diag_tiles(p['wq']),
        _block_diag_tiles(p['wk']), _block_diag_tiles(p['wv']), wg, bg, n_seq, seq_len)

    def per_head(t, perm):
        t = t[:, :, :n_gate].reshape(n_seq, seq_len, ML_HEADS, 4)
        return jnp.transpose(t, perm)

    hn = _mlstm(q, kt, v, per_head(g1, (0, 2, 1, 3)), per_head(g2, (0, 2, 1, 3)),
                per_head(g3, (0, 2, 3, 1)), n_seq, seq_len)

    n_route = N_GROUPS + N_EXPERTS
    w_route = jnp.concatenate([p['rg_w'], p['re_w']], axis=1)
    w_route = jnp.pad(w_route, ((0, 0), (0, LANES - n_route)))
    w_route = jnp.stack(_split3(w_route)[:2])
    b_route = jnp.pad(jnp.concatenate([p['rg_b'], p['re_b']]), (0, LANES - n_route))
    vec = lambda a: a.reshape(1, D_MODEL)
    x1, route = _post(attn, hn, xc, rest, x, p['w_pa'].astype(BF16), p['w_pb'].astype(BF16),
                      p['w_out'].astype(BF16), vec(p['gn_g']), vec(p['skip']), vec(p['ln1_g']),
                      vec(p['ln1_b']), w_route, b_route.reshape(1, LANES))

    plan = _route_plan(route, n_tok)
    y = _moe(x1, *plan, p['we_g'].astype(BF16), p['we_u'].astype(BF16), p['we_d'].astype(BF16))
    return x1, y, vec(p['ln2_g']), vec(p['ln2_b'])


def kernel(x_prompt, x_sample, w_in, da_lambda_q1, da_lambda_k1, da_lambda_q2, da_lambda_k2, da_subln_g, ml_conv_w, ml_conv_b, ml_wq, ml_wk, ml_wv, ml_w_gate, ml_b_gate, ml_skip, ml_gn_g, w_pa, w_pb, w_out, ln1_g, ln1_b, router_group_w, router_group_b, router_expert_w, router_expert_b, w_e_gate, w_e_up, w_e_down, ln2_g, ln2_b):
    n_p, seq_len, d = x_prompt.shape
    n_s = x_sample.shape[0]
    assert x_sample.shape[1] == seq_len and d == D_MODEL and seq_len % CHUNK == 0
    n_seq = n_p + n_s
    x = jnp.concatenate([x_prompt, x_sample], axis=0).reshape(n_seq * seq_len, d)
    cos, sin = _rope_tables(seq_len)
    stacked = dict(w_in=w_in, lq1=da_lambda_q1, lk1=da_lambda_k1, lq2=da_lambda_q2,
                   lk2=da_lambda_k2, subln_g=da_subln_g, conv_w=ml_conv_w, conv_b=ml_conv_b,
                   wq=ml_wq, wk=ml_wk, wv=ml_wv, w_gate=ml_w_gate, b_gate=ml_b_gate,
                   skip=ml_skip, gn_g=ml_gn_g, w_pa=w_pa, w_pb=w_pb, w_out=w_out,
                   ln1_g=ln1_g, ln1_b=ln1_b, rg_w=router_group_w, rg_b=router_group_b,
                   re_w=router_expert_w, re_b=router_expert_b, we_g=w_e_gate, we_u=w_e_up,
                   we_d=w_e_down, ln2_g=ln2_g, ln2_b=ln2_b)
    src = x
    for l in range(w_in.shape[0]):
        lambda_init = 0.8 - 0.6 * math.exp(-0.3 * l)
        src = _layer(src, n_seq, seq_len, lambda_init, cos, sin,
                     {k: a[l] for k, a in stacked.items()})
    y = _combine(*src).reshape(n_seq, seq_len, d)
    return (y[:n_p], y[n_p:])
```

```python
import functools
import math

import jax
import jax.numpy as jnp
from jax import lax
from jax.experimental import pallas as pl
from jax.experimental.pallas import tpu as pltpu

F32 = jnp.float32
BF16 = jnp.bfloat16

D_MODEL = 1024
DEPTH = 4
DA_HEADS = 8
DA_HEAD_DIM = 64
DA_V_DIM = 128
ROPE_THETA = 10000.0
ML_HEADS = 4
ML_HEAD_DIM = 256
ML_PROJ_BLOCK = 4
N_GROUPS = 4
EXPERTS_PER_GROUP = 8
N_EXPERTS = 32
TOP_K = 2
D_EXPERT = 512
DN_ALPHA = (2 * DEPTH) ** 0.25
LN_EPS = 1e-5

LANES = 128
MXU_DIM = 256
ROW_CHUNKS = D_MODEL // LANES
CHUNK = 128
MOE_ROWS = 512
DMA_PRIORITIES = 2
VMEM_LIMIT = 56 * 1024 * 1024

NEG_BIG = -1e30


def _cparams(sem):
    return pltpu.CompilerParams(dimension_semantics=sem, vmem_limit_bytes=VMEM_LIMIT)


def _sigmoid(x):
    return 0.5 * jnp.tanh(0.5 * x) + 0.5


def _inproj_kernel(*refs, fused, n_rope, n_scaled, n_qkv, scale, cw):
    if fused:
        (x1t_ref, y0_ref, y1_ref, g_ref, b_ref, w_ref, cos_ref, sin_ref,
         x_out_ref, qkv_ref, rest_ref) = refs
        x = _combine_rows(x1t_ref, y0_ref, y1_ref, g_ref, b_ref, qkv_ref.shape[0])
        x_out_ref[...] = x
    else:
        x_ref, w_ref, cos_ref, sin_ref, qkv_ref, rest_ref = refs
        x = x_ref[...]
    xb = x.astype(BF16)
    tm = qkv_ref.shape[0]
    n_cols = w_ref.shape[1]
    cos = cos_ref[...]
    sin = sin_ref[...]
    lane = lax.broadcasted_iota(jnp.int32, (tm, LANES), 1)
    first_half = (lane % DA_HEAD_DIM) < (DA_HEAD_DIM // 2)
    for c in range(n_cols // cw):
        acc = jnp.dot(xb, w_ref[:, c * cw:(c + 1) * cw], preferred_element_type=F32)
        for s in range(cw // LANES):
            col = c * cw + s * LANES
            t = acc[:, s * LANES:(s + 1) * LANES]
            if col < n_rope:
                rot = jnp.where(first_half, pltpu.roll(t, LANES - DA_HEAD_DIM // 2, 1),
                                pltpu.roll(t, DA_HEAD_DIM // 2, 1))
                t = t * cos + rot * sin
                if col < n_scaled:
                    t = t * scale
            if col < n_qkv:
                qkv_ref[:, col:col + LANES] = t.astype(qkv_ref.dtype)
            else:
                rest_ref[:, col - n_qkv:col - n_qkv + LANES] = t.astype(rest_ref.dtype)


def _inproj(src, w, cos, sin, seq_len, *, n_rope, n_scaled, n_qkv, scale):
    fused = isinstance(src, tuple)
    d, n_cols = w.shape
    n_tok = src[0].shape[0] // ROW_CHUNKS if fused else src.shape[0]
    tm = min(512, seq_len)
    n_t = n_tok // tm
    per_seq = seq_len // tm
    kern = functools.partial(_inproj_kernel, fused=fused, n_rope=n_rope, n_scaled=n_scaled,
                             n_qkv=n_qkv, scale=scale, cw=512)
    table = pl.BlockSpec((tm, LANES), lambda i: (i % per_seq, 0))
    wspec = pl.BlockSpec((d, n_cols), lambda i: (0, 0), pipeline_mode=pl.Buffered(1))
    rows = lambda n: pl.BlockSpec((tm, n), lambda i: (i, 0))
    outs = [rows(n_qkv), rows(n_cols - n_qkv)]
    out_shapes = [jax.ShapeDtypeStruct((n_tok, n_qkv), BF16),
                  jax.ShapeDtypeStruct((n_tok, n_cols - n_qkv), BF16)]
    if fused:
        x1t, y, ln_g, ln_b = src
        tiles = lambda off: pl.BlockSpec((tm * ROW_CHUNKS, LANES), lambda i, off=off: (i + off, 0))
        vec = pl.BlockSpec((1, d), lambda i: (0, 0))
        x, qkv, rest = pl.pallas_call(
            kern, grid=(n_t,),
            in_specs=[tiles(0), tiles(0), tiles(n_t), vec, vec, wspec, table, table],
            out_specs=[rows(d)] + outs,
            out_shape=[jax.ShapeDtypeStruct((n_tok, d), F32)] + out_shapes,
            compiler_params=_cparams(("parallel",)),
        )(x1t, y, y, ln_g, ln_b, w, cos, sin)
        return x, qkv, rest
    qkv, rest = pl.pallas_call(
        kern, grid=(n_t,),
        in_specs=[rows(d), wspec, table, table],
        out_specs=outs, out_shape=out_shapes,
        compiler_params=_cparams(("parallel",)),
    )(src, w, cos, sin)
    return src, qkv, rest


def _attn_kernel(sc_ref, q_ref, k_ref, v_ref, g_ref, o_ref, vx_ref, *, bq):
    seq_len = q_ref.shape[0]
    lam = sc_ref[0]
    out_scale = sc_ref[1]
    k = k_ref[...]
    vx_ref[:, :DA_V_DIM] = v_ref[...]
    vx_ref[:, DA_V_DIM:] = jnp.ones((seq_len, DA_V_DIM), vx_ref.dtype)
    vx = vx_ref[...]
    g = g_ref[...] * out_scale
    lane = lax.broadcasted_iota(jnp.int32, (bq, LANES), 1)
    is_first = lane < DA_HEAD_DIM
    dn = (((1,), (1,)), ((), ()))

    def softmax_av(qm):
        s = lax.dot_general(qm, k, dn, preferred_element_type=F32)
        p = jnp.exp2(s - jnp.max(s, axis=-1, keepdims=True)).astype(BF16)
        ox = jnp.dot(p, vx, preferred_element_type=F32)
        return ox[:, :DA_V_DIM] / ox[:, DA_V_DIM:]

    for i in range(seq_len // bq):
        rows = slice(i * bq, (i + 1) * bq)
        qb = q_ref[rows, :]
        zero = jnp.zeros_like(qb)
        o = (softmax_av(jnp.where(is_first, qb, zero))
             - lam * softmax_av(jnp.where(is_first, zero, qb)))
        o = o * lax.rsqrt(jnp.mean(o * o, axis=-1, keepdims=True) + LN_EPS)
        o_ref[rows, :] = (o * g).astype(o_ref.dtype)


def _attention(qkv, scalars, subln_g, n_seq, seq_len):
    n_tok = qkv.shape[0]
    bq = min(128, seq_len)
    return pl.pallas_call(
        functools.partial(_attn_kernel, bq=bq),
        grid=(n_seq, DA_HEADS),
        in_specs=[
            pl.BlockSpec(memory_space=pltpu.SMEM),
            pl.BlockSpec((seq_len, LANES), lambda b, h: (b, h)),
            pl.BlockSpec((seq_len, LANES), lambda b, h: (b, DA_HEADS + h)),
            pl.BlockSpec((seq_len, LANES), lambda b, h: (b, 2 * DA_HEADS + h)),
            pl.BlockSpec((1, DA_V_DIM), lambda b, h: (0, 0)),
        ],
        out_specs=pl.BlockSpec((seq_len, DA_V_DIM), lambda b, h: (b, h)),
        out_shape=jax.ShapeDtypeStruct((n_tok, DA_HEADS * DA_V_DIM), BF16),
        scratch_shapes=[pltpu.VMEM((seq_len, 2 * DA_V_DIM), BF16)],
        compiler_params=_cparams(("parallel", "parallel")),
    )(scalars, qkv, qkv, qkv, subln_g)


def _split3(x):
    x1 = x.astype(BF16)
    r1 = x - x1.astype(F32)
    x2 = r1.astype(BF16)
    x3 = (r1 - x2.astype(F32)).astype(BF16)
    return x1, x2, x3


def _gate_tables(gacc_ref, bg_ref, g1_ref, g2_ref, g3_ref, tmp_ref, last_ref):
    seq_len = gacc_ref.shape[0]
    n_chunks = seq_len // CHUNK
    r = lax.broadcasted_iota(jnp.int32, (CHUNK, CHUNK), 0)
    c = lax.broadcasted_iota(jnp.int32, (CHUNK, CHUNK), 1)
    tri = jnp.where(c <= r, 1.0, 0.0).astype(BF16)
    lane = lax.broadcasted_iota(jnp.int32, (CHUNK, LANES), 1)
    row = lax.broadcasted_iota(jnp.int32, (CHUNK, LANES), 0)
    is_kind0 = (lane % 2) == 0
    is_bwd = ((lane // 2) % 2) == 1
    is_bwd_row = is_bwd[0:1, :]
    bg = bg_ref[...]

    def first_pass(ci, carry):
        rows = pl.ds(pl.multiple_of(ci * CHUNK, CHUNK), CHUNK)
        pre = gacc_ref[rows, :] + bg
        lf = jnp.minimum(pre, 0.0) - jnp.log1p(jnp.exp(-jnp.abs(pre)))
        l1, l2, l3 = _split3(lf)
        pref = (jnp.dot(tri, l1, preferred_element_type=F32)
                + jnp.dot(tri, l2, preferred_element_type=F32)
                + jnp.dot(tri, l3, preferred_element_type=F32))
        suff = pref[CHUNK - 1:CHUNK, :] - pref + lf
        cum = jnp.where(is_bwd, suff, pref)
        b0 = pltpu.roll(cum, LANES - 1, 1)
        g = pre - b0
        mx_f = g
        mx_b = g
        s = 1
        while s < CHUNK:
            mx_f = jnp.maximum(mx_f, jnp.where(row >= s, pltpu.roll(mx_f, s, 0), -jnp.inf))
            mx_b = jnp.maximum(mx_b, jnp.where(row < CHUNK - s,
                                               pltpu.roll(mx_b, CHUNK - s, 0), -jnp.inf))
            s *= 2
        mx = jnp.where(is_bwd, mx_b, mx_f)
        tmp_ref[0, rows, :] = g
        tmp_ref[1, rows, :] = mx
        tmp_ref[2, rows, :] = b0
        last_ref[0, pl.ds(ci, 1), :] = jnp.where(is_bwd_row, b0[0:1, :], b0[CHUNK - 1:CHUNK, :])
        last_ref[1, pl.ds(ci, 1), :] = jnp.where(is_bwd_row, mx[0:1, :], mx[CHUNK - 1:CHUNK, :])
        return carry

    lax.fori_loop(0, n_chunks, first_pass, 0)

    m_f = jnp.zeros((1, LANES), F32)
    m_b = jnp.zeros((1, LANES), F32)
    for t in range(n_chunks):
        cf, cb = t, n_chunks - 1 - t
        last_ref[2, cf:cf + 1, :] = m_f
        last_ref[3, cb:cb + 1, :] = m_b
        m_f = last_ref[0, cf:cf + 1, :] + jnp.maximum(m_f, last_ref[1, cf:cf + 1, :])
        m_b = last_ref[0, cb:cb + 1, :] + jnp.maximum(m_b, last_ref[1, cb:cb + 1, :])

    def second_pass(ci, carry):
        rows = pl.ds(pl.multiple_of(ci * CHUNK, CHUNK), CHUNK)
        g = tmp_ref[0, rows, :]
        mx = tmp_ref[1, rows, :]
        b0 = tmp_ref[2, rows, :]
        m_st = jnp.where(is_bwd_row, last_ref[3, pl.ds(ci, 1), :], last_ref[2, pl.ds(ci, 1), :])
        mm = jnp.maximum(mx, m_st)
        m_up = jnp.maximum(m_st, last_ref[1, pl.ds(ci, 1), :])
        keep = jnp.broadcast_to(jnp.exp(m_st - m_up), (CHUNK, LANES))
        g1_ref[rows, :] = jnp.where(is_kind0, -mm, pltpu.roll(jnp.exp(-mm - b0), 1, 1))
        g2_ref[rows, :] = jnp.where(is_kind0, jnp.exp(m_st - mm), pltpu.roll(keep, 1, 1))
        g3_ref[rows, :] = jnp.where(is_kind0, g, pltpu.roll(jnp.exp(g - m_up), 1, 1))
        return carry

    lax.fori_loop(0, n_chunks, second_pass, 0)


def _mlpre_kernel(xm_ref, cw_ref, cb_ref, wq_ref, wkt_ref, wk_ref, wv_ref, wg_ref, bg_ref,
                  q_ref, kt_ref, v_ref, xc_ref, g1_ref, g2_ref, g3_ref,
                  gacc_ref, tmp_ref, last_ref):
    j = pl.program_id(1)
    seq_len = xm_ref.shape[0]
    xm = xm_ref[...].astype(F32)
    row = lax.broadcasted_iota(jnp.int32, xm.shape, 0)
    prev = jnp.where(row == 0, 0.0, pltpu.roll(xm, 1, 0))
    nxt = jnp.where(row == seq_len - 1, 0.0, pltpu.roll(xm, seq_len - 1, 0))
    xc = cb_ref[...] + prev * cw_ref[0:1, :] + xm * cw_ref[1:2, :] + nxt * cw_ref[2:3, :]
    xc = xc * _sigmoid(xc)
    xcb = xc.astype(BF16)
    xc_ref[...] = xcb
    q = jnp.dot(xcb, wq_ref[...], preferred_element_type=F32)
    k = jnp.dot(xcb, wk_ref[...], preferred_element_type=F32)
    v = jnp.dot(xm_ref[...], wv_ref[...], preferred_element_type=F32)
    qb = q.astype(BF16)
    kb = k.astype(BF16)
    vb = v.astype(BF16)
    q_ref[...] = (q * (ML_HEAD_DIM ** -0.5)).astype(BF16)
    kt_ref[...] = lax.dot_general(wkt_ref[...], xcb, (((1,), (1,)), ((), ())),
                                  preferred_element_type=F32).astype(BF16)
    v_ref[...] = vb
    part = (jnp.dot(qb, wg_ref[0], preferred_element_type=F32)
            + jnp.dot(kb, wg_ref[1], preferred_element_type=F32)
            + jnp.dot(vb, wg_ref[2], preferred_element_type=F32))

    @pl.when(j == 0)
    def _():
        gacc_ref[...] = part

    @pl.when(j > 0)
    def _():
        gacc_ref[...] += part

    @pl.when(j == pl.num_programs(1) - 1)
    def _():
        _gate_tables(gacc_ref, bg_ref, g1_ref, g2_ref, g3_ref, tmp_ref, last_ref)


def _mlstm_pre(rest, conv_w, conv_b, wq_bd, wk_bd, wv_bd, wg, bg, n_seq, seq_len):
    n_tok = rest.shape[0]
    n_ct = D_MODEL // MXU_DIM
    tile = pl.BlockSpec((seq_len, MXU_DIM), lambda b, j: (b, j))
    wspec = pl.BlockSpec((None, MXU_DIM, MXU_DIM), lambda b, j: (j, 0, 0))
    gspec = pl.BlockSpec((None, seq_len, LANES), lambda b, j: (b, 0, 0))
    act = jax.ShapeDtypeStruct((n_tok, D_MODEL), BF16)
    gate = jax.ShapeDtypeStruct((n_seq, seq_len, LANES), F32)
    return pl.pallas_call(
        _mlpre_kernel,
        grid=(n_seq, n_ct),
        in_specs=[
            tile,
            pl.BlockSpec((3, MXU_DIM), lambda b, j: (0, j)),
            pl.BlockSpec((1, MXU_DIM), lambda b, j: (0, j)),
            wspec, wspec, wspec, wspec,
            pl.BlockSpec((3, MXU_DIM, LANES), lambda b, j: (0, j, 0)),
            pl.BlockSpec((1, LANES), lambda b, j: (0, 0)),
        ],
        out_specs=[tile, pl.BlockSpec((MXU_DIM, seq_len), lambda b, j: (j, b)), tile, tile,
                   gspec, gspec, gspec],
        out_shape=[act, jax.ShapeDtypeStruct((D_MODEL, n_tok), BF16), act, act,
                   gate, gate, gate],
        scratch_shapes=[pltpu.VMEM((seq_len, LANES), F32),
                        pltpu.VMEM((3, seq_len, LANES), F32),
                        pltpu.VMEM((4, seq_len // CHUNK, LANES), F32)],
        compiler_params=_cparams(("parallel", "arbitrary")),
    )(rest, conv_w, conv_b, wq_bd, jnp.swapaxes(wk_bd, 1, 2), wk_bd, wv_bd, wg, bg)


def _mlstm_kernel(q_ref, kt_ref, v_ref, gc1_ref, gc2_ref, gr_ref, o_ref,
                  vx_ref, qk_ref, h_ref):
    seq_len = q_ref.shape[0]
    n_chunks = seq_len // CHUNK
    dh = ML_HEAD_DIM
    r = lax.broadcasted_iota(jnp.int32, (CHUNK, CHUNK), 0)
    c = lax.broadcasted_iota(jnp.int32, (CHUNK, CHUNK), 1)
    vx_ref[:, :dh] = v_ref[...]
    vx_ref[:, dh:] = jnp.ones((seq_len, LANES), vx_ref.dtype)

    def lane_replicated(ref, rows, col):
        return jnp.broadcast_to(ref[rows, col:col + 1], (CHUNK, LANES))

    def chunk_step(direction, ci, c_st):
        mask = (c <= r) if direction == 0 else (c >= r)
        rows = slice(ci * CHUNK, (ci + 1) * CHUNK)
        qc = q_ref[rows, :]
        ktc = kt_ref[:, rows]
        vxc = vx_ref[rows, :]
        neg_mm = lane_replicated(gc1_ref, rows, 2 * direction)
        e_mj = lane_replicated(gc1_ref, rows, 2 * direction + 1)
        inter = lane_replicated(gc2_ref, rows, 2 * direction)
        g_row = gr_ref[2 * direction:2 * direction + 1, rows]
        w_row = gr_ref[2 * direction + 1:2 * direction + 2, rows]
        keep = gc2_ref[ci * CHUNK:ci * CHUNK + 1, 2 * direction + 1:2 * direction + 2]
        if ci in first_visit:
            qk = qk_ref[ci]
        else:
            qk = jnp.dot(qc, ktc, preferred_element_type=F32)
            qk_ref[ci] = qk
        sw = qk * jnp.exp(jnp.where(mask, neg_mm + g_row, -jnp.inf))
        intra = jnp.dot(sw.astype(BF16), vxc, preferred_element_type=F32)
        carry_in = jnp.dot(qc, c_st.astype(BF16), preferred_element_type=F32)
        den = intra[:, dh:] + inter * carry_in[:, dh:]
        rdiv = 1.0 / jnp.maximum(jnp.abs(den), e_mj)
        h = jnp.concatenate(
            [(intra[:, s * LANES:(s + 1) * LANES] + inter * carry_in[:, s * LANES:(s + 1) * LANES])
             * rdiv for s in range(dh // LANES)], axis=1)
        if ci in first_visit:
            h = h_ref[rows, :] + h
            mu = jnp.mean(h, axis=-1, keepdims=True)
            hc = h - mu
            var = jnp.mean(hc * hc, axis=-1, keepdims=True)
            o_ref[rows, :] = (hc * lax.rsqrt(var + LN_EPS)).astype(o_ref.dtype)
        else:
            h_ref[rows, :] = h
            first_visit.add(ci)
        kw = (ktc.astype(F32) * w_row).astype(BF16)
        return keep * c_st + jnp.dot(kw, vxc, preferred_element_type=F32)

    first_visit = set()
    c_f = jnp.zeros((dh, dh + LANES), F32)
    c_b = jnp.zeros((dh, dh + LANES), F32)
    for step in range(n_chunks):
        c_f = chunk_step(0, step, c_f)
        c_b = chunk_step(1, n_chunks - 1 - step, c_b)


def _mlstm(q, kt, v, gc1, gc2, gr3, n_seq, seq_len):
    n_tok = q.shape[0]
    tile = pl.BlockSpec((seq_len, ML_HEAD_DIM), lambda b, h: (b, h))
    col = pl.BlockSpec((None, None, seq_len, 4), lambda b, h: (b, h, 0, 0))
    return pl.pallas_call(
        _mlstm_kernel,
        grid=(n_seq, ML_HEADS),
        in_specs=[
            tile,
            pl.BlockSpec((ML_HEAD_DIM, seq_len), lambda b, h: (h, b)),
            tile, col, col,
            pl.BlockSpec((None, None, 4, seq_len), lambda b, h: (b, h, 0, 0)),
        ],
        out_specs=tile,
        out_shape=jax.ShapeDtypeStruct((n_tok, D_MODEL), BF16),
        scratch_shapes=[
            pltpu.VMEM((seq_len, ML_HEAD_DIM + LANES), BF16),
            pltpu.VMEM((seq_len // CHUNK, CHUNK, CHUNK), F32),
            pltpu.VMEM((seq_len, ML_HEAD_DIM), F32),
        ],
        compiler_params=_cparams(("parallel", "parallel")),
    )(q, kt, v, gc1, gc2, gr3)


def _layer_norm(y, g, b):
    mu = jnp.mean(y, axis=-1, keepdims=True)
    yc = y - mu
    var = jnp.mean(yc * yc, axis=-1, keepdims=True)
    return yc * lax.rsqrt(var + LN_EPS) * g + b


def _route(logits):
    lane = lax.broadcasted_iota(jnp.int32, logits.shape, 1)
    big = jnp.int32(4 * LANES)
    gl = jnp.where(lane < N_GROUPS, logits, NEG_BIG)
    gmax = jnp.max(gl, axis=-1, keepdims=True)
    gsum = jnp.sum(jnp.where(lane < N_GROUPS, jnp.exp(gl - gmax), 0.0), axis=-1, keepdims=True)
    g_sel = jnp.min(jnp.where(gl == gmax, lane, big), axis=-1, keepdims=True)
    g_prob = 1.0 / gsum
    lo = N_GROUPS + EXPERTS_PER_GROUP * g_sel
    in_group = jnp.logical_and(lane >= lo, lane < lo + EXPERTS_PER_GROUP)
    el = jnp.where(in_group, logits, NEG_BIG)
    emax = jnp.max(el, axis=-1, keepdims=True)
    esum = jnp.sum(jnp.where(in_group, jnp.exp(el - emax), 0.0), axis=-1, keepdims=True)
    i1 = jnp.min(jnp.where(el == emax, lane, big), axis=-1, keepdims=True)
    el2 = jnp.where(lane == i1, NEG_BIG, el)
    emax2 = jnp.max(el2, axis=-1, keepdims=True)
    i2 = jnp.min(jnp.where(el2 == emax2, lane, big), axis=-1, keepdims=True)
    p1 = 1.0 / esum
    p2 = jnp.exp(emax2 - emax) / esum
    psum = p1 + p2
    gate1 = g_prob * p1 / psum
    gate2 = g_prob * p2 / psum
    e1 = (i1 - N_GROUPS).astype(F32)
    e2 = (i2 - N_GROUPS).astype(F32)
    return jnp.where(lane == 0, e1, jnp.where(lane == 1, e2, jnp.where(lane == 2, gate1, gate2)))


def _post_kernel(attn_ref, hn_ref, xc_ref, z_ref, ga_ref, gb_ref, x_ref,
                 wpa_ref, wpb_ref, wout_ref, gn_ref, skip_ref, lg_ref, lb_ref, wr_ref, br_ref,
                 x1t_ref, route_ref, *, sub):
    tm = x_ref.shape[0]
    for s in range(tm // sub):
        rows = slice(s * sub, (s + 1) * sub)
        z = z_ref[rows, :].astype(F32)
        ml = ((hn_ref[rows, :].astype(F32) * gn_ref[...]
               + skip_ref[...] * xc_ref[rows, :].astype(F32)) * (z * _sigmoid(z)))
        a_out = jnp.dot(attn_ref[rows, :], wpa_ref[...], preferred_element_type=F32)
        m_out = jnp.dot(ml.astype(BF16), wpb_ref[...], preferred_element_type=F32)
        mixed = (_sigmoid(ga_ref[rows, :].astype(F32)) * a_out
                 + _sigmoid(gb_ref[rows, :].astype(F32)) * m_out)
        y = DN_ALPHA * x_ref[rows, :] + jnp.dot(mixed.astype(BF16), wout_ref[...],
                                               preferred_element_type=F32)
        x1 = _layer_norm(y, lg_ref[...], lb_ref[...])
        for c in range(ROW_CHUNKS):
            x1t_ref[pl.ds(s * sub * ROW_CHUNKS + c, sub, stride=ROW_CHUNKS), :] = (
                x1[:, c * LANES:(c + 1) * LANES])
        logits = br_ref[...] + jnp.dot(x1.astype(BF16), wr_ref[...], preferred_element_type=F32)
        route_ref[rows, :] = _route(logits)


def _post(attn, hn, xc, rest, x, w_pa, w_pb, w_out, gn_g, skip, ln_g, ln_b, w_route, b_route):
    n_tok = x.shape[0]
    sub = 256
    tm = 2 * sub if n_tok % (2 * sub) == 0 else sub
    row = lambda col: pl.BlockSpec((tm, D_MODEL), lambda i, col=col: (i, col))
    full = lambda shape: pl.BlockSpec(shape, lambda i: tuple(0 for _ in shape))
    vec = full((1, D_MODEL))
    return pl.pallas_call(
        functools.partial(_post_kernel, sub=sub),
        grid=(n_tok // tm,),
        in_specs=[row(0), row(0), row(0), row(1), row(2), row(3), row(0),
                  full((D_MODEL, D_MODEL)), full((D_MODEL, D_MODEL)), full((D_MODEL, D_MODEL)),
                  vec, vec, vec, vec,
                  full((D_MODEL, LANES)), full((1, LANES))],
        out_specs=[pl.BlockSpec((tm * ROW_CHUNKS, LANES), lambda i: (i, 0)),
                   pl.BlockSpec((tm, LANES), lambda i: (i, 0))],
        out_shape=[jax.ShapeDtypeStruct((n_tok * ROW_CHUNKS, LANES), F32),
                   jax.ShapeDtypeStruct((n_tok, LANES), F32)],
        compiler_params=_cparams(("parallel",)),
    )(attn, hn, xc, rest, rest, rest, x, w_pa, w_pb, w_out, gn_g, skip, ln_g, ln_b,
      w_route, b_route)


def _moe_kernel(bexp_ref, nused_ref,
                tok_hbm, dst_hbm, gate_ref, x_hbm, wg_ref, wu_ref, wd_ref, y_hbm,
                tok_smem, dst_smem, xbuf, obuf, idx_sem, gat_sem, sca_sem):
    i = pl.program_id(0)
    n_used = nused_ref[0]
    p = i % 2

    def index_copies(blk):
        slot = blk % 3
        return (pltpu.make_async_copy(tok_hbm.at[blk], tok_smem.at[slot], idx_sem.at[0, slot]),
                pltpu.make_async_copy(dst_hbm.at[blk], dst_smem.at[slot], idx_sem.at[1, slot]))

    def start_indices(blk):
        for cp in index_copies(blk):
            cp.start()

    def wait_indices(blk):
        for cp in index_copies(blk):
            cp.wait()

    def tile_rows(r):
        return pl.ds(r * ROW_CHUNKS, ROW_CHUNKS)

    def start_gather(blk, slot, rows=range(MOE_ROWS)):
        islot = blk % 3

        for row in rows:
            t = tok_smem[islot, row // LANES, row % LANES]
            pltpu.make_async_copy(x_hbm.at[t], xbuf.at[slot, tile_rows(row)],
                                  gat_sem.at[slot]).start(priority=row % DMA_PRIORITIES)

    def wait_gather(slot):
        pltpu.make_async_copy(xbuf.at[slot], xbuf.at[slot], gat_sem.at[slot]).wait()

    def start_scatter(blk, slot, rows):
        islot = blk % 3

        for row in rows:
            t = dst_smem[islot, row // LANES, row % LANES]
            pltpu.make_async_copy(obuf.at[slot, tile_rows(row)], y_hbm.at[t],
                                  sca_sem.at[slot]).start(priority=row % DMA_PRIORITIES)

    def wait_scatter(slot):
        pltpu.make_async_copy(obuf.at[slot], obuf.at[slot], sca_sem.at[slot]).wait()

    @pl.when(i == 0)
    def _():
        start_indices(0)
        wait_indices(0)
        start_gather(0, 0)

        @pl.when(1 < n_used)
        def _():
            start_indices(1)

    @pl.when(i + 1 < n_used)
    def _():
        wait_indices(i + 1)

    @pl.when(i + 2 < n_used)
    def _():
        start_indices(i + 2)

    @pl.when(i < n_used)
    def _():
        wait_gather(p)

        @pl.when(i >= 2)
        def _():
            wait_scatter(p)

        nxt = jnp.minimum(i + 1, n_used - 1)
        half = MOE_ROWS // 2
        for h in range(2):
            rows = range(h * half, (h + 1) * half)
            start_gather(nxt, 1 - p, rows)
            xb = jnp.concatenate(
                [xbuf[p, pl.ds(h * half * ROW_CHUNKS + c, half, stride=ROW_CHUNKS), :]
                 for c in range(ROW_CHUNKS)], axis=1).astype(BF16)
            hg = jnp.dot(xb, wg_ref[...], preferred_element_type=F32)
            hu = jnp.dot(xb, wu_ref[...], preferred_element_type=F32)
            hh = (hg * _sigmoid(hg) * hu).astype(BF16)
            out = (jnp.dot(hh, wd_ref[...], preferred_element_type=F32)
                   * gate_ref[h * half:(h + 1) * half, :])
            for c in range(ROW_CHUNKS):
                obuf[p, pl.ds(h * half * ROW_CHUNKS + c, half, stride=ROW_CHUNKS), :] = (
                    out[:, c * LANES:(c + 1) * LANES])
            start_scatter(i, p, rows)

        @pl.when(i == n_used - 1)
        def _():
            wait_scatter(p)
            wait_gather(1 - p)

            @pl.when(i >= 1)
            def _():
                wait_scatter(1 - p)


def _moe(x1t, slot_tok, slot_dst, slot_gate, blk_exp, n_used, we_g, we_u, we_d):
    n_tok = x1t.shape[0] // ROW_CHUNKS
    n_blk = blk_exp.shape[0]
    n_out = TOP_K * n_tok + 2 * MOE_ROWS
    wspec_in = pl.BlockSpec((None, D_MODEL, D_EXPERT), lambda i, be, nu: (be[i], 0, 0))
    wspec_out = pl.BlockSpec((None, D_EXPERT, D_MODEL), lambda i, be, nu: (be[i], 0, 0))
    grid_spec = pltpu.PrefetchScalarGridSpec(
        num_scalar_prefetch=2,
        grid=(n_blk,),
        in_specs=[
            pl.BlockSpec(memory_space=pl.ANY),
            pl.BlockSpec(memory_space=pl.ANY),
            pl.BlockSpec((MOE_ROWS, 1), lambda i, be, nu: (i, 0)),
            pl.BlockSpec(memory_space=pl.ANY),
            wspec_in, wspec_in, wspec_out,
        ],
        out_specs=pl.BlockSpec(memory_space=pl.ANY),
        scratch_shapes=[
            pltpu.SMEM((3, MOE_ROWS // LANES, LANES), jnp.int32),
            pltpu.SMEM((3, MOE_ROWS // LANES, LANES), jnp.int32),
            pltpu.VMEM((2, MOE_ROWS * ROW_CHUNKS, LANES), F32),
            pltpu.VMEM((2, MOE_ROWS * ROW_CHUNKS, LANES), F32),
            pltpu.SemaphoreType.DMA((2, 3)),
            pltpu.SemaphoreType.DMA((2,)),
            pltpu.SemaphoreType.DMA((2,)),
        ],
    )
    y = pl.pallas_call(
        _moe_kernel,
        grid_spec=grid_spec,
        out_shape=jax.ShapeDtypeStruct((n_out, ROW_CHUNKS, LANES), F32),
        compiler_params=_cparams(("arbitrary",)),
    )(blk_exp, n_used, slot_tok.reshape(n_blk, MOE_ROWS // LANES, LANES),
      slot_dst.reshape(n_blk, MOE_ROWS // LANES, LANES), slot_gate.reshape(n_blk * MOE_ROWS, 1),
      x1t.reshape(n_tok, ROW_CHUNKS, LANES), we_g, we_u, we_d)
    return y.reshape(n_out * ROW_CHUNKS, LANES)


def _route_plan(route, n_tok):
    n_assign = n_tok * TOP_K
    eid = route[:, 0:TOP_K].astype(jnp.int32).reshape(n_assign)
    gate = route[:, TOP_K:2 * TOP_K].reshape(n_assign)
    order = jnp.argsort(eid).astype(jnp.int32)
    counts = jnp.zeros((N_EXPERTS,), jnp.int32).at[eid].add(1)
    offsets = jnp.cumsum(counts) - counts
    padded = (counts + MOE_ROWS - 1) // MOE_ROWS * MOE_ROWS
    pad_end = jnp.cumsum(padded)
    pad_start = pad_end - padded
    n_blk = (n_assign + MOE_ROWS - 1) // MOE_ROWS + N_EXPERTS
    blk_start = jnp.arange(n_blk, dtype=jnp.int32) * MOE_ROWS
    blk_exp = jnp.minimum(jnp.sum(pad_end[None, :] <= blk_start[:, None], axis=1),
                          N_EXPERTS - 1).astype(jnp.int32)
    row = jnp.arange(MOE_ROWS, dtype=jnp.int32)[None, :]
    j = (blk_start - pad_start[blk_exp])[:, None] + row
    valid = j < counts[blk_exp][:, None]
    src = jnp.clip(offsets[blk_exp][:, None] + j, 0, n_assign - 1)
    assign = order[src]
    tok = assign // TOP_K
    dummy = TOP_K * n_tok + (jnp.arange(n_blk, dtype=jnp.int32) % 2)[:, None] * MOE_ROWS + row
    slot_tok = jnp.where(valid, tok, 0)
    slot_dst = jnp.where(valid, (assign % TOP_K) * n_tok + tok, dummy)
    slot_gate = jnp.where(valid, gate[assign], 0.0)
    n_used = (pad_end[-1:] // MOE_ROWS).astype(jnp.int32)
    return slot_tok, slot_dst, slot_gate, blk_exp, n_used


def _combine_rows(x_ref, y0_ref, y1_ref, g_ref, b_ref, tm):
    def chunk(ref, c):
        return ref[pl.ds(c, tm, stride=ROW_CHUNKS), :]

    y = jnp.concatenate(
        [DN_ALPHA * chunk(x_ref, c) + (chunk(y0_ref, c) + chunk(y1_ref, c))
         for c in range(ROW_CHUNKS)], axis=1)
    return _layer_norm(y, g_ref[...], b_ref[...])


def _combine_kernel(x_ref, y0_ref, y1_ref, g_ref, b_ref, o_ref):
    o_ref[...] = _combine_rows(x_ref, y0_ref, y1_ref, g_ref, b_ref, o_ref.shape[0])


def _combine(x1t, y, ln_g, ln_b):
    n_tok = x1t.shape[0] // ROW_CHUNKS
    tm = 256
    n_t = n_tok // tm
    vec = pl.BlockSpec((1, D_MODEL), lambda i: (0, 0))
    tiles = lambda off: pl.BlockSpec((tm * ROW_CHUNKS, LANES), lambda i, off=off: (i + off, 0))
    return pl.pallas_call(
        _combine_kernel,
        grid=(n_t,),
        in_specs=[tiles(0), tiles(0), tiles(n_t), vec, vec],
        out_specs=pl.BlockSpec((tm, D_MODEL), lambda i: (i, 0)),
        out_shape=jax.ShapeDtypeStruct((n_tok, D_MODEL), F32),
        compiler_params=_cparams(("parallel",)),
    )(x1t, y, y, ln_g, ln_b)


def _rope_tables(seq_len):
    half = DA_HEAD_DIM // 2
    inv = ROPE_THETA ** (-jnp.arange(half, dtype=F32) * (2.0 / DA_HEAD_DIM))
    ang = jnp.arange(seq_len, dtype=F32)[:, None] * inv[None, :]
    cos = jnp.tile(jnp.cos(ang), (1, LANES // half))
    sin = jnp.tile(jnp.sin(ang), (1, LANES // half))
    lane = jnp.arange(LANES)
    sign = jnp.where((lane % DA_HEAD_DIM) < half, -1.0, 1.0).astype(F32)
    return cos, sin * sign[None, :]


def _block_diag_tiles(w):
    per_tile = MXU_DIM // ML_PROJ_BLOCK
    n_tiles = w.shape[0] // per_tile
    w4 = w.reshape(n_tiles, per_tile, ML_PROJ_BLOCK, ML_PROJ_BLOCK)
    eye = jnp.eye(per_tile, dtype=w.dtype)
    bd = jnp.einsum('jgio,gh->jgiho', w4, eye)
    return bd.reshape(n_tiles, MXU_DIM, MXU_DIM).astype(BF16)


def _gate_perm():
    idx = []
    for h in range(ML_HEADS):
        for d in range(2):
            for kind in range(2):
                idx.append(d * 2 * ML_HEADS + kind * ML_HEADS + h)
    return jnp.array(idx, dtype=jnp.int32)


def _layer(src, n_seq, seq_len, lambda_init, cos, sin, p):
    qk_w = DA_HEADS * 2 * DA_HEAD_DIM
    v_w = DA_HEADS * DA_V_DIM
    x, qkv, rest = _inproj(src, p['w_in'].astype(BF16), cos, sin, seq_len, n_rope=2 * qk_w,
                           n_scaled=qk_w, n_qkv=2 * qk_w + v_w,
                           scale=DA_HEAD_DIM ** -0.5 * math.log2(math.e))
    n_tok = x.shape[0]

    lam = (jnp.exp(jnp.sum(p['lq1'] * p['lk1'])) - jnp.exp(jnp.sum(p['lq2'] * p['lk2']))
           + lambda_init)
    scalars = jnp.stack([lam, jnp.asarray(1.0 - lambda_init, F32)]).astype(F32)
    attn = _attention(qkv, scalars, p['subln_g'].reshape(1, DA_V_DIM), n_seq, seq_len)

    perm = _gate_perm()
    n_gate = 4 * ML_HEADS
    wg = p['w_gate'][:, perm].reshape(3, D_MODEL, n_gate)
    wg = jnp.pad(wg, ((0, 0), (0, 0), (0, LANES - n_gate))).astype(BF16)
    bg = jnp.pad(p['b_gate'][perm], (0, LANES - n_gate)).reshape(1, LANES)
    q, kt, v, xc, g1, g2, g3 = _mlstm_pre(
        rest, p['conv_w'], p['conv_b'].reshape(1, D_MODEL), _block_diag_tiles(p['wq']),
        _block_diag_tiles(p['wk']), _block_diag_tiles(p['wv']), wg, bg, n_seq, seq_len)

    def per_head(t, perm):
        t = t[:, :, :n_gate].reshape(n_seq, seq_len, ML_HEADS, 4)
        return jnp.transpose(t, perm)

    hn = _mlstm(q, kt, v, per_head(g1, (0, 2, 1, 3)), per_head(g2, (0, 2, 1, 3)),
                per_head(g3, (0, 2, 3, 1)), n_seq, seq_len)

    n_route = N_GROUPS + N_EXPERTS
    w_route = jnp.concatenate([p['rg_w'], p['re_w']], axis=1)
    w_route = jnp.pad(w_route, ((0, 0), (0, LANES - n_route)))
    w_route = w_route.astype(BF16)
    b_route = jnp.pad(jnp.concatenate([p['rg_b'], p['re_b']]), (0, LANES - n_route))
    vec = lambda a: a.reshape(1, D_MODEL)
    x1, route = _post(attn, hn, xc, rest, x, p['w_pa'].astype(BF16), p['w_pb'].astype(BF16),
                      p['w_out'].astype(BF16), vec(p['gn_g']), vec(p['skip']), vec(p['ln1_g']),
                      vec(p['ln1_b']), w_route, b_route.reshape(1, LANES))

    plan = _route_plan(route, n_tok)
    y = _moe(x1, *plan, p['we_g'].astype(BF16), p['we_u'].astype(BF16), p['we_d'].astype(BF16))
    return x1, y, vec(p['ln2_g']), vec(p['ln2_b'])


def kernel(x_prompt, x_sample, w_in, da_lambda_q1, da_lambda_k1, da_lambda_q2, da_lambda_k2, da_subln_g, ml_conv_w, ml_conv_b, ml_wq, ml_wk, ml_wv, ml_w_gate, ml_b_gate, ml_skip, ml_gn_g, w_pa, w_pb, w_out, ln1_g, ln1_b, router_group_w, router_group_b, router_expert_w, router_expert_b, w_e_gate, w_e_up, w_e_down, ln2_g, ln2_b):
    n_p, seq_len, d = x_prompt.shape
    n_s = x_sample.shape[0]
    assert x_sample.shape[1] == seq_len and d == D_MODEL and seq_len % CHUNK == 0
    n_seq = n_p + n_s
    x = jnp.concatenate([x_prompt, x_sample], axis=0).reshape(n_seq * seq_len, d)
    cos, sin = _rope_tables(seq_len)
    stacked = dict(w_in=w_in, lq1=da_lambda_q1, lk1=da_lambda_k1, lq2=da_lambda_q2,
                   lk2=da_lambda_k2, subln_g=da_subln_g, conv_w=ml_conv_w, conv_b=ml_conv_b,
                   wq=ml_wq, wk=ml_wk, wv=ml_wv, w_gate=ml_w_gate, b_gate=ml_b_gate,
                   skip=ml_skip, gn_g=ml_gn_g, w_pa=w_pa, w_pb=w_pb, w_out=w_out,
                   ln1_g=ln1_g, ln1_b=ln1_b, rg_w=router_group_w, rg_b=router_group_b,
                   re_w=router_expert_w, re_b=router_expert_b, we_g=w_e_gate, we_u=w_e_up,
                   we_d=w_e_down, ln2_g=ln2_g, ln2_b=ln2_b)
    src = x
    for l in range(w_in.shape[0]):
        lambda_init = 0.8 - 0.6 * math.exp(-0.3 * l)
        src = _layer(src, n_seq, seq_len, lambda_init, cos, sin,
                     {k: a[l] for k, a in stacked.items()})
    y = _combine(*src).reshape(n_seq, seq_len, d)
    return (y[:n_p], y[n_p:])
```

```python
import functools
import math

import jax
import jax.numpy as jnp
from jax import lax
from jax.experimental import pallas as pl
from jax.experimental.pallas import tpu as pltpu

F32 = jnp.float32
BF16 = jnp.bfloat16

D_MODEL = 1024
DEPTH = 4
DA_HEADS = 8
DA_HEAD_DIM = 64
DA_V_DIM = 128
ROPE_THETA = 10000.0
ML_HEADS = 4
ML_HEAD_DIM = 256
ML_PROJ_BLOCK = 4
N_GROUPS = 4
EXPERTS_PER_GROUP = 8
N_EXPERTS = 32
TOP_K = 2
D_EXPERT = 512
DN_ALPHA = (2 * DEPTH) ** 0.25
LN_EPS = 1e-5

LANES = 128
MXU_DIM = 256
ROW_CHUNKS = D_MODEL // LANES
CHUNK = 128
MOE_ROWS = 512
DMA_PRIORITIES = 2
VMEM_LIMIT = 56 * 1024 * 1024

NEG_BIG = -1e30


def _cparams(sem):
    return pltpu.CompilerParams(dimension_semantics=sem, vmem_limit_bytes=VMEM_LIMIT)


def _sigmoid(x):
    return 0.5 * jnp.tanh(0.5 * x) + 0.5


def _inproj_kernel(*refs, fused, n_first, n_rope, n_scaled, n_qkv, scale, cw):
    if fused:
        (x1t_ref, y0_ref, y1_ref, g_ref, b_ref, w_ref, cos_ref, sin_ref,
         x_out_ref, qkv_ref, rest_ref) = refs
        x = _combine_rows(x1t_ref, y0_ref, y1_ref, g_ref, b_ref, qkv_ref.shape[0])
        x_out_ref[...] = x
    else:
        xa_ref, xb_ref, w_ref, cos_ref, sin_ref, x_out_ref, qkv_ref, rest_ref = refs
        x = jnp.where(pl.program_id(0) < n_first, xa_ref[...], xb_ref[...])
        x_out_ref[...] = x
    xb = x.astype(BF16)
    tm = qkv_ref.shape[0]
    n_cols = w_ref.shape[1]
    cos = cos_ref[...]
    sin = sin_ref[...]
    lane = lax.broadcasted_iota(jnp.int32, (tm, LANES), 1)
    first_half = (lane % DA_HEAD_DIM) < (DA_HEAD_DIM // 2)
    for c in range(n_cols // cw):
        acc = jnp.dot(xb, w_ref[:, c * cw:(c + 1) * cw], preferred_element_type=F32)
        for s in range(cw // LANES):
            col = c * cw + s * LANES
            t = acc[:, s * LANES:(s + 1) * LANES]
            if col < n_rope:
                rot = jnp.where(first_half, pltpu.roll(t, LANES - DA_HEAD_DIM // 2, 1),
                                pltpu.roll(t, DA_HEAD_DIM // 2, 1))
                t = t * cos + rot * sin
                if col < n_scaled:
                    t = t * scale
            if col < n_qkv:
                qkv_ref[:, col:col + LANES] = t.astype(qkv_ref.dtype)
            else:
                rest_ref[:, col - n_qkv:col - n_qkv + LANES] = t.astype(rest_ref.dtype)


def _inproj(src, w, cos, sin, seq_len, *, n_rope, n_scaled, n_qkv, scale):
    fused = isinstance(src, tuple)
    d, n_cols = w.shape
    tm = min(512, seq_len)
    n_first = 0 if fused else src[0].shape[0] // tm
    n_tok = src[0].shape[0] // ROW_CHUNKS if fused else src[0].shape[0] + src[1].shape[0]
    n_t = n_tok // tm
    per_seq = seq_len // tm
    kern = functools.partial(_inproj_kernel, fused=fused, n_first=n_first, n_rope=n_rope,
                             n_scaled=n_scaled, n_qkv=n_qkv, scale=scale, cw=512)
    table = pl.BlockSpec((tm, LANES), lambda i: (i % per_seq, 0))
    wspec = pl.BlockSpec((d, n_cols), lambda i: (0, 0), pipeline_mode=pl.Buffered(1))
    rows = lambda n: pl.BlockSpec((tm, n), lambda i: (i, 0))
    outs = [rows(n_qkv), rows(n_cols - n_qkv)]
    out_shapes = [jax.ShapeDtypeStruct((n_tok, n_qkv), BF16),
                  jax.ShapeDtypeStruct((n_tok, n_cols - n_qkv), BF16)]
    if fused:
        x1t, y, ln_g, ln_b = src
        tiles = lambda off: pl.BlockSpec((tm * ROW_CHUNKS, LANES), lambda i, off=off: (i + off, 0))
        vec = pl.BlockSpec((1, d), lambda i: (0, 0))
        in_specs = [tiles(0), tiles(0), tiles(n_t), vec, vec, wspec, table, table]
        operands = (x1t, y, y, ln_g, ln_b, w, cos, sin)
    else:
        in_specs = [pl.BlockSpec((tm, d), lambda i: (jnp.minimum(i, n_first - 1), 0)),
                    pl.BlockSpec((tm, d), lambda i: (jnp.maximum(i - n_first, 0), 0)),
                    wspec, table, table]
        operands = (src[0], src[1], w, cos, sin)
    return pl.pallas_call(
        kern, grid=(n_t,),
        in_specs=in_specs,
        out_specs=[rows(d)] + outs,
        out_shape=[jax.ShapeDtypeStruct((n_tok, d), F32)] + out_shapes,
        compiler_params=_cparams(("parallel",)),
    )(*operands)


def _attn_kernel(sc_ref, q_ref, k_ref, v_ref, g_ref, o_ref, vx_ref, *, bq):
    seq_len = q_ref.shape[0]
    lam = sc_ref[0]
    out_scale = sc_ref[1]
    k = k_ref[...]
    vx_ref[:, :DA_V_DIM] = v_ref[...]
    vx_ref[:, DA_V_DIM:] = jnp.ones((seq_len, DA_V_DIM), vx_ref.dtype)
    vx = vx_ref[...]
    g = g_ref[...] * out_scale
    lane = lax.broadcasted_iota(jnp.int32, (bq, LANES), 1)
    is_first = lane < DA_HEAD_DIM
    dn = (((1,), (1,)), ((), ()))

    def softmax_av(qm):
        s = lax.dot_general(qm, k, dn, preferred_element_type=F32)
        p = jnp.exp2(s - jnp.max(s, axis=-1, keepdims=True)).astype(BF16)
        ox = jnp.dot(p, vx, preferred_element_type=F32)
        return ox[:, :DA_V_DIM] / ox[:, DA_V_DIM:]

    for i in range(seq_len // bq):
        rows = slice(i * bq, (i + 1) * bq)
        qb = q_ref[rows, :]
        zero = jnp.zeros_like(qb)
        o = (softmax_av(jnp.where(is_first, qb, zero))
             - lam * softmax_av(jnp.where(is_first, zero, qb)))
        o = o * lax.rsqrt(jnp.mean(o * o, axis=-1, keepdims=True) + LN_EPS)
        o_ref[rows, :] = (o * g).astype(o_ref.dtype)


def _attention(qkv, scalars, subln_g, n_seq, seq_len):
    n_tok = qkv.shape[0]
    bq = min(128, seq_len)
    return pl.pallas_call(
        functools.partial(_attn_kernel, bq=bq),
        grid=(n_seq, DA_HEADS),
        in_specs=[
            pl.BlockSpec(memory_space=pltpu.SMEM),
            pl.BlockSpec((seq_len, LANES), lambda b, h: (b, h)),
            pl.BlockSpec((seq_len, LANES), lambda b, h: (b, DA_HEADS + h)),
            pl.BlockSpec((seq_len, LANES), lambda b, h: (b, 2 * DA_HEADS + h)),
            pl.BlockSpec((1, DA_V_DIM), lambda b, h: (0, 0)),
        ],
        out_specs=pl.BlockSpec((seq_len, DA_V_DIM), lambda b, h: (b, h)),
        out_shape=jax.ShapeDtypeStruct((n_tok, DA_HEADS * DA_V_DIM), BF16),
        scratch_shapes=[pltpu.VMEM((seq_len, 2 * DA_V_DIM), BF16)],
        compiler_params=_cparams(("parallel", "parallel")),
    )(scalars, qkv, qkv, qkv, subln_g)


def _split3(x):
    x1 = x.astype(BF16)
    r1 = x - x1.astype(F32)
    x2 = r1.astype(BF16)
    x3 = (r1 - x2.astype(F32)).astype(BF16)
    return x1, x2, x3


def _gate_tables(gacc_ref, bg_ref, g1_ref, g2_ref, g3_ref, tmp_ref, last_ref):
    seq_len = gacc_ref.shape[0]
    n_chunks = seq_len // CHUNK
    r = lax.broadcasted_iota(jnp.int32, (CHUNK, CHUNK), 0)
    c = lax.broadcasted_iota(jnp.int32, (CHUNK, CHUNK), 1)
    tri = jnp.where(c <= r, 1.0, 0.0).astype(BF16)
    lane = lax.broadcasted_iota(jnp.int32, (CHUNK, LANES), 1)
    row = lax.broadcasted_iota(jnp.int32, (CHUNK, LANES), 0)
    is_kind0 = (lane % 2) == 0
    is_bwd = ((lane // 2) % 2) == 1
    is_bwd_row = is_bwd[0:1, :]
    bg = bg_ref[...]

    def first_pass(ci, carry):
        rows = pl.ds(pl.multiple_of(ci * CHUNK, CHUNK), CHUNK)
        pre = gacc_ref[rows, :] + bg
        lf = jnp.minimum(pre, 0.0) - jnp.log1p(jnp.exp(-jnp.abs(pre)))
        l1, l2, l3 = _split3(lf)
        pref = (jnp.dot(tri, l1, preferred_element_type=F32)
                + jnp.dot(tri, l2, preferred_element_type=F32)
                + jnp.dot(tri, l3, preferred_element_type=F32))
        suff = pref[CHUNK - 1:CHUNK, :] - pref + lf
        cum = jnp.where(is_bwd, suff, pref)
        b0 = pltpu.roll(cum, LANES - 1, 1)
        g = pre - b0
        mx_f = g
        mx_b = g
        s = 1
        while s < CHUNK:
            mx_f = jnp.maximum(mx_f, jnp.where(row >= s, pltpu.roll(mx_f, s, 0), -jnp.inf))
            mx_b = jnp.maximum(mx_b, jnp.where(row < CHUNK - s,
                                               pltpu.roll(mx_b, CHUNK - s, 0), -jnp.inf))
            s *= 2
        mx = jnp.where(is_bwd, mx_b, mx_f)
        tmp_ref[0, rows, :] = g
        tmp_ref[1, rows, :] = mx
        tmp_ref[2, rows, :] = b0
        last_ref[0, pl.ds(ci, 1), :] = jnp.where(is_bwd_row, b0[0:1, :], b0[CHUNK - 1:CHUNK, :])
        last_ref[1, pl.ds(ci, 1), :] = jnp.where(is_bwd_row, mx[0:1, :], mx[CHUNK - 1:CHUNK, :])
        return carry

    lax.fori_loop(0, n_chunks, first_pass, 0)

    m_f = jnp.zeros((1, LANES), F32)
    m_b = jnp.zeros((1, LANES), F32)
    for t in range(n_chunks):
        cf, cb = t, n_chunks - 1 - t
        last_ref[2, cf:cf + 1, :] = m_f
        last_ref[3, cb:cb + 1, :] = m_b
        m_f = last_ref[0, cf:cf + 1, :] + jnp.maximum(m_f, last_ref[1, cf:cf + 1, :])
        m_b = last_ref[0, cb:cb + 1, :] + jnp.maximum(m_b, last_ref[1, cb:cb + 1, :])

    def second_pass(ci, carry):
        rows = pl.ds(pl.multiple_of(ci * CHUNK, CHUNK), CHUNK)
        g = tmp_ref[0, rows, :]
        mx = tmp_ref[1, rows, :]
        b0 = tmp_ref[2, rows, :]
        m_st = jnp.where(is_bwd_row, last_ref[3, pl.ds(ci, 1), :], last_ref[2, pl.ds(ci, 1), :])
        mm = jnp.maximum(mx, m_st)
        m_up = jnp.maximum(m_st, last_ref[1, pl.ds(ci, 1), :])
        keep = jnp.broadcast_to(jnp.exp(m_st - m_up), (CHUNK, LANES))
        t1 = jnp.where(is_kind0, -mm, pltpu.roll(jnp.exp(-mm - b0), 1, 1))
        t2 = jnp.where(is_kind0, jnp.exp(m_st - mm), pltpu.roll(keep, 1, 1))
        t3 = jnp.where(is_kind0, g, pltpu.roll(jnp.exp(g - m_up), 1, 1)).T
        for h in range(ML_HEADS):
            g1_ref[h, rows, :] = t1[:, 4 * h:4 * h + 4]
            g2_ref[h, rows, :] = t2[:, 4 * h:4 * h + 4]
            g3_ref[h, :, rows] = t3[4 * h:4 * h + 4, :]
        return carry

    lax.fori_loop(0, n_chunks, second_pass, 0)


def _mlpre_kernel(xm_ref, cw_ref, cb_ref, wq_ref, wkt_ref, wk_ref, wv_ref, wg_ref, bg_ref,
                  q_ref, kt_ref, v_ref, xc_ref, g1_ref, g2_ref, g3_ref,
                  gacc_ref, tmp_ref, last_ref):
    j = pl.program_id(1)
    seq_len = xm_ref.shape[0]
    xm = xm_ref[...].astype(F32)
    row = lax.broadcasted_iota(jnp.int32, xm.shape, 0)
    prev = jnp.where(row == 0, 0.0, pltpu.roll(xm, 1, 0))
    nxt = jnp.where(row == seq_len - 1, 0.0, pltpu.roll(xm, seq_len - 1, 0))
    xc = cb_ref[...] + prev * cw_ref[0:1, :] + xm * cw_ref[1:2, :] + nxt * cw_ref[2:3, :]
    xc = xc * _sigmoid(xc)
    xcb = xc.astype(BF16)
    xc_ref[...] = xcb
    q = jnp.dot(xcb, wq_ref[...], preferred_element_type=F32)
    k = jnp.dot(xcb, wk_ref[...], preferred_element_type=F32)
    v = jnp.dot(xm_ref[...], wv_ref[...], preferred_element_type=F32)
    qb = q.astype(BF16)
    kb = k.astype(BF16)
    vb = v.astype(BF16)
    q_ref[...] = (q * (ML_HEAD_DIM ** -0.5)).astype(BF16)
    kt_ref[...] = lax.dot_general(wkt_ref[...], xcb, (((1,), (1,)), ((), ())),
                                  preferred_element_type=F32).astype(BF16)
    v_ref[...] = vb
    part = (jnp.dot(qb, wg_ref[0], preferred_element_type=F32)
            + jnp.dot(kb, wg_ref[1], preferred_element_type=F32)
            + jnp.dot(vb, wg_ref[2], preferred_element_type=F32))

    @pl.when(j == 0)
    def _():
        gacc_ref[...] = part

    @pl.when(j > 0)
    def _():
        gacc_ref[...] += part

    @pl.when(j == pl.num_programs(1) - 1)
    def _():
        _gate_tables(gacc_ref, bg_ref, g1_ref, g2_ref, g3_ref, tmp_ref, last_ref)


def _mlstm_pre(rest, conv_w, conv_b, wq_bd, wk_bd, wv_bd, wg, bg, n_seq, seq_len):
    n_tok = rest.shape[0]
    n_ct = D_MODEL // MXU_DIM
    tile = pl.BlockSpec((seq_len, MXU_DIM), lambda b, j: (b, j))
    wspec = pl.BlockSpec((None, MXU_DIM, MXU_DIM), lambda b, j: (j, 0, 0))
    gcol = pl.BlockSpec((None, ML_HEADS, seq_len, 4), lambda b, j: (b, 0, 0, 0))
    grow = pl.BlockSpec((None, ML_HEADS, 4, seq_len), lambda b, j: (b, 0, 0, 0))
    act = jax.ShapeDtypeStruct((n_tok, D_MODEL), BF16)
    gate_col = jax.ShapeDtypeStruct((n_seq, ML_HEADS, seq_len, 4), F32)
    gate_row = jax.ShapeDtypeStruct((n_seq, ML_HEADS, 4, seq_len), F32)
    return pl.pallas_call(
        _mlpre_kernel,
        grid=(n_seq, n_ct),
        in_specs=[
            tile,
            pl.BlockSpec((3, MXU_DIM), lambda b, j: (0, j)),
            pl.BlockSpec((1, MXU_DIM), lambda b, j: (0, j)),
            wspec, wspec, wspec, wspec,
            pl.BlockSpec((3, MXU_DIM, LANES), lambda b, j: (0, j, 0)),
            pl.BlockSpec((1, LANES), lambda b, j: (0, 0)),
        ],
        out_specs=[tile, pl.BlockSpec((MXU_DIM, seq_len), lambda b, j: (j, b)), tile, tile,
                   gcol, gcol, grow],
        out_shape=[act, jax.ShapeDtypeStruct((D_MODEL, n_tok), BF16), act, act,
                   gate_col, gate_col, gate_row],
        scratch_shapes=[pltpu.VMEM((seq_len, LANES), F32),
                        pltpu.VMEM((3, seq_len, LANES), F32),
                        pltpu.VMEM((4, seq_len // CHUNK, LANES), F32)],
        compiler_params=_cparams(("parallel", "arbitrary")),
    )(rest, conv_w, conv_b, wq_bd, jnp.swapaxes(wk_bd, 1, 2), wk_bd, wv_bd, wg, bg)


def _mlstm_kernel(q_ref, kt_ref, v_ref, gc1_ref, gc2_ref, gr_ref, o_ref,
                  vx_ref, qk_ref, h_ref):
    seq_len = q_ref.shape[0]
    n_chunks = seq_len // CHUNK
    dh = ML_HEAD_DIM
    r = lax.broadcasted_iota(jnp.int32, (CHUNK, CHUNK), 0)
    c = lax.broadcasted_iota(jnp.int32, (CHUNK, CHUNK), 1)
    vx_ref[:, :dh] = v_ref[...]
    vx_ref[:, dh:] = jnp.ones((seq_len, LANES), vx_ref.dtype)

    def lane_replicated(ref, rows, col):
        return jnp.broadcast_to(ref[rows, col:col + 1], (CHUNK, LANES))

    def chunk_step(direction, ci, c_st):
        mask = (c <= r) if direction == 0 else (c >= r)
        rows = slice(ci * CHUNK, (ci + 1) * CHUNK)
        qc = q_ref[rows, :]
        ktc = kt_ref[:, rows]
        vxc = vx_ref[rows, :]
        neg_mm = lane_replicated(gc1_ref, rows, 2 * direction)
        e_mj = lane_replicated(gc1_ref, rows, 2 * direction + 1)
        inter = lane_replicated(gc2_ref, rows, 2 * direction)
        g_row = gr_ref[2 * direction:2 * direction + 1, rows]
        w_row = gr_ref[2 * direction + 1:2 * direction + 2, rows]
        keep = gc2_ref[ci * CHUNK:ci * CHUNK + 1, 2 * direction + 1:2 * direction + 2]
        if ci in first_visit:
            qk = qk_ref[ci]
        else:
            qk = jnp.dot(qc, ktc, preferred_element_type=F32)
            qk_ref[ci] = qk
        sw = qk * jnp.exp(jnp.where(mask, neg_mm + g_row, -jnp.inf))
        intra = jnp.dot(sw.astype(BF16), vxc, preferred_element_type=F32)
        carry_in = jnp.dot(qc, c_st.astype(BF16), preferred_element_type=F32)
        den = intra[:, dh:] + inter * carry_in[:, dh:]
        rdiv = 1.0 / jnp.maximum(jnp.abs(den), e_mj)
        h = jnp.concatenate(
            [(intra[:, s * LANES:(s + 1) * LANES] + inter * carry_in[:, s * LANES:(s + 1) * LANES])
             * rdiv for s in range(dh // LANES)], axis=1)
        if ci in first_visit:
            h = h_ref[rows, :] + h
            mu = jnp.mean(h, axis=-1, keepdims=True)
            hc = h - mu
            var = jnp.mean(hc * hc, axis=-1, keepdims=True)
            o_ref[rows, :] = (hc * lax.rsqrt(var + LN_EPS)).astype(o_ref.dtype)
        else:
            h_ref[rows, :] = h
            first_visit.add(ci)
        kw = (ktc.astype(F32) * w_row).astype(BF16)
        return keep * c_st + jnp.dot(kw, vxc, preferred_element_type=F32)

    first_visit = set()
    c_f = jnp.zeros((dh, dh + LANES), F32)
    c_b = jnp.zeros((dh, dh + LANES), F32)
    for step in range(n_chunks):
        c_f = chunk_step(0, step, c_f)
        c_b = chunk_step(1, n_chunks - 1 - step, c_b)


def _mlstm(q, kt, v, gc1, gc2, gr3, n_seq, seq_len):
    n_tok = q.shape[0]
    tile = pl.BlockSpec((seq_len, ML_HEAD_DIM), lambda b, h: (b, h))
    col = pl.BlockSpec((None, None, seq_len, 4), lambda b, h: (b, h, 0, 0))
    return pl.pallas_call(
        _mlstm_kernel,
        grid=(n_seq, ML_HEADS),
        in_specs=[
            tile,
            pl.BlockSpec((ML_HEAD_DIM, seq_len), lambda b, h: (h, b)),
            tile, col, col,
            pl.BlockSpec((None, None, 4, seq_len), lambda b, h: (b, h, 0, 0)),
        ],
        out_specs=tile,
        out_shape=jax.ShapeDtypeStruct((n_tok, D_MODEL), BF16),
        scratch_shapes=[
            pltpu.VMEM((seq_len, ML_HEAD_DIM + LANES), BF16),
            pltpu.VMEM((seq_len // CHUNK, CHUNK, CHUNK), F32),
            pltpu.VMEM((seq_len, ML_HEAD_DIM), F32),
        ],
        compiler_params=_cparams(("parallel", "parallel")),
    )(q, kt, v, gc1, gc2, gr3)


def _layer_norm(y, g, b):
    mu = jnp.mean(y, axis=-1, keepdims=True)
    yc = y - mu
    var = jnp.mean(yc * yc, axis=-1, keepdims=True)
    return yc * lax.rsqrt(var + LN_EPS) * g + b


def _route(logits):
    lane = lax.broadcasted_iota(jnp.int32, logits.shape, 1)
    big = jnp.int32(4 * LANES)
    gl = jnp.where(lane < N_GROUPS, logits, NEG_BIG)
    gmax = jnp.max(gl, axis=-1, keepdims=True)
    gsum = jnp.sum(jnp.where(lane < N_GROUPS, jnp.exp(gl - gmax), 0.0), axis=-1, keepdims=True)
    g_sel = jnp.min(jnp.where(gl == gmax, lane, big), axis=-1, keepdims=True)
    g_prob = 1.0 / gsum
    lo = N_GROUPS + EXPERTS_PER_GROUP * g_sel
    in_group = jnp.logical_and(lane >= lo, lane < lo + EXPERTS_PER_GROUP)
    el = jnp.where(in_group, logits, NEG_BIG)
    emax = jnp.max(el, axis=-1, keepdims=True)
    esum = jnp.sum(jnp.where(in_group, jnp.exp(el - emax), 0.0), axis=-1, keepdims=True)
    i1 = jnp.min(jnp.where(el == emax, lane, big), axis=-1, keepdims=True)
    el2 = jnp.where(lane == i1, NEG_BIG, el)
    emax2 = jnp.max(el2, axis=-1, keepdims=True)
    i2 = jnp.min(jnp.where(el2 == emax2, lane, big), axis=-1, keepdims=True)
    p1 = 1.0 / esum
    p2 = jnp.exp(emax2 - emax) / esum
    psum = p1 + p2
    gate1 = g_prob * p1 / psum
    gate2 = g_prob * p2 / psum
    e1 = (i1 - N_GROUPS).astype(F32)
    e2 = (i2 - N_GROUPS).astype(F32)
    return jnp.where(lane == 0, e1, jnp.where(lane == 1, e2, jnp.where(lane == 2, gate1, gate2)))


def _post_kernel(attn_ref, hn_ref, xc_ref, z_ref, ga_ref, gb_ref, x_ref,
                 wpa_ref, wpb_ref, wout_ref, gn_ref, skip_ref, lg_ref, lb_ref, wr_ref, br_ref,
                 x1t_ref, route_ref, *, sub):
    tm = x_ref.shape[0]
    for s in range(tm // sub):
        rows = slice(s * sub, (s + 1) * sub)
        z = z_ref[rows, :].astype(F32)
        ml = ((hn_ref[rows, :].astype(F32) * gn_ref[...]
               + skip_ref[...] * xc_ref[rows, :].astype(F32)) * (z * _sigmoid(z)))
        a_out = jnp.dot(attn_ref[rows, :], wpa_ref[...], preferred_element_type=F32)
        m_out = jnp.dot(ml.astype(BF16), wpb_ref[...], preferred_element_type=F32)
        mixed = (_sigmoid(ga_ref[rows, :].astype(F32)) * a_out
                 + _sigmoid(gb_ref[rows, :].astype(F32)) * m_out)
        y = DN_ALPHA * x_ref[rows, :] + jnp.dot(mixed.astype(BF16), wout_ref[...],
                                               preferred_element_type=F32)
        x1 = _layer_norm(y, lg_ref[...], lb_ref[...])
        for c in range(ROW_CHUNKS):
            x1t_ref[pl.ds(s * sub * ROW_CHUNKS + c, sub, stride=ROW_CHUNKS), :] = (
                x1[:, c * LANES:(c + 1) * LANES])
        logits = br_ref[...] + jnp.dot(x1.astype(BF16), wr_ref[...], preferred_element_type=F32)
        route_ref[rows, :] = _route(logits)


def _post(attn, hn, xc, rest, x, w_pa, w_pb, w_out, gn_g, skip, ln_g, ln_b, w_route, b_route):
    n_tok = x.shape[0]
    sub = 256
    tm = 2 * sub if n_tok % (2 * sub) == 0 else sub
    row = lambda col: pl.BlockSpec((tm, D_MODEL), lambda i, col=col: (i, col))
    full = lambda shape: pl.BlockSpec(shape, lambda i: tuple(0 for _ in shape))
    vec = full((1, D_MODEL))
    return pl.pallas_call(
        functools.partial(_post_kernel, sub=sub),
        grid=(n_tok // tm,),
        in_specs=[row(0), row(0), row(0), row(1), row(2), row(3), row(0),
                  full((D_MODEL, D_MODEL)), full((D_MODEL, D_MODEL)), full((D_MODEL, D_MODEL)),
                  vec, vec, vec, vec,
                  full((D_MODEL, LANES)), full((1, LANES))],
        out_specs=[pl.BlockSpec((tm * ROW_CHUNKS, LANES), lambda i: (i, 0)),
                   pl.BlockSpec((tm, LANES), lambda i: (i, 0))],
        out_shape=[jax.ShapeDtypeStruct((n_tok * ROW_CHUNKS, LANES), F32),
                   jax.ShapeDtypeStruct((n_tok, LANES), F32)],
        compiler_params=_cparams(("parallel",)),
    )(attn, hn, xc, rest, rest, rest, x, w_pa, w_pb, w_out, gn_g, skip, ln_g, ln_b,
      w_route, b_route)


def _moe_kernel(bexp_ref, nused_ref,
                tok_hbm, dst_hbm, gate_ref, x_hbm, wg_ref, wu_ref, wd_ref, y_hbm,
                tok_smem, dst_smem, xbuf, obuf, idx_sem, gat_sem, sca_sem):
    i = pl.program_id(0)
    n_used = nused_ref[0]
    p = i % 2

    def index_copies(blk):
        slot = blk % 3
        return (pltpu.make_async_copy(tok_hbm.at[blk], tok_smem.at[slot], idx_sem.at[0, slot]),
                pltpu.make_async_copy(dst_hbm.at[blk], dst_smem.at[slot], idx_sem.at[1, slot]))

    def start_indices(blk):
        for cp in index_copies(blk):
            cp.start()

    def wait_indices(blk):
        for cp in index_copies(blk):
            cp.wait()

    def tile_rows(r):
        return pl.ds(r * ROW_CHUNKS, ROW_CHUNKS)

    def start_gather(blk, slot, rows=range(MOE_ROWS)):
        islot = blk % 3

        for row in rows:
            t = tok_smem[islot, row // LANES, row % LANES]
            pltpu.make_async_copy(x_hbm.at[t], xbuf.at[slot, tile_rows(row)],
                                  gat_sem.at[slot]).start(priority=row % DMA_PRIORITIES)

    def wait_gather(slot):
        pltpu.make_async_copy(xbuf.at[slot], xbuf.at[slot], gat_sem.at[slot]).wait()

    def start_scatter(blk, slot, rows):
        islot = blk % 3

        for row in rows:
            t = dst_smem[islot, row // LANES, row % LANES]
            pltpu.make_async_copy(obuf.at[slot, tile_rows(row)], y_hbm.at[t],
                                  sca_sem.at[slot]).start(priority=row % DMA_PRIORITIES)

    def wait_scatter(slot):
        pltpu.make_async_copy(obuf.at[slot], obuf.at[slot], sca_sem.at[slot]).wait()

    @pl.when(i == 0)
    def _():
        start_indices(0)
        wait_indices(0)
        start_gather(0, 0)

        @pl.when(1 < n_used)
        def _():
            start_indices(1)

    @pl.when(i + 1 < n_used)
    def _():
        wait_indices(i + 1)

    @pl.when(i + 2 < n_used)
    def _():
        start_indices(i + 2)

    @pl.when(i < n_used)
    def _():
        wait_gather(p)

        @pl.when(i >= 2)
        def _():
            wait_scatter(p)

        nxt = jnp.minimum(i + 1, n_used - 1)
        half = MOE_ROWS // 2
        for h in range(2):
            rows = range(h * half, (h + 1) * half)
            start_gather(nxt, 1 - p, rows)
            xb = jnp.concatenate(
                [xbuf[p, pl.ds(h * half * ROW_CHUNKS + c, half, stride=ROW_CHUNKS), :]
                 for c in range(ROW_CHUNKS)], axis=1).astype(BF16)
            hg = jnp.dot(xb, wg_ref[...], preferred_element_type=F32)
            hu = jnp.dot(xb, wu_ref[...], preferred_element_type=F32)
            hh = (hg * _sigmoid(hg) * hu).astype(BF16)
            out = (jnp.dot(hh, wd_ref[...], preferred_element_type=F32)
                   * gate_ref[h * half:(h + 1) * half, :])
            for c in range(ROW_CHUNKS):
                obuf[p, pl.ds(h * half * ROW_CHUNKS + c, half, stride=ROW_CHUNKS), :] = (
                    out[:, c * LANES:(c + 1) * LANES])
            start_scatter(i, p, rows)

        @pl.when(i == n_used - 1)
        def _():
            wait_scatter(p)
            wait_gather(1 - p)

            @pl.when(i >= 1)
            def _():
                wait_scatter(1 - p)


def _moe(x1t, slot_tok, slot_dst, slot_gate, blk_exp, n_used, we_g, we_u, we_d):
    n_tok = x1t.shape[0] // ROW_CHUNKS
    n_blk = blk_exp.shape[0]
    n_out = TOP_K * n_tok + 2 * MOE_ROWS
    wspec_in = pl.BlockSpec((None, D_MODEL, D_EXPERT), lambda i, be, nu: (be[i], 0, 0))
    wspec_out = pl.BlockSpec((None, D_EXPERT, D_MODEL), lambda i, be, nu: (be[i], 0, 0))
    grid_spec = pltpu.PrefetchScalarGridSpec(
        num_scalar_prefetch=2,
        grid=(n_blk,),
        in_specs=[
            pl.BlockSpec(memory_space=pl.ANY),
            pl.BlockSpec(memory_space=pl.ANY),
            pl.BlockSpec((MOE_ROWS, 1), lambda i, be, nu: (i, 0)),
            pl.BlockSpec(memory_space=pl.ANY),
            wspec_in, wspec_in, wspec_out,
        ],
        out_specs=pl.BlockSpec(memory_space=pl.ANY),
        scratch_shapes=[
            pltpu.SMEM((3, MOE_ROWS // LANES, LANES), jnp.int32),
            pltpu.SMEM((3, MOE_ROWS // LANES, LANES), jnp.int32),
            pltpu.VMEM((2, MOE_ROWS * ROW_CHUNKS, LANES), F32),
            pltpu.VMEM((2, MOE_ROWS * ROW_CHUNKS, LANES), F32),
            pltpu.SemaphoreType.DMA((2, 3)),
            pltpu.SemaphoreType.DMA((2,)),
            pltpu.SemaphoreType.DMA((2,)),
        ],
    )
    y = pl.pallas_call(
        _moe_kernel,
        grid_spec=grid_spec,
        out_shape=jax.ShapeDtypeStruct((n_out, ROW_CHUNKS, LANES), F32),
        compiler_params=_cparams(("arbitrary",)),
    )(blk_exp, n_used, slot_tok.reshape(n_blk, MOE_ROWS // LANES, LANES),
      slot_dst.reshape(n_blk, MOE_ROWS // LANES, LANES), slot_gate.reshape(n_blk * MOE_ROWS, 1),
      x1t.reshape(n_tok, ROW_CHUNKS, LANES), we_g, we_u, we_d)
    return y.reshape(n_out * ROW_CHUNKS, LANES)


def _route_plan(route, n_tok):
    n_assign = n_tok * TOP_K
    eid = route[:, 0:TOP_K].astype(jnp.int32).reshape(n_assign)
    gate = route[:, TOP_K:2 * TOP_K].reshape(n_assign)
    order = jnp.argsort(eid).astype(jnp.int32)
    counts = jnp.zeros((N_EXPERTS,), jnp.int32).at[eid].add(1)
    offsets = jnp.cumsum(counts) - counts
    padded = (counts + MOE_ROWS - 1) // MOE_ROWS * MOE_ROWS
    pad_end = jnp.cumsum(padded)
    pad_start = pad_end - padded
    n_blk = (n_assign + MOE_ROWS - 1) // MOE_ROWS + N_EXPERTS
    blk_start = jnp.arange(n_blk, dtype=jnp.int32) * MOE_ROWS
    blk_exp = jnp.minimum(jnp.sum(pad_end[None, :] <= blk_start[:, None], axis=1),
                          N_EXPERTS - 1).astype(jnp.int32)
    row = jnp.arange(MOE_ROWS, dtype=jnp.int32)[None, :]
    j = (blk_start - pad_start[blk_exp])[:, None] + row
    valid = j < counts[blk_exp][:, None]
    src = jnp.clip(offsets[blk_exp][:, None] + j, 0, n_assign - 1)
    assign = order[src]
    tok = assign // TOP_K
    dummy = TOP_K * n_tok + (jnp.arange(n_blk, dtype=jnp.int32) % 2)[:, None] * MOE_ROWS + row
    slot_tok = jnp.where(valid, tok, 0)
    slot_dst = jnp.where(valid, (assign % TOP_K) * n_tok + tok, dummy)
    slot_gate = jnp.where(valid, gate[assign], 0.0)
    n_used = (pad_end[-1:] // MOE_ROWS).astype(jnp.int32)
    return slot_tok, slot_dst, slot_gate, blk_exp, n_used


def _combine_rows(x_ref, y0_ref, y1_ref, g_ref, b_ref, tm):
    def chunk(ref, c):
        return ref[pl.ds(c, tm, stride=ROW_CHUNKS), :]

    y = jnp.concatenate(
        [DN_ALPHA * chunk(x_ref, c) + (chunk(y0_ref, c) + chunk(y1_ref, c))
         for c in range(ROW_CHUNKS)], axis=1)
    return _layer_norm(y, g_ref[...], b_ref[...])


def _combine_kernel(x_ref, y0_ref, y1_ref, g_ref, b_ref, o_ref):
    o_ref[...] = _combine_rows(x_ref, y0_ref, y1_ref, g_ref, b_ref, o_ref.shape[0])


def _combine(x1t, y, ln_g, ln_b, first_tok, n_out):
    n_tok = x1t.shape[0] // ROW_CHUNKS
    tm = 256
    n_t = n_tok // tm
    t0 = first_tok // tm
    vec = pl.BlockSpec((1, D_MODEL), lambda i: (0, 0))
    tiles = lambda off: pl.BlockSpec((tm * ROW_CHUNKS, LANES), lambda i, off=off: (i + off, 0))
    return pl.pallas_call(
        _combine_kernel,
        grid=(n_out // tm,),
        in_specs=[tiles(t0), tiles(t0), tiles(t0 + n_t), vec, vec],
        out_specs=pl.BlockSpec((tm, D_MODEL), lambda i: (i, 0)),
        out_shape=jax.ShapeDtypeStruct((n_out, D_MODEL), F32),
        compiler_params=_cparams(("parallel",)),
    )(x1t, y, y, ln_g, ln_b)


def _rope_tables(seq_len):
    half = DA_HEAD_DIM // 2
    inv = ROPE_THETA ** (-jnp.arange(half, dtype=F32) * (2.0 / DA_HEAD_DIM))
    ang = jnp.arange(seq_len, dtype=F32)[:, None] * inv[None, :]
    cos = jnp.tile(jnp.cos(ang), (1, LANES // half))
    sin = jnp.tile(jnp.sin(ang), (1, LANES // half))
    lane = jnp.arange(LANES)
    sign = jnp.where((lane % DA_HEAD_DIM) < half, -1.0, 1.0).astype(F32)
    return cos, sin * sign[None, :]


def _block_diag_tiles(w):
    per_tile = MXU_DIM // ML_PROJ_BLOCK
    n_tiles = w.shape[0] // per_tile
    w4 = w.reshape(n_tiles, per_tile, ML_PROJ_BLOCK, ML_PROJ_BLOCK)
    eye = jnp.eye(per_tile, dtype=w.dtype)
    bd = jnp.einsum('jgio,gh->jgiho', w4, eye)
    return bd.reshape(n_tiles, MXU_DIM, MXU_DIM).astype(BF16)


def _gate_perm():
    idx = []
    for h in range(ML_HEADS):
        for d in range(2):
            for kind in range(2):
                idx.append(d * 2 * ML_HEADS + kind * ML_HEADS + h)
    return jnp.array(idx, dtype=jnp.int32)


def _layer(src, n_seq, seq_len, lambda_init, cos, sin, p):
    qk_w = DA_HEADS * 2 * DA_HEAD_DIM
    v_w = DA_HEADS * DA_V_DIM
    x, qkv, rest = _inproj(src, p['w_in'].astype(BF16), cos, sin, seq_len, n_rope=2 * qk_w,
                           n_scaled=qk_w, n_qkv=2 * qk_w + v_w,
                           scale=DA_HEAD_DIM ** -0.5 * math.log2(math.e))
    n_tok = x.shape[0]

    lam = (jnp.exp(jnp.sum(p['lq1'] * p['lk1'])) - jnp.exp(jnp.sum(p['lq2'] * p['lk2']))
           + lambda_init)
    scalars = jnp.stack([lam, jnp.asarray(1.0 - lambda_init, F32)]).astype(F32)
    attn = _attention(qkv, scalars, p['subln_g'].reshape(1, DA_V_DIM), n_seq, seq_len)

    perm = _gate_perm()
    n_gate = 4 * ML_HEADS
    wg = p['w_gate'][:, perm].reshape(3, D_MODEL, n_gate)
    wg = jnp.pad(wg, ((0, 0), (0, 0), (0, LANES - n_gate))).astype(BF16)
    bg = jnp.pad(p['b_gate'][perm], (0, LANES - n_gate)).reshape(1, LANES)
    q, kt, v, xc, g1, g2, g3 = _mlstm_pre(
        rest, p['conv_w'], p['conv_b'].reshape(1, D_MODEL), _block_diag_tiles(p['wq']),
        _block_diag_tiles(p['wk']), _block_diag_tiles(p['wv']), wg, bg, n_seq, seq_len)

    hn = _mlstm(q, kt, v, g1, g2, g3, n_seq, seq_len)

    n_route = N_GROUPS + N_EXPERTS
    w_route = jnp.concatenate([p['rg_w'], p['re_w']], axis=1)
    w_route = jnp.pad(w_route, ((0, 0), (0, LANES - n_route)))
    w_route = w_route.astype(BF16)
    b_route = jnp.pad(jnp.concatenate([p['rg_b'], p['re_b']]), (0, LANES - n_route))
    vec = lambda a: a.reshape(1, D_MODEL)
    x1, route = _post(attn, hn, xc, rest, x, p['w_pa'].astype(BF16), p['w_pb'].astype(BF16),
                      p['w_out'].astype(BF16), vec(p['gn_g']), vec(p['skip']), vec(p['ln1_g']),
                      vec(p['ln1_b']), w_route, b_route.reshape(1, LANES))

    plan = _route_plan(route, n_tok)
    y = _moe(x1, *plan, p['we_g'].astype(BF16), p['we_u'].astype(BF16), p['we_d'].astype(BF16))
    return x1, y, vec(p['ln2_g']), vec(p['ln2_b'])


def kernel(x_prompt, x_sample, w_in, da_lambda_q1, da_lambda_k1, da_lambda_q2, da_lambda_k2, da_subln_g, ml_conv_w, ml_conv_b, ml_wq, ml_wk, ml_wv, ml_w_gate, ml_b_gate, ml_skip, ml_gn_g, w_pa, w_pb, w_out, ln1_g, ln1_b, router_group_w, router_group_b, router_expert_w, router_expert_b, w_e_gate, w_e_up, w_e_down, ln2_g, ln2_b):
    n_p, seq_len, d = x_prompt.shape
    n_s = x_sample.shape[0]
    assert x_sample.shape[1] == seq_len and d == D_MODEL and seq_len % CHUNK == 0
    n_seq = n_p + n_s
    cos, sin = _rope_tables(seq_len)
    stacked = dict(w_in=w_in, lq1=da_lambda_q1, lk1=da_lambda_k1, lq2=da_lambda_q2,
                   lk2=da_lambda_k2, subln_g=da_subln_g, conv_w=ml_conv_w, conv_b=ml_conv_b,
                   wq=ml_wq, wk=ml_wk, wv=ml_wv, w_gate=ml_w_gate, b_gate=ml_b_gate,
                   skip=ml_skip, gn_g=ml_gn_g, w_pa=w_pa, w_pb=w_pb, w_out=w_out,
                   ln1_g=ln1_g, ln1_b=ln1_b, rg_w=router_group_w, rg_b=router_group_b,
                   re_w=router_expert_w, re_b=router_expert_b, we_g=w_e_gate, we_u=w_e_up,
                   we_d=w_e_down, ln2_g=ln2_g, ln2_b=ln2_b)
    src = [x_prompt.reshape(n_p * seq_len, d), x_sample.reshape(n_s * seq_len, d)]
    for l in range(w_in.shape[0]):
        lambda_init = 0.8 - 0.6 * math.exp(-0.3 * l)
        src = _layer(src, n_seq, seq_len, lambda_init, cos, sin,
                     {k: a[l] for k, a in stacked.items()})
    y_p = _combine(*src, 0, n_p * seq_len).reshape(n_p, seq_len, d)
    y_s = _combine(*src, n_p * seq_len, n_s * seq_len).reshape(n_s, seq_len, d)
    return (y_p, y_s)
```

```python
import functools
import math

import jax
import jax.numpy as jnp
from jax import lax
from jax.experimental import pallas as pl
from jax.experimental.pallas import tpu as pltpu

F32 = jnp.float32
BF16 = jnp.bfloat16

D_MODEL = 1024
DEPTH = 4
DA_HEADS = 8
DA_HEAD_DIM = 64
DA_V_DIM = 128
ROPE_THETA = 10000.0
ML_HEADS = 4
ML_HEAD_DIM = 256
ML_PROJ_BLOCK = 4
N_GROUPS = 4
EXPERTS_PER_GROUP = 8
N_EXPERTS = 32
TOP_K = 2
D_EXPERT = 512
DN_ALPHA = (2 * DEPTH) ** 0.25
LN_EPS = 1e-5

LANES = 128
MXU_DIM = 256
ROW_CHUNKS = D_MODEL // LANES
CHUNK = 128
MOE_ROWS = 512
DMA_PRIORITIES = 2
VMEM_LIMIT = 56 * 1024 * 1024

NEG_BIG = -1e30


def _cparams(sem):
    return pltpu.CompilerParams(dimension_semantics=sem, vmem_limit_bytes=VMEM_LIMIT)


def _sigmoid(x):
    return 0.5 * jnp.tanh(0.5 * x) + 0.5


def _inproj_kernel(*refs, fused, n_first, n_rope, n_scaled, n_qkv, scale, cw):
    if fused:
        (x1t_ref, y0_ref, y1_ref, g_ref, b_ref, w_ref, cos_ref, sin_ref,
         x_out_ref, qkv_ref, rest_ref) = refs
        x = _combine_rows(x1t_ref, y0_ref, y1_ref, g_ref, b_ref, qkv_ref.shape[0])
        x_out_ref[...] = x
    else:
        xa_ref, xb_ref, w_ref, cos_ref, sin_ref, x_out_ref, qkv_ref, rest_ref = refs
        x = jnp.where(pl.program_id(0) < n_first, xa_ref[...], xb_ref[...])
        x_out_ref[...] = x
    xb = x.astype(BF16)
    tm = qkv_ref.shape[0]
    n_cols = w_ref.shape[1]
    cos = cos_ref[...]
    sin = sin_ref[...]
    lane = lax.broadcasted_iota(jnp.int32, (tm, LANES), 1)
    first_half = (lane % DA_HEAD_DIM) < (DA_HEAD_DIM // 2)
    for c in range(n_cols // cw):
        acc = jnp.dot(xb, w_ref[:, c * cw:(c + 1) * cw], preferred_element_type=F32)
        for s in range(cw // LANES):
            col = c * cw + s * LANES
            t = acc[:, s * LANES:(s + 1) * LANES]
            if col < n_rope:
                rot = jnp.where(first_half, pltpu.roll(t, LANES - DA_HEAD_DIM // 2, 1),
                                pltpu.roll(t, DA_HEAD_DIM // 2, 1))
                t = t * cos + rot * sin
                if col < n_scaled:
                    t = t * scale
            if col < n_qkv:
                qkv_ref[:, col:col + LANES] = t.astype(qkv_ref.dtype)
            else:
                rest_ref[:, col - n_qkv:col - n_qkv + LANES] = t.astype(rest_ref.dtype)


def _inproj(src, w, cos, sin, seq_len, *, n_rope, n_scaled, n_qkv, scale):
    fused = isinstance(src, tuple)
    d, n_cols = w.shape
    tm = min(512, seq_len)
    n_first = 0 if fused else src[0].shape[0] // tm
    n_tok = src[0].shape[0] // ROW_CHUNKS if fused else src[0].shape[0] + src[1].shape[0]
    n_t = n_tok // tm
    per_seq = seq_len // tm
    kern = functools.partial(_inproj_kernel, fused=fused, n_first=n_first, n_rope=n_rope,
                             n_scaled=n_scaled, n_qkv=n_qkv, scale=scale, cw=512)
    table = pl.BlockSpec((tm, LANES), lambda i: (i % per_seq, 0))
    wspec = pl.BlockSpec((d, n_cols), lambda i: (0, 0), pipeline_mode=pl.Buffered(1))
    rows = lambda n: pl.BlockSpec((tm, n), lambda i: (i, 0))
    outs = [rows(n_qkv), rows(n_cols - n_qkv)]
    out_shapes = [jax.ShapeDtypeStruct((n_tok, n_qkv), BF16),
                  jax.ShapeDtypeStruct((n_tok, n_cols - n_qkv), BF16)]
    if fused:
        x1t, y, ln_g, ln_b = src
        tiles = lambda off: pl.BlockSpec((tm * ROW_CHUNKS, LANES), lambda i, off=off: (i + off, 0))
        vec = pl.BlockSpec((1, d), lambda i: (0, 0))
        in_specs = [tiles(0), tiles(0), tiles(n_t), vec, vec, wspec, table, table]
        operands = (x1t, y, y, ln_g, ln_b, w, cos, sin)
    else:
        in_specs = [pl.BlockSpec((tm, d), lambda i: (jnp.minimum(i, n_first - 1), 0)),
                    pl.BlockSpec((tm, d), lambda i: (jnp.maximum(i - n_first, 0), 0)),
                    wspec, table, table]
        operands = (src[0], src[1], w, cos, sin)
    return pl.pallas_call(
        kern, grid=(n_t,),
        in_specs=in_specs,
        out_specs=[rows(d)] + outs,
        out_shape=[jax.ShapeDtypeStruct((n_tok, d), F32)] + out_shapes,
        compiler_params=_cparams(("parallel",)),
    )(*operands)


def _attn_kernel(sc_ref, q_ref, k_ref, v_ref, g_ref, o_ref, vx_ref, *, bq):
    seq_len = q_ref.shape[0]
    lam = sc_ref[0]
    out_scale = sc_ref[1]
    k = k_ref[...]
    vx_ref[:, :DA_V_DIM] = v_ref[...]
    vx_ref[:, DA_V_DIM:] = jnp.ones((seq_len, DA_V_DIM), vx_ref.dtype)
    vx = vx_ref[...]
    g = g_ref[...] * out_scale
    lane = lax.broadcasted_iota(jnp.int32, (bq, LANES), 1)
    is_first = lane < DA_HEAD_DIM
    dn = (((1,), (1,)), ((), ()))

    def softmax_av(qm):
        s = lax.dot_general(qm, k, dn, preferred_element_type=F32)
        p = jnp.exp2(s - jnp.max(s, axis=-1, keepdims=True)).astype(BF16)
        ox = jnp.dot(p, vx, preferred_element_type=F32)
        return ox[:, :DA_V_DIM] / ox[:, DA_V_DIM:]

    for i in range(seq_len // bq):
        rows = slice(i * bq, (i + 1) * bq)
        qb = q_ref[rows, :]
        zero = jnp.zeros_like(qb)
        o = (softmax_av(jnp.where(is_first, qb, zero))
             - lam * softmax_av(jnp.where(is_first, zero, qb)))
        o = o * lax.rsqrt(jnp.mean(o * o, axis=-1, keepdims=True) + LN_EPS)
        o_ref[rows, :] = (o * g).astype(o_ref.dtype)


def _attention(qkv, scalars, subln_g, n_seq, seq_len):
    n_tok = qkv.shape[0]
    bq = min(128, seq_len)
    return pl.pallas_call(
        functools.partial(_attn_kernel, bq=bq),
        grid=(n_seq, DA_HEADS),
        in_specs=[
            pl.BlockSpec(memory_space=pltpu.SMEM),
            pl.BlockSpec((seq_len, LANES), lambda b, h: (b, h)),
            pl.BlockSpec((seq_len, LANES), lambda b, h: (b, DA_HEADS + h)),
            pl.BlockSpec((seq_len, LANES), lambda b, h: (b, 2 * DA_HEADS + h)),
            pl.BlockSpec((1, DA_V_DIM), lambda b, h: (0, 0)),
        ],
        out_specs=pl.BlockSpec((seq_len, DA_V_DIM), lambda b, h: (b, h)),
        out_shape=jax.ShapeDtypeStruct((n_tok, DA_HEADS * DA_V_DIM), BF16),
        scratch_shapes=[pltpu.VMEM((seq_len, 2 * DA_V_DIM), BF16)],
        compiler_params=_cparams(("parallel", "parallel")),
    )(scalars, qkv, qkv, qkv, subln_g)


def _split3(x):
    x1 = x.astype(BF16)
    r1 = x - x1.astype(F32)
    x2 = r1.astype(BF16)
    x3 = (r1 - x2.astype(F32)).astype(BF16)
    return x1, x2, x3


def _gate_tables(gacc_ref, bg_ref, g1_ref, g2_ref, g3_ref, tmp_ref, last_ref):
    seq_len = gacc_ref.shape[0]
    n_chunks = seq_len // CHUNK
    r = lax.broadcasted_iota(jnp.int32, (CHUNK, CHUNK), 0)
    c = lax.broadcasted_iota(jnp.int32, (CHUNK, CHUNK), 1)
    tri = jnp.where(c <= r, 1.0, 0.0).astype(BF16)
    lane = lax.broadcasted_iota(jnp.int32, (CHUNK, LANES), 1)
    row = lax.broadcasted_iota(jnp.int32, (CHUNK, LANES), 0)
    is_kind0 = (lane % 2) == 0
    is_bwd = ((lane // 2) % 2) == 1
    is_bwd_row = is_bwd[0:1, :]
    bg = bg_ref[...]

    def first_pass(ci, carry):
        rows = pl.ds(pl.multiple_of(ci * CHUNK, CHUNK), CHUNK)
        pre = gacc_ref[rows, :] + bg
        lf = jnp.minimum(pre, 0.0) - jnp.log1p(jnp.exp(-jnp.abs(pre)))
        l1, l2, l3 = _split3(lf)
        pref = (jnp.dot(tri, l1, preferred_element_type=F32)
                + jnp.dot(tri, l2, preferred_element_type=F32)
                + jnp.dot(tri, l3, preferred_element_type=F32))
        suff = pref[CHUNK - 1:CHUNK, :] - pref + lf
        cum = jnp.where(is_bwd, suff, pref)
        b0 = pltpu.roll(cum, LANES - 1, 1)
        g = pre - b0
        mx_f = g
        mx_b = g
        s = 1
        while s < CHUNK:
            mx_f = jnp.maximum(mx_f, jnp.where(row >= s, pltpu.roll(mx_f, s, 0), -jnp.inf))
            mx_b = jnp.maximum(mx_b, jnp.where(row < CHUNK - s,
                                               pltpu.roll(mx_b, CHUNK - s, 0), -jnp.inf))
            s *= 2
        mx = jnp.where(is_bwd, mx_b, mx_f)
        tmp_ref[0, rows, :] = g
        tmp_ref[1, rows, :] = mx
        tmp_ref[2, rows, :] = b0
        last_ref[0, pl.ds(ci, 1), :] = jnp.where(is_bwd_row, b0[0:1, :], b0[CHUNK - 1:CHUNK, :])
        last_ref[1, pl.ds(ci, 1), :] = jnp.where(is_bwd_row, mx[0:1, :], mx[CHUNK - 1:CHUNK, :])
        return carry

    lax.fori_loop(0, n_chunks, first_pass, 0)

    m_f = jnp.zeros((1, LANES), F32)
    m_b = jnp.zeros((1, LANES), F32)
    for t in range(n_chunks):
        cf, cb = t, n_chunks - 1 - t
        last_ref[2, cf:cf + 1, :] = m_f
        last_ref[3, cb:cb + 1, :] = m_b
        m_f = last_ref[0, cf:cf + 1, :] + jnp.maximum(m_f, last_ref[1, cf:cf + 1, :])
        m_b = last_ref[0, cb:cb + 1, :] + jnp.maximum(m_b, last_ref[1, cb:cb + 1, :])

    def second_pass(ci, carry):
        rows = pl.ds(pl.multiple_of(ci * CHUNK, CHUNK), CHUNK)
        g = tmp_ref[0, rows, :]
        mx = tmp_ref[1, rows, :]
        b0 = tmp_ref[2, rows, :]
        m_st = jnp.where(is_bwd_row, last_ref[3, pl.ds(ci, 1), :], last_ref[2, pl.ds(ci, 1), :])
        mm = jnp.maximum(mx, m_st)
        m_up = jnp.maximum(m_st, last_ref[1, pl.ds(ci, 1), :])
        keep = jnp.broadcast_to(jnp.exp(m_st - m_up), (CHUNK, LANES))
        g1_ref[rows, :] = jnp.where(is_kind0, -mm, pltpu.roll(jnp.exp(-mm - b0), 1, 1))
        g2_ref[rows, :] = jnp.where(is_kind0, jnp.exp(m_st - mm), pltpu.roll(keep, 1, 1))
        g3_ref[rows, :] = jnp.where(is_kind0, g, pltpu.roll(jnp.exp(g - m_up), 1, 1))
        return carry

    lax.fori_loop(0, n_chunks, second_pass, 0)


def _mlpre_kernel(xm_ref, cw_ref, cb_ref, wq_ref, wkt_ref, wk_ref, wv_ref, wg_ref, bg_ref,
                  q_ref, kt_ref, v_ref, xc_ref, g1_ref, g2_ref, g3_ref,
                  gacc_ref, tmp_ref, last_ref):
    j = pl.program_id(1)
    seq_len = xm_ref.shape[0]
    xm = xm_ref[...].astype(F32)
    row = lax.broadcasted_iota(jnp.int32, xm.shape, 0)
    prev = jnp.where(row == 0, 0.0, pltpu.roll(xm, 1, 0))
    nxt = jnp.where(row == seq_len - 1, 0.0, pltpu.roll(xm, seq_len - 1, 0))
    xc = cb_ref[...] + prev * cw_ref[0:1, :] + xm * cw_ref[1:2, :] + nxt * cw_ref[2:3, :]
    xc = xc * _sigmoid(xc)
    xcb = xc.astype(BF16)
    xc_ref[...] = xcb
    q = jnp.dot(xcb, wq_ref[...], preferred_element_type=F32)
    k = jnp.dot(xcb, wk_ref[...], preferred_element_type=F32)
    v = jnp.dot(xm_ref[...], wv_ref[...], preferred_element_type=F32)
    qb = q.astype(BF16)
    kb = k.astype(BF16)
    vb = v.astype(BF16)
    q_ref[...] = (q * (ML_HEAD_DIM ** -0.5)).astype(BF16)
    kt_ref[...] = lax.dot_general(wkt_ref[...], xcb, (((1,), (1,)), ((), ())),
                                  preferred_element_type=F32).astype(BF16)
    v_ref[...] = vb
    part = (jnp.dot(qb, wg_ref[0], preferred_element_type=F32)
            + jnp.dot(kb, wg_ref[1], preferred_element_type=F32)
            + jnp.dot(vb, wg_ref[2], preferred_element_type=F32))

    @pl.when(j == 0)
    def _():
        gacc_ref[...] = part

    @pl.when(j > 0)
    def _():
        gacc_ref[...] += part

    @pl.when(j == pl.num_programs(1) - 1)
    def _():
        _gate_tables(gacc_ref, bg_ref, g1_ref, g2_ref, g3_ref, tmp_ref, last_ref)


def _mlstm_pre(rest, conv_w, conv_b, wq_bd, wk_bd, wv_bd, wg, bg, n_seq, seq_len):
    n_tok = rest.shape[0]
    n_ct = D_MODEL // MXU_DIM
    tile = pl.BlockSpec((seq_len, MXU_DIM), lambda b, j: (b, j))
    wspec = pl.BlockSpec((None, MXU_DIM, MXU_DIM), lambda b, j: (j, 0, 0))
    gspec = pl.BlockSpec((None, seq_len, LANES), lambda b, j: (b, 0, 0))
    act = jax.ShapeDtypeStruct((n_tok, D_MODEL), BF16)
    gate = jax.ShapeDtypeStruct((n_seq, seq_len, LANES), F32)
    return pl.pallas_call(
        _mlpre_kernel,
        grid=(n_seq, n_ct),
        in_specs=[
            tile,
            pl.BlockSpec((3, MXU_DIM), lambda b, j: (0, j)),
            pl.BlockSpec((1, MXU_DIM), lambda b, j: (0, j)),
            wspec, wspec, wspec, wspec,
            pl.BlockSpec((3, MXU_DIM, LANES), lambda b, j: (0, j, 0)),
            pl.BlockSpec((1, LANES), lambda b, j: (0, 0)),
        ],
        out_specs=[tile, pl.BlockSpec((MXU_DIM, seq_len), lambda b, j: (j, b)), tile, tile,
                   gspec, gspec, gspec],
        out_shape=[act, jax.ShapeDtypeStruct((D_MODEL, n_tok), BF16), act, act,
                   gate, gate, gate],
        scratch_shapes=[pltpu.VMEM((seq_len, LANES), F32),
                        pltpu.VMEM((3, seq_len, LANES), F32),
                        pltpu.VMEM((4, seq_len // CHUNK, LANES), F32)],
        compiler_params=_cparams(("parallel", "arbitrary")),
    )(rest, conv_w, conv_b, wq_bd, jnp.swapaxes(wk_bd, 1, 2), wk_bd, wv_bd, wg, bg)


def _mlstm_kernel(q_ref, kt_ref, v_ref, gc1_ref, gc2_ref, gr_ref, o_ref,
                  vx_ref, qk_ref, h_ref):
    seq_len = q_ref.shape[0]
    n_chunks = seq_len // CHUNK
    dh = ML_HEAD_DIM
    r = lax.broadcasted_iota(jnp.int32, (CHUNK, CHUNK), 0)
    c = lax.broadcasted_iota(jnp.int32, (CHUNK, CHUNK), 1)
    vx_ref[:, :dh] = v_ref[...]
    vx_ref[:, dh:] = jnp.ones((seq_len, LANES), vx_ref.dtype)

    def lane_replicated(ref, rows, col):
        return jnp.broadcast_to(ref[rows, col:col + 1], (CHUNK, LANES))

    def chunk_step(direction, ci, c_st):
        mask = (c <= r) if direction == 0 else (c >= r)
        rows = slice(ci * CHUNK, (ci + 1) * CHUNK)
        qc = q_ref[rows, :]
        ktc = kt_ref[:, rows]
        vxc = vx_ref[rows, :]
        neg_mm = lane_replicated(gc1_ref, rows, 2 * direction)
        e_mj = lane_replicated(gc1_ref, rows, 2 * direction + 1)
        inter = lane_replicated(gc2_ref, rows, 2 * direction)
        g_row = gr_ref[2 * direction:2 * direction + 1, rows]
        w_row = gr_ref[2 * direction + 1:2 * direction + 2, rows]
        keep = gc2_ref[ci * CHUNK:ci * CHUNK + 1, 2 * direction + 1:2 * direction + 2]
        if ci in first_visit:
            qk = qk_ref[ci]
        else:
            qk = jnp.dot(qc, ktc, preferred_element_type=F32)
            qk_ref[ci] = qk
        sw = qk * jnp.exp(jnp.where(mask, neg_mm + g_row, -jnp.inf))
        intra = jnp.dot(sw.astype(BF16), vxc, preferred_element_type=F32)
        carry_in = jnp.dot(qc, c_st.astype(BF16), preferred_element_type=F32)
        den = intra[:, dh:] + inter * carry_in[:, dh:]
        rdiv = 1.0 / jnp.maximum(jnp.abs(den), e_mj)
        h = jnp.concatenate(
            [(intra[:, s * LANES:(s + 1) * LANES] + inter * carry_in[:, s * LANES:(s + 1) * LANES])
             * rdiv for s in range(dh // LANES)], axis=1)
        if ci in first_visit:
            h = h_ref[rows, :] + h
            mu = jnp.mean(h, axis=-1, keepdims=True)
            hc = h - mu
            var = jnp.mean(hc * hc, axis=-1, keepdims=True)
            o_ref[rows, :] = (hc * lax.rsqrt(var + LN_EPS)).astype(o_ref.dtype)
        else:
            h_ref[rows, :] = h
            first_visit.add(ci)
        kw = (ktc.astype(F32) * w_row).astype(BF16)
        return keep * c_st + jnp.dot(kw, vxc, preferred_element_type=F32)

    first_visit = set()
    c_f = jnp.zeros((dh, dh + LANES), F32)
    c_b = jnp.zeros((dh, dh + LANES), F32)
    for step in range(n_chunks):
        c_f = chunk_step(0, step, c_f)
        c_b = chunk_step(1, n_chunks - 1 - step, c_b)


def _mlstm(q, kt, v, gc1, gc2, gr3, n_seq, seq_len):
    n_tok = q.shape[0]
    tile = pl.BlockSpec((seq_len, ML_HEAD_DIM), lambda b, h: (b, h))
    col = pl.BlockSpec((None, None, seq_len, 4), lambda b, h: (b, h, 0, 0))
    return pl.pallas_call(
        _mlstm_kernel,
        grid=(n_seq, ML_HEADS),
        in_specs=[
            tile,
            pl.BlockSpec((ML_HEAD_DIM, seq_len), lambda b, h: (h, b)),
            tile, col, col,
            pl.BlockSpec((None, None, 4, seq_len), lambda b, h: (b, h, 0, 0)),
        ],
        out_specs=tile,
        out_shape=jax.ShapeDtypeStruct((n_tok, D_MODEL), BF16),
        scratch_shapes=[
            pltpu.VMEM((seq_len, ML_HEAD_DIM + LANES), BF16),
            pltpu.VMEM((seq_len // CHUNK, CHUNK, CHUNK), F32),
            pltpu.VMEM((seq_len, ML_HEAD_DIM), F32),
        ],
        compiler_params=_cparams(("parallel", "parallel")),
    )(q, kt, v, gc1, gc2, gr3)


def _layer_norm(y, g, b):
    mu = jnp.mean(y, axis=-1, keepdims=True)
    yc = y - mu
    var = jnp.mean(yc * yc, axis=-1, keepdims=True)
    return yc * lax.rsqrt(var + LN_EPS) * g + b


def _route(logits):
    lane = lax.broadcasted_iota(jnp.int32, logits.shape, 1)
    big = jnp.int32(4 * LANES)
    gl = jnp.where(lane < N_GROUPS, logits, NEG_BIG)
    gmax = jnp.max(gl, axis=-1, keepdims=True)
    gsum = jnp.sum(jnp.where(lane < N_GROUPS, jnp.exp(gl - gmax), 0.0), axis=-1, keepdims=True)
    g_sel = jnp.min(jnp.where(gl == gmax, lane, big), axis=-1, keepdims=True)
    g_prob = 1.0 / gsum
    lo = N_GROUPS + EXPERTS_PER_GROUP * g_sel
    in_group = jnp.logical_and(lane >= lo, lane < lo + EXPERTS_PER_GROUP)
    el = jnp.where(in_group, logits, NEG_BIG)
    emax = jnp.max(el, axis=-1, keepdims=True)
    esum = jnp.sum(jnp.where(in_group, jnp.exp(el - emax), 0.0), axis=-1, keepdims=True)
    i1 = jnp.min(jnp.where(el == emax, lane, big), axis=-1, keepdims=True)
    el2 = jnp.where(lane == i1, NEG_BIG, el)
    emax2 = jnp.max(el2, axis=-1, keepdims=True)
    i2 = jnp.min(jnp.where(el2 == emax2, lane, big), axis=-1, keepdims=True)
    p1 = 1.0 / esum
    p2 = jnp.exp(emax2 - emax) / esum
    psum = p1 + p2
    gate1 = g_prob * p1 / psum
    gate2 = g_prob * p2 / psum
    e1 = (i1 - N_GROUPS).astype(F32)
    e2 = (i2 - N_GROUPS).astype(F32)
    return jnp.where(lane == 0, e1, jnp.where(lane == 1, e2, jnp.where(lane == 2, gate1, gate2)))


def _post_kernel(attn_ref, hn_ref, xc_ref, z_ref, ga_ref, gb_ref, x_ref,
                 wpa_ref, wpb_ref, wout_ref, gn_ref, skip_ref, lg_ref, lb_ref, wr_ref, br_ref,
                 x1t_ref, route_ref, *, sub):
    tm = x_ref.shape[0]
    for s in range(tm // sub):
        rows = slice(s * sub, (s + 1) * sub)
        z = z_ref[rows, :].astype(F32)
        ml = ((hn_ref[rows, :].astype(F32) * gn_ref[...]
               + skip_ref[...] * xc_ref[rows, :].astype(F32)) * (z * _sigmoid(z)))
        a_out = jnp.dot(attn_ref[rows, :], wpa_ref[...], preferred_element_type=F32)
        m_out = jnp.dot(ml.astype(BF16), wpb_ref[...], preferred_element_type=F32)
        mixed = (_sigmoid(ga_ref[rows, :].astype(F32)) * a_out
                 + _sigmoid(gb_ref[rows, :].astype(F32)) * m_out)
        y = DN_ALPHA * x_ref[rows, :] + jnp.dot(mixed.astype(BF16), wout_ref[...],
                                               preferred_element_type=F32)
        x1 = _layer_norm(y, lg_ref[...], lb_ref[...])
        for c in range(ROW_CHUNKS):
            x1t_ref[pl.ds(s * sub * ROW_CHUNKS + c, sub, stride=ROW_CHUNKS), :] = (
                x1[:, c * LANES:(c + 1) * LANES])
        logits = br_ref[...] + jnp.dot(x1.astype(BF16), wr_ref[...], preferred_element_type=F32)
        route_ref[rows, :] = _route(logits)


def _post(attn, hn, xc, rest, x, w_pa, w_pb, w_out, gn_g, skip, ln_g, ln_b, w_route, b_route):
    n_tok = x.shape[0]
    sub = 256
    tm = 2 * sub if n_tok % (2 * sub) == 0 else sub
    row = lambda col: pl.BlockSpec((tm, D_MODEL), lambda i, col=col: (i, col))
    full = lambda shape: pl.BlockSpec(shape, lambda i: tuple(0 for _ in shape))
    vec = full((1, D_MODEL))
    return pl.pallas_call(
        functools.partial(_post_kernel, sub=sub),
        grid=(n_tok // tm,),
        in_specs=[row(0), row(0), row(0), row(1), row(2), row(3), row(0),
                  full((D_MODEL, D_MODEL)), full((D_MODEL, D_MODEL)), full((D_MODEL, D_MODEL)),
                  vec, vec, vec, vec,
                  full((D_MODEL, LANES)), full((1, LANES))],
        out_specs=[pl.BlockSpec((tm * ROW_CHUNKS, LANES), lambda i: (i, 0)),
                   pl.BlockSpec((tm, LANES), lambda i: (i, 0))],
        out_shape=[jax.ShapeDtypeStruct((n_tok * ROW_CHUNKS, LANES), F32),
                   jax.ShapeDtypeStruct((n_tok, LANES), F32)],
        compiler_params=_cparams(("parallel",)),
    )(attn, hn, xc, rest, rest, rest, x, w_pa, w_pb, w_out, gn_g, skip, ln_g, ln_b,
      w_route, b_route)


def _moe_kernel(bexp_ref, nused_ref,
                tok_hbm, dst_hbm, gate_ref, x_hbm, wg_ref, wu_ref, wd_ref, y_hbm,
                tok_smem, dst_smem, xbuf, obuf, idx_sem, gat_sem, sca_sem):
    i = pl.program_id(0)
    n_used = nused_ref[0]
    p = i % 2

    def index_copies(blk):
        slot = blk % 3
        return (pltpu.make_async_copy(tok_hbm.at[blk], tok_smem.at[slot], idx_sem.at[0, slot]),
                pltpu.make_async_copy(dst_hbm.at[blk], dst_smem.at[slot], idx_sem.at[1, slot]))

    def start_indices(blk):
        for cp in index_copies(blk):
            cp.start()

    def wait_indices(blk):
        for cp in index_copies(blk):
            cp.wait()

    def tile_rows(r):
        return pl.ds(r * ROW_CHUNKS, ROW_CHUNKS)

    def start_gather(blk, slot, rows=range(MOE_ROWS)):
        islot = blk % 3

        for row in rows:
            t = tok_smem[islot, row // LANES, row % LANES]
            pltpu.make_async_copy(x_hbm.at[t], xbuf.at[slot, tile_rows(row)],
                                  gat_sem.at[slot]).start(priority=row % DMA_PRIORITIES)

    def wait_gather(slot):
        pltpu.make_async_copy(xbuf.at[slot], xbuf.at[slot], gat_sem.at[slot]).wait()

    def start_scatter(blk, slot, rows):
        islot = blk % 3

        for row in rows:
            t = dst_smem[islot, row // LANES, row % LANES]
            pltpu.make_async_copy(obuf.at[slot, tile_rows(row)], y_hbm.at[t],
                                  sca_sem.at[slot]).start(priority=row % DMA_PRIORITIES)

    def wait_scatter(slot):
        pltpu.make_async_copy(obuf.at[slot], obuf.at[slot], sca_sem.at[slot]).wait()

    @pl.when(i == 0)
    def _():
        start_indices(0)
        wait_indices(0)
        start_gather(0, 0)

        @pl.when(1 < n_used)
        def _():
            start_indices(1)

    @pl.when(i + 1 < n_used)
    def _():
        wait_indices(i + 1)

    @pl.when(i + 2 < n_used)
    def _():
        start_indices(i + 2)

    @pl.when(i < n_used)
    def _():
        wait_gather(p)

        @pl.when(i >= 2)
        def _():
            wait_scatter(p)

        nxt = jnp.minimum(i + 1, n_used - 1)
        half = MOE_ROWS // 2
        for h in range(2):
            rows = range(h * half, (h + 1) * half)
            start_gather(nxt, 1 - p, rows)
            xb = jnp.concatenate(
                [xbuf[p, pl.ds(h * half * ROW_CHUNKS + c, half, stride=ROW_CHUNKS), :]
                 for c in range(ROW_CHUNKS)], axis=1).astype(BF16)
            hg = jnp.dot(xb, wg_ref[...], preferred_element_type=F32)
            hu = jnp.dot(xb, wu_ref[...], preferred_element_type=F32)
            hh = (hg * _sigmoid(hg) * hu).astype(BF16)
            out = (jnp.dot(hh, wd_ref[...], preferred_element_type=F32)
                   * gate_ref[h * half:(h + 1) * half, :])
            for c in range(ROW_CHUNKS):
                obuf[p, pl.ds(h * half * ROW_CHUNKS + c, half, stride=ROW_CHUNKS), :] = (
                    out[:, c * LANES:(c + 1) * LANES])
            start_scatter(i, p, rows)

        @pl.when(i == n_used - 1)
        def _():
            wait_scatter(p)
            wait_gather(1 - p)

            @pl.when(i >= 1)
            def _():
                wait_scatter(1 - p)


def _moe(x1t, slot_tok, slot_dst, slot_gate, blk_exp, n_used, we_g, we_u, we_d):
    n_tok = x1t.shape[0] // ROW_CHUNKS
    n_blk = blk_exp.shape[0]
    n_out = TOP_K * n_tok + 2 * MOE_ROWS
    wspec_in = pl.BlockSpec((None, D_MODEL, D_EXPERT), lambda i, be, nu: (be[i], 0, 0))
    wspec_out = pl.BlockSpec((None, D_EXPERT, D_MODEL), lambda i, be, nu: (be[i], 0, 0))
    grid_spec = pltpu.PrefetchScalarGridSpec(
        num_scalar_prefetch=2,
        grid=(n_blk,),
        in_specs=[
            pl.BlockSpec(memory_space=pl.ANY),
            pl.BlockSpec(memory_space=pl.ANY),
            pl.BlockSpec((MOE_ROWS, 1), lambda i, be, nu: (i, 0)),
            pl.BlockSpec(memory_space=pl.ANY),
            wspec_in, wspec_in, wspec_out,
        ],
        out_specs=pl.BlockSpec(memory_space=pl.ANY),
        scratch_shapes=[
            pltpu.SMEM((3, MOE_ROWS // LANES, LANES), jnp.int32),
            pltpu.SMEM((3, MOE_ROWS // LANES, LANES), jnp.int32),
            pltpu.VMEM((2, MOE_ROWS * ROW_CHUNKS, LANES), F32),
            pltpu.VMEM((2, MOE_ROWS * ROW_CHUNKS, LANES), F32),
            pltpu.SemaphoreType.DMA((2, 3)),
            pltpu.SemaphoreType.DMA((2,)),
            pltpu.SemaphoreType.DMA((2,)),
        ],
    )
    y = pl.pallas_call(
        _moe_kernel,
        grid_spec=grid_spec,
        out_shape=jax.ShapeDtypeStruct((n_out, ROW_CHUNKS, LANES), F32),
        compiler_params=_cparams(("arbitrary",)),
    )(blk_exp, n_used, slot_tok.reshape(n_blk, MOE_ROWS // LANES, LANES),
      slot_dst.reshape(n_blk, MOE_ROWS // LANES, LANES), slot_gate.reshape(n_blk * MOE_ROWS, 1),
      x1t.reshape(n_tok, ROW_CHUNKS, LANES), we_g, we_u, we_d)
    return y.reshape(n_out * ROW_CHUNKS, LANES)


def _route_plan(route, n_tok):
    n_assign = n_tok * TOP_K
    eid = route[:, 0:TOP_K].astype(jnp.int32).reshape(n_assign)
    gate = route[:, TOP_K:2 * TOP_K].reshape(n_assign)
    order = jnp.argsort(eid).astype(jnp.int32)
    counts = jnp.zeros((N_EXPERTS,), jnp.int32).at[eid].add(1)
    offsets = jnp.cumsum(counts) - counts
    padded = (counts + MOE_ROWS - 1) // MOE_ROWS * MOE_ROWS
    pad_end = jnp.cumsum(padded)
    pad_start = pad_end - padded
    n_blk = (n_assign + MOE_ROWS - 1) // MOE_ROWS + N_EXPERTS
    blk_start = jnp.arange(n_blk, dtype=jnp.int32) * MOE_ROWS
    blk_exp = jnp.minimum(jnp.sum(pad_end[None, :] <= blk_start[:, None], axis=1),
                          N_EXPERTS - 1).astype(jnp.int32)
    row = jnp.arange(MOE_ROWS, dtype=jnp.int32)[None, :]
    j = (blk_start - pad_start[blk_exp])[:, None] + row
    valid = j < counts[blk_exp][:, None]
    src = jnp.clip(offsets[blk_exp][:, None] + j, 0, n_assign - 1)
    assign = order[src]
    tok = assign // TOP_K
    dummy = TOP_K * n_tok + (jnp.arange(n_blk, dtype=jnp.int32) % 2)[:, None] * MOE_ROWS + row
    slot_tok = jnp.where(valid, tok, 0)
    slot_dst = jnp.where(valid, (assign % TOP_K) * n_tok + tok, dummy)
    slot_gate = jnp.where(valid, gate[assign], 0.0)
    n_used = (pad_end[-1:] // MOE_ROWS).astype(jnp.int32)
    return slot_tok, slot_dst, slot_gate, blk_exp, n_used


def _combine_rows(x_ref, y0_ref, y1_ref, g_ref, b_ref, tm):
    def chunk(ref, c):
        return ref[pl.ds(c, tm, stride=ROW_CHUNKS), :]

    y = jnp.concatenate(
        [DN_ALPHA * chunk(x_ref, c) + (chunk(y0_ref, c) + chunk(y1_ref, c))
         for c in range(ROW_CHUNKS)], axis=1)
    return _layer_norm(y, g_ref[...], b_ref[...])


def _combine_kernel(x_ref, y0_ref, y1_ref, g_ref, b_ref, o_ref):
    o_ref[...] = _combine_rows(x_ref, y0_ref, y1_ref, g_ref, b_ref, o_ref.shape[0])


def _combine(x1t, y, ln_g, ln_b, first_tok, n_out):
    n_tok = x1t.shape[0] // ROW_CHUNKS
    tm = 256
    n_t = n_tok // tm
    t0 = first_tok // tm
    vec = pl.BlockSpec((1, D_MODEL), lambda i: (0, 0))
    tiles = lambda off: pl.BlockSpec((tm * ROW_CHUNKS, LANES), lambda i, off=off: (i + off, 0))
    return pl.pallas_call(
        _combine_kernel,
        grid=(n_out // tm,),
        in_specs=[tiles(t0), tiles(t0), tiles(t0 + n_t), vec, vec],
        out_specs=pl.BlockSpec((tm, D_MODEL), lambda i: (i, 0)),
        out_shape=jax.ShapeDtypeStruct((n_out, D_MODEL), F32),
        compiler_params=_cparams(("parallel",)),
    )(x1t, y, y, ln_g, ln_b)


def _rope_tables(seq_len):
    half = DA_HEAD_DIM // 2
    inv = ROPE_THETA ** (-jnp.arange(half, dtype=F32) * (2.0 / DA_HEAD_DIM))
    ang = jnp.arange(seq_len, dtype=F32)[:, None] * inv[None, :]
    cos = jnp.tile(jnp.cos(ang), (1, LANES // half))
    sin = jnp.tile(jnp.sin(ang), (1, LANES // half))
    lane = jnp.arange(LANES)
    sign = jnp.where((lane % DA_HEAD_DIM) < half, -1.0, 1.0).astype(F32)
    return cos, sin * sign[None, :]


def _block_diag_tiles(w):
    per_tile = MXU_DIM // ML_PROJ_BLOCK
    n_tiles = w.shape[0] // per_tile
    w4 = w.reshape(n_tiles, per_tile, ML_PROJ_BLOCK, ML_PROJ_BLOCK)
    eye = jnp.eye(per_tile, dtype=w.dtype)
    bd = jnp.einsum('jgio,gh->jgiho', w4, eye)
    return bd.reshape(n_tiles, MXU_DIM, MXU_DIM).astype(BF16)


def _gate_perm():
    idx = []
    for h in range(ML_HEADS):
        for d in range(2):
            for kind in range(2):
                idx.append(d * 2 * ML_HEADS + kind * ML_HEADS + h)
    return jnp.array(idx, dtype=jnp.int32)


def _layer(src, n_seq, seq_len, lambda_init, cos, sin, p):
    qk_w = DA_HEADS * 2 * DA_HEAD_DIM
    v_w = DA_HEADS * DA_V_DIM
    x, qkv, rest = _inproj(src, p['w_in'].astype(BF16), cos, sin, seq_len, n_rope=2 * qk_w,
                           n_scaled=qk_w, n_qkv=2 * qk_w + v_w,
                           scale=DA_HEAD_DIM ** -0.5 * math.log2(math.e))
    n_tok = x.shape[0]

    lam = (jnp.exp(jnp.sum(p['lq1'] * p['lk1'])) - jnp.exp(jnp.sum(p['lq2'] * p['lk2']))
           + lambda_init)
    scalars = jnp.stack([lam, jnp.asarray(1.0 - lambda_init, F32)]).astype(F32)
    attn = _attention(qkv, scalars, p['subln_g'].reshape(1, DA_V_DIM), n_seq, seq_len)

    perm = _gate_perm()
    n_gate = 4 * ML_HEADS
    wg = p['w_gate'][:, perm].reshape(3, D_MODEL, n_gate)
    wg = jnp.pad(wg, ((0, 0), (0, 0), (0, LANES - n_gate))).astype(BF16)
    bg = jnp.pad(p['b_gate'][perm], (0, LANES - n_gate)).reshape(1, LANES)
    q, kt, v, xc, g1, g2, g3 = _mlstm_pre(
        rest, p['conv_w'], p['conv_b'].reshape(1, D_MODEL), _block_diag_tiles(p['wq']),
        _block_diag_tiles(p['wk']), _block_diag_tiles(p['wv']), wg, bg, n_seq, seq_len)

    def per_head(t, perm):
        t = t[:, :, :n_gate].reshape(n_seq, seq_len, ML_HEADS, 4)
        return jnp.transpose(t, perm)

    hn = _mlstm(q, kt, v, per_head(g1, (0, 2, 1, 3)), per_head(g2, (0, 2, 1, 3)),
                per_head(g3, (0, 2, 3, 1)), n_seq, seq_len)

    n_route = N_GROUPS + N_EXPERTS
    w_route = jnp.concatenate([p['rg_w'], p['re_w']], axis=1)
    w_route = jnp.pad(w_route, ((0, 0), (0, LANES - n_route)))
    w_route = w_route.astype(BF16)
    b_route = jnp.pad(jnp.concatenate([p['rg_b'], p['re_b']]), (0, LANES - n_route))
    vec = lambda a: a.reshape(1, D_MODEL)
    x1, route = _post(attn, hn, xc, rest, x, p['w_pa'].astype(BF16), p['w_pb'].astype(BF16),
                      p['w_out'].astype(BF16), vec(p['gn_g']), vec(p['skip']), vec(p['ln1_g']),
                      vec(p['ln1_b']), w_route, b_route.reshape(1, LANES))

    plan = _route_plan(route, n_tok)
    y = _moe(x1, *plan, p['we_g'].astype(BF16), p['we_u'].astype(BF16), p['we_d'].astype(BF16))
    return x1, y, vec(p['ln2_g']), vec(p['ln2_b'])


def kernel(x_prompt, x_sample, w_in, da_lambda_q1, da_lambda_k1, da_lambda_q2, da_lambda_k2, da_subln_g, ml_conv_w, ml_conv_b, ml_wq, ml_wk, ml_wv, ml_w_gate, ml_b_gate, ml_skip, ml_gn_g, w_pa, w_pb, w_out, ln1_g, ln1_b, router_group_w, router_group_b, router_expert_w, router_expert_b, w_e_gate, w_e_up, w_e_down, ln2_g, ln2_b):
    n_p, seq_len, d = x_prompt.shape
    n_s = x_sample.shape[0]
    assert x_sample.shape[1] == seq_len and d == D_MODEL and seq_len % CHUNK == 0
    n_seq = n_p + n_s
    cos, sin = _rope_tables(seq_len)
    stacked = dict(w_in=w_in, lq1=da_lambda_q1, lk1=da_lambda_k1, lq2=da_lambda_q2,
                   lk2=da_lambda_k2, subln_g=da_subln_g, conv_w=ml_conv_w, conv_b=ml_conv_b,
                   wq=ml_wq, wk=ml_wk, wv=ml_wv, w_gate=ml_w_gate, b_gate=ml_b_gate,
                   skip=ml_skip, gn_g=ml_gn_g, w_pa=w_pa, w_pb=w_pb, w_out=w_out,
                   ln1_g=ln1_g, ln1_b=ln1_b, rg_w=router_group_w, rg_b=router_group_b,
                   re_w=router_expert_w, re_b=router_expert_b, we_g=w_e_gate, we_u=w_e_up,
                   we_d=w_e_down, ln2_g=ln2_g, ln2_b=ln2_b)
    src = [x_prompt.reshape(n_p * seq_len, d), x_sample.reshape(n_s * seq_len, d)]
    for l in range(w_in.shape[0]):
        lambda_init = 0.8 - 0.6 * math.exp(-0.3 * l)
        src = _layer(src, n_seq, seq_len, lambda_init, cos, sin,
                     {k: a[l] for k, a in stacked.items()})
    y_p = _combine(*src, 0, n_p * seq_len).reshape(n_p, seq_len, d)
    y_s = _combine(*src, n_p * seq_len, n_s * seq_len).reshape(n_s, seq_len, d)
    return (y_p, y_s)
```

```python
import functools
import math

import jax
import jax.numpy as jnp
from jax import lax
from jax.experimental import pallas as pl
from jax.experimental.pallas import tpu as pltpu

F32 = jnp.float32
BF16 = jnp.bfloat16

D_MODEL = 1024
DEPTH = 4
DA_HEADS = 8
DA_HEAD_DIM = 64
DA_V_DIM = 128
ROPE_THETA = 10000.0
ML_HEADS = 4
ML_HEAD_DIM = 256
ML_PROJ_BLOCK = 4
N_GROUPS = 4
EXPERTS_PER_GROUP = 8
N_EXPERTS = 32
TOP_K = 2
D_EXPERT = 512
DN_ALPHA = (2 * DEPTH) ** 0.25
LN_EPS = 1e-5

LANES = 128
MXU_DIM = 256
ROW_CHUNKS = D_MODEL // LANES
CHUNK = 128
MOE_ROWS = 512
DMA_PRIORITIES = 2
VMEM_LIMIT = 56 * 1024 * 1024

NEG_BIG = -1e30


def _cparams(sem):
    return pltpu.CompilerParams(dimension_semantics=sem, vmem_limit_bytes=VMEM_LIMIT)


def _sigmoid(x):
    return 0.5 * jnp.tanh(0.5 * x) + 0.5


def _inproj_kernel(*refs, fused, n_first, n_rope, n_scaled, n_qkv, scale, cw):
    if fused:
        (x1t_ref, y0_ref, y1_ref, r_ref, g_ref, b_ref, w_ref, cos_ref, sin_ref,
         x_out_ref, qkv_ref, rest_ref) = refs
        x = _combine_rows(x1t_ref, y0_ref, y1_ref, r_ref, g_ref, b_ref, qkv_ref.shape[0])
        x_out_ref[...] = x
    else:
        xa_ref, xb_ref, w_ref, cos_ref, sin_ref, x_out_ref, qkv_ref, rest_ref = refs
        x = jnp.where(pl.program_id(0) < n_first, xa_ref[...], xb_ref[...])
        x_out_ref[...] = x
    xb = x.astype(BF16)
    tm = qkv_ref.shape[0]
    n_cols = w_ref.shape[1]
    cos = cos_ref[...]
    sin = sin_ref[...]
    lane = lax.broadcasted_iota(jnp.int32, (tm, LANES), 1)
    first_half = (lane % DA_HEAD_DIM) < (DA_HEAD_DIM // 2)
    for c in range(n_cols // cw):
        acc = jnp.dot(xb, w_ref[:, c * cw:(c + 1) * cw], preferred_element_type=F32)
        for s in range(cw // LANES):
            col = c * cw + s * LANES
            t = acc[:, s * LANES:(s + 1) * LANES]
            if col < n_rope:
                rot = jnp.where(first_half, pltpu.roll(t, LANES - DA_HEAD_DIM // 2, 1),
                                pltpu.roll(t, DA_HEAD_DIM // 2, 1))
                t = t * cos + rot * sin
                if col < n_scaled:
                    t = t * scale
            if col < n_qkv:
                qkv_ref[:, col:col + LANES] = t.astype(qkv_ref.dtype)
            else:
                rest_ref[:, col - n_qkv:col - n_qkv + LANES] = t.astype(rest_ref.dtype)


def _inproj(src, w, cos, sin, seq_len, *, n_rope, n_scaled, n_qkv, scale):
    fused = isinstance(src, tuple)
    d, n_cols = w.shape
    tm = min(512, seq_len)
    n_first = 0 if fused else src[0].shape[0] // tm
    n_tok = src[0].shape[0] // ROW_CHUNKS if fused else src[0].shape[0] + src[1].shape[0]
    n_t = n_tok // tm
    per_seq = seq_len // tm
    kern = functools.partial(_inproj_kernel, fused=fused, n_first=n_first, n_rope=n_rope,
                             n_scaled=n_scaled, n_qkv=n_qkv, scale=scale, cw=512)
    table = pl.BlockSpec((tm, LANES), lambda i: (i % per_seq, 0))
    wspec = pl.BlockSpec((d, n_cols), lambda i: (0, 0), pipeline_mode=pl.Buffered(1))
    rows = lambda n: pl.BlockSpec((tm, n), lambda i: (i, 0))
    outs = [rows(n_qkv), rows(n_cols - n_qkv)]
    out_shapes = [jax.ShapeDtypeStruct((n_tok, n_qkv), BF16),
                  jax.ShapeDtypeStruct((n_tok, n_cols - n_qkv), BF16)]
    if fused:
        x1t, y, route, ln_g, ln_b = src
        tiles = lambda off: pl.BlockSpec((tm * ROW_CHUNKS, LANES), lambda i, off=off: (i + off, 0))
        vec = pl.BlockSpec((1, d), lambda i: (0, 0))
        in_specs = [tiles(0), tiles(0), tiles(n_t), rows(LANES), vec, vec, wspec, table, table]
        operands = (x1t, y, y, route, ln_g, ln_b, w, cos, sin)
    else:
        in_specs = [pl.BlockSpec((tm, d), lambda i: (jnp.minimum(i, n_first - 1), 0)),
                    pl.BlockSpec((tm, d), lambda i: (jnp.maximum(i - n_first, 0), 0)),
                    wspec, table, table]
        operands = (src[0], src[1], w, cos, sin)
    return pl.pallas_call(
        kern, grid=(n_t,),
        in_specs=in_specs,
        out_specs=[rows(d)] + outs,
        out_shape=[jax.ShapeDtypeStruct((n_tok, d), F32)] + out_shapes,
        compiler_params=_cparams(("parallel",)),
    )(*operands)


def _attn_kernel(sc_ref, q_ref, k_ref, v_ref, g_ref, o_ref, vx_ref, *, bq):
    seq_len = q_ref.shape[0]
    lam = sc_ref[0]
    out_scale = sc_ref[1]
    k = k_ref[...]
    vx_ref[:, :DA_V_DIM] = v_ref[...]
    vx_ref[:, DA_V_DIM:] = jnp.ones((seq_len, DA_V_DIM), vx_ref.dtype)
    vx = vx_ref[...]
    g = g_ref[...] * out_scale
    lane = lax.broadcasted_iota(jnp.int32, (bq, LANES), 1)
    is_first = lane < DA_HEAD_DIM
    dn = (((1,), (1,)), ((), ()))

    def softmax_av(qm):
        s = lax.dot_general(qm, k, dn, preferred_element_type=F32)
        p = jnp.exp2(s - jnp.max(s, axis=-1, keepdims=True)).astype(BF16)
        ox = jnp.dot(p, vx, preferred_element_type=F32)
        return ox[:, :DA_V_DIM] / ox[:, DA_V_DIM:]

    for i in range(seq_len // bq):
        rows = slice(i * bq, (i + 1) * bq)
        qb = q_ref[rows, :]
        zero = jnp.zeros_like(qb)
        o = (softmax_av(jnp.where(is_first, qb, zero))
             - lam * softmax_av(jnp.where(is_first, zero, qb)))
        o = o * lax.rsqrt(jnp.mean(o * o, axis=-1, keepdims=True) + LN_EPS)
        o_ref[rows, :] = (o * g).astype(o_ref.dtype)


def _attention(qkv, scalars, subln_g, n_seq, seq_len):
    n_tok = qkv.shape[0]
    bq = min(128, seq_len)
    return pl.pallas_call(
        functools.partial(_attn_kernel, bq=bq),
        grid=(n_seq, DA_HEADS),
        in_specs=[
            pl.BlockSpec(memory_space=pltpu.SMEM),
            pl.BlockSpec((seq_len, LANES), lambda b, h: (b, h)),
            pl.BlockSpec((seq_len, LANES), lambda b, h: (b, DA_HEADS + h)),
            pl.BlockSpec((seq_len, LANES), lambda b, h: (b, 2 * DA_HEADS + h)),
            pl.BlockSpec((1, DA_V_DIM), lambda b, h: (0, 0)),
        ],
        out_specs=pl.BlockSpec((seq_len, DA_V_DIM), lambda b, h: (b, h)),
        out_shape=jax.ShapeDtypeStruct((n_tok, DA_HEADS * DA_V_DIM), BF16),
        scratch_shapes=[pltpu.VMEM((seq_len, 2 * DA_V_DIM), BF16)],
        compiler_params=_cparams(("parallel", "parallel")),
    )(scalars, qkv, qkv, qkv, subln_g)


def _split3(x):
    x1 = x.astype(BF16)
    r1 = x - x1.astype(F32)
    x2 = r1.astype(BF16)
    x3 = (r1 - x2.astype(F32)).astype(BF16)
    return x1, x2, x3


def _gate_tables(gacc_ref, bg_ref, g1_ref, g2_ref, g3_ref, tmp_ref, last_ref):
    seq_len = gacc_ref.shape[0]
    n_chunks = seq_len // CHUNK
    r = lax.broadcasted_iota(jnp.int32, (CHUNK, CHUNK), 0)
    c = lax.broadcasted_iota(jnp.int32, (CHUNK, CHUNK), 1)
    tri = jnp.where(c <= r, 1.0, 0.0).astype(BF16)
    lane = lax.broadcasted_iota(jnp.int32, (CHUNK, LANES), 1)
    row = lax.broadcasted_iota(jnp.int32, (CHUNK, LANES), 0)
    is_kind0 = (lane % 2) == 0
    is_bwd = ((lane // 2) % 2) == 1
    is_bwd_row = is_bwd[0:1, :]
    bg = bg_ref[...]

    def first_pass(ci, carry):
        rows = pl.ds(pl.multiple_of(ci * CHUNK, CHUNK), CHUNK)
        pre = gacc_ref[rows, :] + bg
        lf = jnp.minimum(pre, 0.0) - jnp.log1p(jnp.exp(-jnp.abs(pre)))
        l1, l2, l3 = _split3(lf)
        pref = (jnp.dot(tri, l1, preferred_element_type=F32)
                + jnp.dot(tri, l2, preferred_element_type=F32)
                + jnp.dot(tri, l3, preferred_element_type=F32))
        suff = pref[CHUNK - 1:CHUNK, :] - pref + lf
        cum = jnp.where(is_bwd, suff, pref)
        b0 = pltpu.roll(cum, LANES - 1, 1)
        g = pre - b0
        mx_f = g
        mx_b = g
        s = 1
        while s < CHUNK:
            mx_f = jnp.maximum(mx_f, jnp.where(row >= s, pltpu.roll(mx_f, s, 0), -jnp.inf))
            mx_b = jnp.maximum(mx_b, jnp.where(row < CHUNK - s,
                                               pltpu.roll(mx_b, CHUNK - s, 0), -jnp.inf))
            s *= 2
        mx = jnp.where(is_bwd, mx_b, mx_f)
        tmp_ref[0, rows, :] = g
        tmp_ref[1, rows, :] = mx
        tmp_ref[2, rows, :] = b0
        last_ref[0, pl.ds(ci, 1), :] = jnp.where(is_bwd_row, b0[0:1, :], b0[CHUNK - 1:CHUNK, :])
        last_ref[1, pl.ds(ci, 1), :] = jnp.where(is_bwd_row, mx[0:1, :], mx[CHUNK - 1:CHUNK, :])
        return carry

    lax.fori_loop(0, n_chunks, first_pass, 0)

    m_f = jnp.zeros((1, LANES), F32)
    m_b = jnp.zeros((1, LANES), F32)
    for t in range(n_chunks):
        cf, cb = t, n_chunks - 1 - t
        last_ref[2, cf:cf + 1, :] = m_f
        last_ref[3, cb:cb + 1, :] = m_b
        m_f = last_ref[0, cf:cf + 1, :] + jnp.maximum(m_f, last_ref[1, cf:cf + 1, :])
        m_b = last_ref[0, cb:cb + 1, :] + jnp.maximum(m_b, last_ref[1, cb:cb + 1, :])

    def second_pass(ci, carry):
        rows = pl.ds(pl.multiple_of(ci * CHUNK, CHUNK), CHUNK)
        g = tmp_ref[0, rows, :]
        mx = tmp_ref[1, rows, :]
        b0 = tmp_ref[2, rows, :]
        m_st = jnp.where(is_bwd_row, last_ref[3, pl.ds(ci, 1), :], last_ref[2, pl.ds(ci, 1), :])
        mm = jnp.maximum(mx, m_st)
        m_up = jnp.maximum(m_st, last_ref[1, pl.ds(ci, 1), :])
        keep = jnp.broadcast_to(jnp.exp(m_st - m_up), (CHUNK, LANES))
        g1_ref[rows, :] = jnp.where(is_kind0, -mm, pltpu.roll(jnp.exp(-mm - b0), 1, 1))
        g2_ref[rows, :] = jnp.where(is_kind0, jnp.exp(m_st - mm), pltpu.roll(keep, 1, 1))
        g3_ref[rows, :] = jnp.where(is_kind0, g, pltpu.roll(jnp.exp(g - m_up), 1, 1))
        return carry

    lax.fori_loop(0, n_chunks, second_pass, 0)


def _mlpre_kernel(xm_ref, cw_ref, cb_ref, wq_ref, wkt_ref, wk_ref, wv_ref, wg_ref, bg_ref,
                  q_ref, kt_ref, v_ref, xc_ref, g1_ref, g2_ref, g3_ref,
                  gacc_ref, tmp_ref, last_ref):
    j = pl.program_id(1)
    seq_len = xm_ref.shape[0]
    xm = xm_ref[...].astype(F32)
    row = lax.broadcasted_iota(jnp.int32, xm.shape, 0)
    prev = jnp.where(row == 0, 0.0, pltpu.roll(xm, 1, 0))
    nxt = jnp.where(row == seq_len - 1, 0.0, pltpu.roll(xm, seq_len - 1, 0))
    xc = cb_ref[...] + prev * cw_ref[0:1, :] + xm * cw_ref[1:2, :] + nxt * cw_ref[2:3, :]
    xc = xc * _sigmoid(xc)
    xcb = xc.astype(BF16)
    xc_ref[...] = xcb
    q = jnp.dot(xcb, wq_ref[...], preferred_element_type=F32)
    k = jnp.dot(xcb, wk_ref[...], preferred_element_type=F32)
    v = jnp.dot(xm_ref[...], wv_ref[...], preferred_element_type=F32)
    qb = q.astype(BF16)
    kb = k.astype(BF16)
    vb = v.astype(BF16)
    q_ref[...] = (q * (ML_HEAD_DIM ** -0.5)).astype(BF16)
    kt_ref[...] = lax.dot_general(wkt_ref[...], xcb, (((1,), (1,)), ((), ())),
                                  preferred_element_type=F32).astype(BF16)
    v_ref[...] = vb
    part = (jnp.dot(qb, wg_ref[0], preferred_element_type=F32)
            + jnp.dot(kb, wg_ref[1], preferred_element_type=F32)
            + jnp.dot(vb, wg_ref[2], preferred_element_type=F32))

    @pl.when(j == 0)
    def _():
        gacc_ref[...] = part

    @pl.when(j > 0)
    def _():
        gacc_ref[...] += part

    @pl.when(j == pl.num_programs(1) - 1)
    def _():
        _gate_tables(gacc_ref, bg_ref, g1_ref, g2_ref, g3_ref, tmp_ref, last_ref)


def _mlstm_pre(rest, conv_w, conv_b, wq_bd, wk_bd, wv_bd, wg, bg, n_seq, seq_len):
    n_tok = rest.shape[0]
    n_ct = D_MODEL // MXU_DIM
    tile = pl.BlockSpec((seq_len, MXU_DIM), lambda b, j: (b, j))
    wspec = pl.BlockSpec((None, MXU_DIM, MXU_DIM), lambda b, j: (j, 0, 0))
    gspec = pl.BlockSpec((None, seq_len, LANES), lambda b, j: (b, 0, 0))
    act = jax.ShapeDtypeStruct((n_tok, D_MODEL), BF16)
    gate = jax.ShapeDtypeStruct((n_seq, seq_len, LANES), F32)
    return pl.pallas_call(
        _mlpre_kernel,
        grid=(n_seq, n_ct),
        in_specs=[
            tile,
            pl.BlockSpec((3, MXU_DIM), lambda b, j: (0, j)),
            pl.BlockSpec((1, MXU_DIM), lambda b, j: (0, j)),
            wspec, wspec, wspec, wspec,
            pl.BlockSpec((3, MXU_DIM, LANES), lambda b, j: (0, j, 0)),
            pl.BlockSpec((1, LANES), lambda b, j: (0, 0)),
        ],
        out_specs=[tile, pl.BlockSpec((MXU_DIM, seq_len), lambda b, j: (j, b)), tile, tile,
                   gspec, gspec, gspec],
        out_shape=[act, jax.ShapeDtypeStruct((D_MODEL, n_tok), BF16), act, act,
                   gate, gate, gate],
        scratch_shapes=[pltpu.VMEM((seq_len, LANES), F32),
                        pltpu.VMEM((3, seq_len, LANES), F32),
                        pltpu.VMEM((4, seq_len // CHUNK, LANES), F32)],
        compiler_params=_cparams(("parallel", "arbitrary")),
    )(rest, conv_w, conv_b, wq_bd, jnp.swapaxes(wk_bd, 1, 2), wk_bd, wv_bd, wg, bg)


def _mlstm_kernel(q_ref, kt_ref, v_ref, gc1_ref, gc2_ref, gr_ref, o_ref,
                  vx_ref, qk_ref, h_ref):
    seq_len = q_ref.shape[0]
    n_chunks = seq_len // CHUNK
    dh = ML_HEAD_DIM
    r = lax.broadcasted_iota(jnp.int32, (CHUNK, CHUNK), 0)
    c = lax.broadcasted_iota(jnp.int32, (CHUNK, CHUNK), 1)
    vx_ref[:, :dh] = v_ref[...]
    vx_ref[:, dh:] = jnp.ones((seq_len, LANES), vx_ref.dtype)

    def lane_replicated(ref, rows, col):
        return jnp.broadcast_to(ref[rows, col:col + 1], (CHUNK, LANES))

    def chunk_step(direction, ci, c_st):
        mask = (c <= r) if direction == 0 else (c >= r)
        rows = slice(ci * CHUNK, (ci + 1) * CHUNK)
        qc = q_ref[rows, :]
        ktc = kt_ref[:, rows]
        vxc = vx_ref[rows, :]
        neg_mm = lane_replicated(gc1_ref, rows, 2 * direction)
        e_mj = lane_replicated(gc1_ref, rows, 2 * direction + 1)
        inter = lane_replicated(gc2_ref, rows, 2 * direction)
        g_row = gr_ref[2 * direction:2 * direction + 1, rows]
        w_row = gr_ref[2 * direction + 1:2 * direction + 2, rows]
        keep = gc2_ref[ci * CHUNK:ci * CHUNK + 1, 2 * direction + 1:2 * direction + 2]
        if ci in first_visit:
            qk = qk_ref[ci]
        else:
            qk = jnp.dot(qc, ktc, preferred_element_type=F32)
            qk_ref[ci] = qk
        sw = qk * jnp.exp(jnp.where(mask, neg_mm + g_row, -jnp.inf))
        intra = jnp.dot(sw.astype(BF16), vxc, preferred_element_type=F32)
        carry_in = jnp.dot(qc, c_st.astype(BF16), preferred_element_type=F32)
        den = intra[:, dh:] + inter * carry_in[:, dh:]
        rdiv = 1.0 / jnp.maximum(jnp.abs(den), e_mj)
        h = jnp.concatenate(
            [(intra[:, s * LANES:(s + 1) * LANES] + inter * carry_in[:, s * LANES:(s + 1) * LANES])
             * rdiv for s in range(dh // LANES)], axis=1)
        if ci in first_visit:
            h = h_ref[rows, :] + h
            mu = jnp.mean(h, axis=-1, keepdims=True)
            hc = h - mu
            var = jnp.mean(hc * hc, axis=-1, keepdims=True)
            o_ref[rows, :] = (hc * lax.rsqrt(var + LN_EPS)).astype(o_ref.dtype)
        else:
            h_ref[rows, :] = h
            first_visit.add(ci)
        kw = (ktc.astype(F32) * w_row).astype(BF16)
        return keep * c_st + jnp.dot(kw, vxc, preferred_element_type=F32)

    first_visit = set()
    c_f = jnp.zeros((dh, dh + LANES), F32)
    c_b = jnp.zeros((dh, dh + LANES), F32)
    for step in range(n_chunks):
        c_f = chunk_step(0, step, c_f)
        c_b = chunk_step(1, n_chunks - 1 - step, c_b)


def _mlstm(q, kt, v, gc1, gc2, gr3, n_seq, seq_len):
    n_tok = q.shape[0]
    tile = pl.BlockSpec((seq_len, ML_HEAD_DIM), lambda b, h: (b, h))
    col = pl.BlockSpec((None, None, seq_len, 4), lambda b, h: (b, h, 0, 0))
    return pl.pallas_call(
        _mlstm_kernel,
        grid=(n_seq, ML_HEADS),
        in_specs=[
            tile,
            pl.BlockSpec((ML_HEAD_DIM, seq_len), lambda b, h: (h, b)),
            tile, col, col,
            pl.BlockSpec((None, None, 4, seq_len), lambda b, h: (b, h, 0, 0)),
        ],
        out_specs=tile,
        out_shape=jax.ShapeDtypeStruct((n_tok, D_MODEL), BF16),
        scratch_shapes=[
            pltpu.VMEM((seq_len, ML_HEAD_DIM + LANES), BF16),
            pltpu.VMEM((seq_len // CHUNK, CHUNK, CHUNK), F32),
            pltpu.VMEM((seq_len, ML_HEAD_DIM), F32),
        ],
        compiler_params=_cparams(("parallel", "parallel")),
    )(q, kt, v, gc1, gc2, gr3)


def _layer_norm(y, g, b):
    mu = jnp.mean(y, axis=-1, keepdims=True)
    yc = y - mu
    var = jnp.mean(yc * yc, axis=-1, keepdims=True)
    return yc * lax.rsqrt(var + LN_EPS) * g + b


def _route(logits):
    lane = lax.broadcasted_iota(jnp.int32, logits.shape, 1)
    big = jnp.int32(4 * LANES)
    gl = jnp.where(lane < N_GROUPS, logits, NEG_BIG)
    gmax = jnp.max(gl, axis=-1, keepdims=True)
    gsum = jnp.sum(jnp.where(lane < N_GROUPS, jnp.exp(gl - gmax), 0.0), axis=-1, keepdims=True)
    g_sel = jnp.min(jnp.where(gl == gmax, lane, big), axis=-1, keepdims=True)
    g_prob = 1.0 / gsum
    lo = N_GROUPS + EXPERTS_PER_GROUP * g_sel
    in_group = jnp.logical_and(lane >= lo, lane < lo + EXPERTS_PER_GROUP)
    el = jnp.where(in_group, logits, NEG_BIG)
    emax = jnp.max(el, axis=-1, keepdims=True)
    esum = jnp.sum(jnp.where(in_group, jnp.exp(el - emax), 0.0), axis=-1, keepdims=True)
    i1 = jnp.min(jnp.where(el == emax, lane, big), axis=-1, keepdims=True)
    el2 = jnp.where(lane == i1, NEG_BIG, el)
    emax2 = jnp.max(el2, axis=-1, keepdims=True)
    i2 = jnp.min(jnp.where(el2 == emax2, lane, big), axis=-1, keepdims=True)
    p1 = 1.0 / esum
    p2 = jnp.exp(emax2 - emax) / esum
    psum = p1 + p2
    gate1 = g_prob * p1 / psum
    gate2 = g_prob * p2 / psum
    e1 = (i1 - N_GROUPS).astype(F32)
    e2 = (i2 - N_GROUPS).astype(F32)
    return jnp.where(lane == 0, e1, jnp.where(lane == 1, e2, jnp.where(lane == 2, gate1, gate2)))


def _post_kernel(attn_ref, hn_ref, xc_ref, z_ref, ga_ref, gb_ref, x_ref,
                 wpa_ref, wpb_ref, wout_ref, gn_ref, skip_ref, lg_ref, lb_ref, wr_ref, br_ref,
                 x1t_ref, route_ref, *, sub):
    tm = x_ref.shape[0]
    for s in range(tm // sub):
        rows = slice(s * sub, (s + 1) * sub)
        z = z_ref[rows, :].astype(F32)
        ml = ((hn_ref[rows, :].astype(F32) * gn_ref[...]
               + skip_ref[...] * xc_ref[rows, :].astype(F32)) * (z * _sigmoid(z)))
        a_out = jnp.dot(attn_ref[rows, :], wpa_ref[...], preferred_element_type=F32)
        m_out = jnp.dot(ml.astype(BF16), wpb_ref[...], preferred_element_type=F32)
        mixed = (_sigmoid(ga_ref[rows, :].astype(F32)) * a_out
                 + _sigmoid(gb_ref[rows, :].astype(F32)) * m_out)
        y = DN_ALPHA * x_ref[rows, :] + jnp.dot(mixed.astype(BF16), wout_ref[...],
                                               preferred_element_type=F32)
        x1 = _layer_norm(y, lg_ref[...], lb_ref[...])
        for c in range(ROW_CHUNKS):
            x1t_ref[pl.ds(s * sub * ROW_CHUNKS + c, sub, stride=ROW_CHUNKS), :] = (
                x1[:, c * LANES:(c + 1) * LANES])
        logits = br_ref[...] + jnp.dot(x1.astype(BF16), wr_ref[...], preferred_element_type=F32)
        route_ref[rows, :] = _route(logits)


def _post(attn, hn, xc, rest, x, w_pa, w_pb, w_out, gn_g, skip, ln_g, ln_b, w_route, b_route):
    n_tok = x.shape[0]
    sub = 256
    tm = 2 * sub if n_tok % (2 * sub) == 0 else sub
    row = lambda col: pl.BlockSpec((tm, D_MODEL), lambda i, col=col: (i, col))
    full = lambda shape: pl.BlockSpec(shape, lambda i: tuple(0 for _ in shape))
    vec = full((1, D_MODEL))
    return pl.pallas_call(
        functools.partial(_post_kernel, sub=sub),
        grid=(n_tok // tm,),
        in_specs=[row(0), row(0), row(0), row(1), row(2), row(3), row(0),
                  full((D_MODEL, D_MODEL)), full((D_MODEL, D_MODEL)), full((D_MODEL, D_MODEL)),
                  vec, vec, vec, vec,
                  full((D_MODEL, LANES)), full((1, LANES))],
        out_specs=[pl.BlockSpec((tm * ROW_CHUNKS, LANES), lambda i: (i, 0)),
                   pl.BlockSpec((tm, LANES), lambda i: (i, 0))],
        out_shape=[jax.ShapeDtypeStruct((n_tok * ROW_CHUNKS, LANES), F32),
                   jax.ShapeDtypeStruct((n_tok, LANES), F32)],
        compiler_params=_cparams(("parallel",)),
    )(attn, hn, xc, rest, rest, rest, x, w_pa, w_pb, w_out, gn_g, skip, ln_g, ln_b,
      w_route, b_route)


def _moe_kernel(bexp_ref, nused_ref,
                tok_hbm, dst_hbm, x_hbm, wg_ref, wu_ref, wd_ref, y_hbm,
                tok_smem, dst_smem, xbuf, obuf, idx_sem, gat_sem, sca_sem):
    i = pl.program_id(0)
    n_used = nused_ref[0]
    p = i % 2

    def index_copies(blk):
        slot = blk % 3
        return (pltpu.make_async_copy(tok_hbm.at[blk], tok_smem.at[slot], idx_sem.at[0, slot]),
                pltpu.make_async_copy(dst_hbm.at[blk], dst_smem.at[slot], idx_sem.at[1, slot]))

    def start_indices(blk):
        for cp in index_copies(blk):
            cp.start()

    def wait_indices(blk):
        for cp in index_copies(blk):
            cp.wait()

    def tile_rows(r):
        return pl.ds(r * ROW_CHUNKS, ROW_CHUNKS)

    def start_gather(blk, slot, rows=range(MOE_ROWS)):
        islot = blk % 3

        for row in rows:
            t = tok_smem[islot, row // LANES, row % LANES]
            pltpu.make_async_copy(x_hbm.at[t], xbuf.at[slot, tile_rows(row)],
                                  gat_sem.at[slot]).start(priority=row % DMA_PRIORITIES)

    def wait_gather(slot):
        pltpu.make_async_copy(xbuf.at[slot], xbuf.at[slot], gat_sem.at[slot]).wait()

    def start_scatter(blk, slot, rows):
        islot = blk % 3

        for row in rows:
            t = dst_smem[islot, row // LANES, row % LANES]
            pltpu.make_async_copy(obuf.at[slot, tile_rows(row)], y_hbm.at[t],
                                  sca_sem.at[slot]).start(priority=row % DMA_PRIORITIES)

    def wait_scatter(slot):
        pltpu.make_async_copy(obuf.at[slot], obuf.at[slot], sca_sem.at[slot]).wait()

    @pl.when(i == 0)
    def _():
        start_indices(0)
        wait_indices(0)
        start_gather(0, 0)

        @pl.when(1 < n_used)
        def _():
            start_indices(1)

    @pl.when(i + 1 < n_used)
    def _():
        wait_indices(i + 1)

    @pl.when(i + 2 < n_used)
    def _():
        start_indices(i + 2)

    @pl.when(i < n_used)
    def _():
        wait_gather(p)

        @pl.when(i >= 2)
        def _():
            wait_scatter(p)

        nxt = jnp.minimum(i + 1, n_used - 1)
        half = MOE_ROWS // 2
        for h in range(2):
            rows = range(h * half, (h + 1) * half)
            start_gather(nxt, 1 - p, rows)
            xb = jnp.concatenate(
                [xbuf[p, pl.ds(h * half * ROW_CHUNKS + c, half, stride=ROW_CHUNKS), :]
                 for c in range(ROW_CHUNKS)], axis=1).astype(BF16)
            hg = jnp.dot(xb, wg_ref[...], preferred_element_type=F32)
            hu = jnp.dot(xb, wu_ref[...], preferred_element_type=F32)
            hh = (hg * _sigmoid(hg) * hu).astype(BF16)
            out = jnp.dot(hh, wd_ref[...], preferred_element_type=F32)
            for c in range(ROW_CHUNKS):
                obuf[p, pl.ds(h * half * ROW_CHUNKS + c, half, stride=ROW_CHUNKS), :] = (
                    out[:, c * LANES:(c + 1) * LANES])
            start_scatter(i, p, rows)

        @pl.when(i == n_used - 1)
        def _():
            wait_scatter(p)
            wait_gather(1 - p)

            @pl.when(i >= 1)
            def _():
                wait_scatter(1 - p)


def _moe(x1t, slot_tok, slot_dst, blk_exp, n_used, we_g, we_u, we_d):
    n_tok = x1t.shape[0] // ROW_CHUNKS
    n_blk = blk_exp.shape[0]
    n_out = TOP_K * n_tok + 2 * MOE_ROWS
    wspec_in = pl.BlockSpec((None, D_MODEL, D_EXPERT), lambda i, be, nu: (be[i], 0, 0))
    wspec_out = pl.BlockSpec((None, D_EXPERT, D_MODEL), lambda i, be, nu: (be[i], 0, 0))
    grid_spec = pltpu.PrefetchScalarGridSpec(
        num_scalar_prefetch=2,
        grid=(n_blk,),
        in_specs=[
            pl.BlockSpec(memory_space=pl.ANY),
            pl.BlockSpec(memory_space=pl.ANY),
            pl.BlockSpec(memory_space=pl.ANY),
            wspec_in, wspec_in, wspec_out,
        ],
        out_specs=pl.BlockSpec(memory_space=pl.ANY),
        scratch_shapes=[
            pltpu.SMEM((3, MOE_ROWS // LANES, LANES), jnp.int32),
            pltpu.SMEM((3, MOE_ROWS // LANES, LANES), jnp.int32),
            pltpu.VMEM((2, MOE_ROWS * ROW_CHUNKS, LANES), F32),
            pltpu.VMEM((2, MOE_ROWS * ROW_CHUNKS, LANES), F32),
            pltpu.SemaphoreType.DMA((2, 3)),
            pltpu.SemaphoreType.DMA((2,)),
            pltpu.SemaphoreType.DMA((2,)),
        ],
    )
    y = pl.pallas_call(
        _moe_kernel,
        grid_spec=grid_spec,
        out_shape=jax.ShapeDtypeStruct((n_out, ROW_CHUNKS, LANES), F32),
        compiler_params=_cparams(("arbitrary",)),
    )(blk_exp, n_used, slot_tok.reshape(n_blk, MOE_ROWS // LANES, LANES),
      slot_dst.reshape(n_blk, MOE_ROWS // LANES, LANES),
      x1t.reshape(n_tok, ROW_CHUNKS, LANES), we_g, we_u, we_d)
    return y.reshape(n_out * ROW_CHUNKS, LANES)


def _route_plan(route, n_tok):
    n_assign = n_tok * TOP_K
    eid = route[:, 0:TOP_K].astype(jnp.int32).reshape(n_assign)
    order = jnp.argsort(eid).astype(jnp.int32)
    counts = jnp.zeros((N_EXPERTS,), jnp.int32).at[eid].add(1)
    offsets = jnp.cumsum(counts) - counts
    padded = (counts + MOE_ROWS - 1) // MOE_ROWS * MOE_ROWS
    pad_end = jnp.cumsum(padded)
    pad_start = pad_end - padded
    n_blk = (n_assign + MOE_ROWS - 1) // MOE_ROWS + N_EXPERTS
    blk_start = jnp.arange(n_blk, dtype=jnp.int32) * MOE_ROWS
    blk_exp = jnp.minimum(jnp.sum(pad_end[None, :] <= blk_start[:, None], axis=1),
                          N_EXPERTS - 1).astype(jnp.int32)
    row = jnp.arange(MOE_ROWS, dtype=jnp.int32)[None, :]
    j = (blk_start - pad_start[blk_exp])[:, None] + row
    valid = j < counts[blk_exp][:, None]
    src = jnp.clip(offsets[blk_exp][:, None] + j, 0, n_assign - 1)
    assign = order[src]
    tok = assign // TOP_K
    dummy = TOP_K * n_tok + (jnp.arange(n_blk, dtype=jnp.int32) % 2)[:, None] * MOE_ROWS + row
    slot_tok = jnp.where(valid, tok, 0)
    slot_dst = jnp.where(valid, (assign % TOP_K) * n_tok + tok, dummy)
    n_used = (pad_end[-1:] // MOE_ROWS).astype(jnp.int32)
    return slot_tok, slot_dst, blk_exp, n_used


def _combine_rows(x_ref, y0_ref, y1_ref, r_ref, g_ref, b_ref, tm):
    def chunk(ref, c):
        return ref[pl.ds(c, tm, stride=ROW_CHUNKS), :]

    route = r_ref[...]
    gate0 = jnp.broadcast_to(route[:, TOP_K:TOP_K + 1], (tm, LANES))
    gate1 = jnp.broadcast_to(route[:, TOP_K + 1:TOP_K + 2], (tm, LANES))
    y = jnp.concatenate(
        [DN_ALPHA * chunk(x_ref, c) + (gate0 * chunk(y0_ref, c) + gate1 * chunk(y1_ref, c))
         for c in range(ROW_CHUNKS)], axis=1)
    return _layer_norm(y, g_ref[...], b_ref[...])


def _combine_kernel(x_ref, y0_ref, y1_ref, r_ref, g_ref, b_ref, o_ref):
    o_ref[...] = _combine_rows(x_ref, y0_ref, y1_ref, r_ref, g_ref, b_ref, o_ref.shape[0])


def _combine(x1t, y, route, ln_g, ln_b, first_tok, n_out):
    n_tok = x1t.shape[0] // ROW_CHUNKS
    tm = 256
    n_t = n_tok // tm
    t0 = first_tok // tm
    vec = pl.BlockSpec((1, D_MODEL), lambda i: (0, 0))
    tiles = lambda off: pl.BlockSpec((tm * ROW_CHUNKS, LANES), lambda i, off=off: (i + off, 0))
    return pl.pallas_call(
        _combine_kernel,
        grid=(n_out // tm,),
        in_specs=[tiles(t0), tiles(t0), tiles(t0 + n_t),
                  pl.BlockSpec((tm, LANES), lambda i: (i + t0, 0)), vec, vec],
        out_specs=pl.BlockSpec((tm, D_MODEL), lambda i: (i, 0)),
        out_shape=jax.ShapeDtypeStruct((n_out, D_MODEL), F32),
        compiler_params=_cparams(("parallel",)),
    )(x1t, y, y, route, ln_g, ln_b)


def _rope_tables(seq_len):
    half = DA_HEAD_DIM // 2
    inv = ROPE_THETA ** (-jnp.arange(half, dtype=F32) * (2.0 / DA_HEAD_DIM))
    ang = jnp.arange(seq_len, dtype=F32)[:, None] * inv[None, :]
    cos = jnp.tile(jnp.cos(ang), (1, LANES // half))
    sin = jnp.tile(jnp.sin(ang), (1, LANES // half))
    lane = jnp.arange(LANES)
    sign = jnp.where((lane % DA_HEAD_DIM) < half, -1.0, 1.0).astype(F32)
    return cos, sin * sign[None, :]


def _block_diag_tiles(w):
    per_tile = MXU_DIM // ML_PROJ_BLOCK
    n_tiles = w.shape[0] // per_tile
    w4 = w.reshape(n_tiles, per_tile, ML_PROJ_BLOCK, ML_PROJ_BLOCK)
    eye = jnp.eye(per_tile, dtype=w.dtype)
    bd = jnp.einsum('jgio,gh->jgiho', w4, eye)
    return bd.reshape(n_tiles, MXU_DIM, MXU_DIM).astype(BF16)


def _gate_perm():
    idx = []
    for h in range(ML_HEADS):
        for d in range(2):
            for kind in range(2):
                idx.append(d * 2 * ML_HEADS + kind * ML_HEADS + h)
    return jnp.array(idx, dtype=jnp.int32)


def _layer(src, n_seq, seq_len, lambda_init, cos, sin, p):
    qk_w = DA_HEADS * 2 * DA_HEAD_DIM
    v_w = DA_HEADS * DA_V_DIM
    x, qkv, rest = _inproj(src, p['w_in'].astype(BF16), cos, sin, seq_len, n_rope=2 * qk_w,
                           n_scaled=qk_w, n_qkv=2 * qk_w + v_w,
                           scale=DA_HEAD_DIM ** -0.5 * math.log2(math.e))
    n_tok = x.shape[0]

    lam = (jnp.exp(jnp.sum(p['lq1'] * p['lk1'])) - jnp.exp(jnp.sum(p['lq2'] * p['lk2']))
           + lambda_init)
    scalars = jnp.stack([lam, jnp.asarray(1.0 - lambda_init, F32)]).astype(F32)
    attn = _attention(qkv, scalars, p['subln_g'].reshape(1, DA_V_DIM), n_seq, seq_len)

    perm = _gate_perm()
    n_gate = 4 * ML_HEADS
    wg = p['w_gate'][:, perm].reshape(3, D_MODEL, n_gate)
    wg = jnp.pad(wg, ((0, 0), (0, 0), (0, LANES - n_gate))).astype(BF16)
    bg = jnp.pad(p['b_gate'][perm], (0, LANES - n_gate)).reshape(1, LANES)
    q, kt, v, xc, g1, g2, g3 = _mlstm_pre(
        rest, p['conv_w'], p['conv_b'].reshape(1, D_MODEL), _block_diag_tiles(p['wq']),
        _block_diag_tiles(p['wk']), _block_diag_tiles(p['wv']), wg, bg, n_seq, seq_len)

    def per_head(t, perm):
        t = t[:, :, :n_gate].reshape(n_seq, seq_len, ML_HEADS, 4)
        return jnp.transpose(t, perm)

    hn = _mlstm(q, kt, v, per_head(g1, (0, 2, 1, 3)), per_head(g2, (0, 2, 1, 3)),
                per_head(g3, (0, 2, 3, 1)), n_seq, seq_len)

    n_route = N_GROUPS + N_EXPERTS
    w_route = jnp.concatenate([p['rg_w'], p['re_w']], axis=1)
    w_route = jnp.pad(w_route, ((0, 0), (0, LANES - n_route)))
    w_route = w_route.astype(BF16)
    b_route = jnp.pad(jnp.concatenate([p['rg_b'], p['re_b']]), (0, LANES - n_route))
    vec = lambda a: a.reshape(1, D_MODEL)
    x1, route = _post(attn, hn, xc, rest, x, p['w_pa'].astype(BF16), p['w_pb'].astype(BF16),
                      p['w_out'].astype(BF16), vec(p['gn_g']), vec(p['skip']), vec(p['ln1_g']),
                      vec(p['ln1_b']), w_route, b_route.reshape(1, LANES))

    plan = _route_plan(route, n_tok)
    y = _moe(x1, *plan, p['we_g'].astype(BF16), p['we_u'].astype(BF16), p['we_d'].astype(BF16))
    return x1, y, route, vec(p['ln2_g']), vec(p['ln2_b'])


def kernel(x_prompt, x_sample, w_in, da_lambda_q1, da_lambda_k1, da_lambda_q2, da_lambda_k2, da_subln_g, ml_conv_w, ml_conv_b, ml_wq, ml_wk, ml_wv, ml_w_gate, ml_b_gate, ml_skip, ml_gn_g, w_pa, w_pb, w_out, ln1_g, ln1_b, router_group_w, router_group_b, router_expert_w, router_expert_b, w_e_gate, w_e_up, w_e_down, ln2_g, ln2_b):
    n_p, seq_len, d = x_prompt.shape
    n_s = x_sample.shape[0]
    assert x_sample.shape[1] == seq_len and d == D_MODEL and seq_len % CHUNK == 0
    n_seq = n_p + n_s
    cos, sin = _rope_tables(seq_len)
    stacked = dict(w_in=w_in, lq1=da_lambda_q1, lk1=da_lambda_k1, lq2=da_lambda_q2,
                   lk2=da_lambda_k2, subln_g=da_subln_g, conv_w=ml_conv_w, conv_b=ml_conv_b,
                   wq=ml_wq, wk=ml_wk, wv=ml_wv, w_gate=ml_w_gate, b_gate=ml_b_gate,
                   skip=ml_skip, gn_g=ml_gn_g, w_pa=w_pa, w_pb=w_pb, w_out=w_out,
                   ln1_g=ln1_g, ln1_b=ln1_b, rg_w=router_group_w, rg_b=router_group_b,
                   re_w=router_expert_w, re_b=router_expert_b, we_g=w_e_gate, we_u=w_e_up,
                   we_d=w_e_down, ln2_g=ln2_g, ln2_b=ln2_b)
    src = [x_prompt.reshape(n_p * seq_len, d), x_sample.reshape(n_s * seq_len, d)]
    for l in range(w_in.shape[0]):
        lambda_init = 0.8 - 0.6 * math.exp(-0.3 * l)
        src = _layer(src, n_seq, seq_len, lambda_init, cos, sin,
                     {k: a[l] for k, a in stacked.items()})
    y_p = _combine(*src, 0, n_p * seq_len).reshape(n_p, seq_len, d)
    y_s = _combine(*src, n_p * seq_len, n_s * seq_len).reshape(n_s, seq_len, d)
    return (y_p, y_s)
```

```python
import functools
import math

import jax
import jax.numpy as jnp
from jax import lax
from jax.experimental import pallas as pl
from jax.experimental.pallas import tpu as pltpu

F32 = jnp.float32
BF16 = jnp.bfloat16

D_MODEL = 1024
DEPTH = 4
DA_HEADS = 8
DA_HEAD_DIM = 64
DA_V_DIM = 128
ROPE_THETA = 10000.0
ML_HEADS = 4
ML_HEAD_DIM = 256
ML_PROJ_BLOCK = 4
N_GROUPS = 4
EXPERTS_PER_GROUP = 8
N_EXPERTS = 32
TOP_K = 2
D_EXPERT = 512
DN_ALPHA = (2 * DEPTH) ** 0.25
LN_EPS = 1e-5

LANES = 128
MXU_DIM = 256
ROW_CHUNKS = D_MODEL // LANES
CHUNK = 128
MOE_ROWS = 512
DMA_PRIORITIES = 2
VMEM_LIMIT = 56 * 1024 * 1024

NEG_BIG = -jnp.inf


def _cparams(sem):
    return pltpu.CompilerParams(dimension_semantics=sem, vmem_limit_bytes=VMEM_LIMIT)


def _sigmoid(x):
    return 0.5 * jnp.tanh(0.5 * x) + 0.5


def _inproj_kernel(*refs, fused, n_first, n_rope, n_scaled, n_qkv, scale, cw):
    if fused:
        (x1t_ref, y0_ref, y1_ref, r_ref, g_ref, b_ref, w_ref, cos_ref, sin_ref,
         x_out_ref, qkv_ref, rest_ref) = refs
        x = _combine_rows(x1t_ref, y0_ref, y1_ref, r_ref, g_ref, b_ref, qkv_ref.shape[0])
        x_out_ref[...] = x
    else:
        xa_ref, xb_ref, w_ref, cos_ref, sin_ref, x_out_ref, qkv_ref, rest_ref = refs
        x = jnp.where(pl.program_id(0) < n_first, xa_ref[...], xb_ref[...])
        x_out_ref[...] = x
    xb = x.astype(BF16)
    tm = qkv_ref.shape[0]
    n_cols = w_ref.shape[1]
    cos = cos_ref[...]
    sin = sin_ref[...]
    lane = lax.broadcasted_iota(jnp.int32, (tm, LANES), 1)
    first_half = (lane % DA_HEAD_DIM) < (DA_HEAD_DIM // 2)
    for c in range(n_cols // cw):
        acc = jnp.dot(xb, w_ref[:, c * cw:(c + 1) * cw], preferred_element_type=F32)
        for s in range(cw // LANES):
            col = c * cw + s * LANES
            t = acc[:, s * LANES:(s + 1) * LANES]
            if col < n_rope:
                rot = jnp.where(first_half, pltpu.roll(t, LANES - DA_HEAD_DIM // 2, 1),
                                pltpu.roll(t, DA_HEAD_DIM // 2, 1))
                t = t * cos + rot * sin
                if col < n_scaled:
                    t = t * scale
            if col < n_qkv:
                qkv_ref[:, col:col + LANES] = t.astype(qkv_ref.dtype)
            else:
                rest_ref[:, col - n_qkv:col - n_qkv + LANES] = t.astype(rest_ref.dtype)


def _inproj(src, w, cos, sin, seq_len, *, n_rope, n_scaled, n_qkv, scale):
    fused = isinstance(src, tuple)
    d, n_cols = w.shape
    tm = min(512, seq_len)
    n_first = 0 if fused else src[0].shape[0] // tm
    n_tok = src[0].shape[0] // ROW_CHUNKS if fused else src[0].shape[0] + src[1].shape[0]
    n_t = n_tok // tm
    per_seq = seq_len // tm
    kern = functools.partial(_inproj_kernel, fused=fused, n_first=n_first, n_rope=n_rope,
                             n_scaled=n_scaled, n_qkv=n_qkv, scale=scale, cw=512)
    table = pl.BlockSpec((tm, LANES), lambda i: (i % per_seq, 0))
    wspec = pl.BlockSpec((d, n_cols), lambda i: (0, 0), pipeline_mode=pl.Buffered(1))
    rows = lambda n: pl.BlockSpec((tm, n), lambda i: (i, 0))
    outs = [rows(n_qkv), rows(n_cols - n_qkv)]
    out_shapes = [jax.ShapeDtypeStruct((n_tok, n_qkv), BF16),
                  jax.ShapeDtypeStruct((n_tok, n_cols - n_qkv), BF16)]
    if fused:
        x1t, y, route, ln_g, ln_b = src
        tiles = lambda off: pl.BlockSpec((tm * ROW_CHUNKS, LANES), lambda i, off=off: (i + off, 0))
        vec = pl.BlockSpec((1, d), lambda i: (0, 0))
        in_specs = [tiles(0), tiles(0), tiles(n_t), rows(LANES), vec, vec, wspec, table, table]
        operands = (x1t, y, y, route, ln_g, ln_b, w, cos, sin)
    else:
        in_specs = [pl.BlockSpec((tm, d), lambda i: (jnp.minimum(i, n_first - 1), 0)),
                    pl.BlockSpec((tm, d), lambda i: (jnp.maximum(i - n_first, 0), 0)),
                    wspec, table, table]
        operands = (src[0], src[1], w, cos, sin)
    return pl.pallas_call(
        kern, grid=(n_t,),
        in_specs=in_specs,
        out_specs=[rows(d)] + outs,
        out_shape=[jax.ShapeDtypeStruct((n_tok, d), F32)] + out_shapes,
        compiler_params=_cparams(("parallel",)),
    )(*operands)


def _attn_kernel(sc_ref, q_ref, k_ref, v_ref, g_ref, o_ref, vx_ref, *, bq):
    seq_len = q_ref.shape[0]
    lam = sc_ref[0]
    out_scale = sc_ref[1]
    k = k_ref[...]
    vx_ref[:, :DA_V_DIM] = v_ref[...]
    vx_ref[:, DA_V_DIM:] = jnp.ones((seq_len, DA_V_DIM), vx_ref.dtype)
    vx = vx_ref[...]
    g = g_ref[...] * out_scale
    lane = lax.broadcasted_iota(jnp.int32, (bq, LANES), 1)
    is_first = lane < DA_HEAD_DIM
    dn = (((1,), (1,)), ((), ()))

    def softmax_av(qm):
        s = lax.dot_general(qm, k, dn, preferred_element_type=F32)
        p = jnp.exp2(s - jnp.max(s, axis=-1, keepdims=True)).astype(BF16)
        ox = jnp.dot(p, vx, preferred_element_type=F32)
        return ox[:, :DA_V_DIM] / ox[:, DA_V_DIM:]

    for i in range(seq_len // bq):
        rows = slice(i * bq, (i + 1) * bq)
        qb = q_ref[rows, :]
        zero = jnp.zeros_like(qb)
        o = (softmax_av(jnp.where(is_first, qb, zero))
             - lam * softmax_av(jnp.where(is_first, zero, qb)))
        o = o * lax.rsqrt(jnp.mean(o * o, axis=-1, keepdims=True) + LN_EPS)
        o_ref[rows, :] = (o * g).astype(o_ref.dtype)


def _attention(qkv, scalars, subln_g, n_seq, seq_len):
    n_tok = qkv.shape[0]
    bq = min(128, seq_len)
    return pl.pallas_call(
        functools.partial(_attn_kernel, bq=bq),
        grid=(n_seq, DA_HEADS),
        in_specs=[
            pl.BlockSpec(memory_space=pltpu.SMEM),
            pl.BlockSpec((seq_len, LANES), lambda b, h: (b, h)),
            pl.BlockSpec((seq_len, LANES), lambda b, h: (b, DA_HEADS + h)),
            pl.BlockSpec((seq_len, LANES), lambda b, h: (b, 2 * DA_HEADS + h)),
            pl.BlockSpec((1, DA_V_DIM), lambda b, h: (0, 0)),
        ],
        out_specs=pl.BlockSpec((seq_len, DA_V_DIM), lambda b, h: (b, h)),
        out_shape=jax.ShapeDtypeStruct((n_tok, DA_HEADS * DA_V_DIM), BF16),
        scratch_shapes=[pltpu.VMEM((seq_len, 2 * DA_V_DIM), BF16)],
        compiler_params=_cparams(("parallel", "parallel")),
    )(scalars, qkv, qkv, qkv, subln_g)


def _split3(x):
    x1 = x.astype(BF16)
    r1 = x - x1.astype(F32)
    x2 = r1.astype(BF16)
    x3 = (r1 - x2.astype(F32)).astype(BF16)
    return x1, x2, x3


def _gate_tables(gacc_ref, bg_ref, g1_ref, g2_ref, g3_ref, tmp_ref, last_ref):
    seq_len = gacc_ref.shape[0]
    n_chunks = seq_len // CHUNK
    r = lax.broadcasted_iota(jnp.int32, (CHUNK, CHUNK), 0)
    c = lax.broadcasted_iota(jnp.int32, (CHUNK, CHUNK), 1)
    tri = jnp.where(c <= r, 1.0, 0.0).astype(BF16)
    lane = lax.broadcasted_iota(jnp.int32, (CHUNK, LANES), 1)
    row = lax.broadcasted_iota(jnp.int32, (CHUNK, LANES), 0)
    is_kind0 = (lane % 2) == 0
    is_bwd = ((lane // 2) % 2) == 1
    is_bwd_row = is_bwd[0:1, :]
    bg = bg_ref[...]

    def first_pass(ci, carry):
        rows = pl.ds(pl.multiple_of(ci * CHUNK, CHUNK), CHUNK)
        pre = gacc_ref[rows, :] + bg
        lf = jnp.minimum(pre, 0.0) - jnp.log1p(jnp.exp(-jnp.abs(pre)))
        l1, l2, l3 = _split3(lf)
        pref = (jnp.dot(tri, l1, preferred_element_type=F32)
                + jnp.dot(tri, l2, preferred_element_type=F32)
                + jnp.dot(tri, l3, preferred_element_type=F32))
        suff = pref[CHUNK - 1:CHUNK, :] - pref + lf
        cum = jnp.where(is_bwd, suff, pref)
        b0 = pltpu.roll(cum, LANES - 1, 1)
        g = pre - b0
        mx_f = g
        mx_b = g
        s = 1
        while s < CHUNK:
            mx_f = jnp.maximum(mx_f, jnp.where(row >= s, pltpu.roll(mx_f, s, 0), -jnp.inf))
            mx_b = jnp.maximum(mx_b, jnp.where(row < CHUNK - s,
                                               pltpu.roll(mx_b, CHUNK - s, 0), -jnp.inf))
            s *= 2
        mx = jnp.where(is_bwd, mx_b, mx_f)
        tmp_ref[0, rows, :] = g
        tmp_ref[1, rows, :] = mx
        tmp_ref[2, rows, :] = b0
        last_ref[0, pl.ds(ci, 1), :] = jnp.where(is_bwd_row, b0[0:1, :], b0[CHUNK - 1:CHUNK, :])
        last_ref[1, pl.ds(ci, 1), :] = jnp.where(is_bwd_row, mx[0:1, :], mx[CHUNK - 1:CHUNK, :])
        return carry

    lax.fori_loop(0, n_chunks, first_pass, 0)

    m_f = jnp.zeros((1, LANES), F32)
    m_b = jnp.zeros((1, LANES), F32)
    for t in range(n_chunks):
        cf, cb = t, n_chunks - 1 - t
        last_ref[2, cf:cf + 1, :] = m_f
        last_ref[3, cb:cb + 1, :] = m_b
        m_f = last_ref[0, cf:cf + 1, :] + jnp.maximum(m_f, last_ref[1, cf:cf + 1, :])
        m_b = last_ref[0, cb:cb + 1, :] + jnp.maximum(m_b, last_ref[1, cb:cb + 1, :])

    def second_pass(ci, carry):
        rows = pl.ds(pl.multiple_of(ci * CHUNK, CHUNK), CHUNK)
        g = tmp_ref[0, rows, :]
        mx = tmp_ref[1, rows, :]
        b0 = tmp_ref[2, rows, :]
        m_st = jnp.where(is_bwd_row, last_ref[3, pl.ds(ci, 1), :], last_ref[2, pl.ds(ci, 1), :])
        mm = jnp.maximum(mx, m_st)
        m_up = jnp.maximum(m_st, last_ref[1, pl.ds(ci, 1), :])
        keep = jnp.broadcast_to(jnp.exp(m_st - m_up), (CHUNK, LANES))
        g1_ref[rows, :] = jnp.where(is_kind0, -mm, pltpu.roll(jnp.exp(-mm - b0), 1, 1))
        g2_ref[rows, :] = jnp.where(is_kind0, jnp.exp(m_st - mm), pltpu.roll(keep, 1, 1))
        g3_ref[rows, :] = jnp.where(is_kind0, g, pltpu.roll(jnp.exp(g - m_up), 1, 1))
        return carry

    lax.fori_loop(0, n_chunks, second_pass, 0)


def _mlpre_kernel(xm_ref, cw_ref, cb_ref, wq_ref, wkt_ref, wk_ref, wv_ref, wg_ref, bg_ref,
                  q_ref, kt_ref, v_ref, xc_ref, g1_ref, g2_ref, g3_ref,
                  gacc_ref, tmp_ref, last_ref):
    j = pl.program_id(1)
    seq_len = xm_ref.shape[0]
    xm = xm_ref[...].astype(F32)
    row = lax.broadcasted_iota(jnp.int32, xm.shape, 0)
    prev = jnp.where(row == 0, 0.0, pltpu.roll(xm, 1, 0))
    nxt = jnp.where(row == seq_len - 1, 0.0, pltpu.roll(xm, seq_len - 1, 0))
    xc = cb_ref[...] + prev * cw_ref[0:1, :] + xm * cw_ref[1:2, :] + nxt * cw_ref[2:3, :]
    xc = xc * _sigmoid(xc)
    xcb = xc.astype(BF16)
    xc_ref[...] = xcb
    q = jnp.dot(xcb, wq_ref[...], preferred_element_type=F32)
    k = jnp.dot(xcb, wk_ref[...], preferred_element_type=F32)
    v = jnp.dot(xm_ref[...], wv_ref[...], preferred_element_type=F32)
    qb = q.astype(BF16)
    kb = k.astype(BF16)
    vb = v.astype(BF16)
    q_ref[...] = (q * (ML_HEAD_DIM ** -0.5)).astype(BF16)
    kt_ref[...] = lax.dot_general(wkt_ref[...], xcb, (((1,), (1,)), ((), ())),
                                  preferred_element_type=F32).astype(BF16)
    v_ref[...] = vb
    part = (jnp.dot(qb, wg_ref[0], preferred_element_type=F32)
            + jnp.dot(kb, wg_ref[1], preferred_element_type=F32)
            + jnp.dot(vb, wg_ref[2], preferred_element_type=F32))

    @pl.when(j == 0)
    def _():
        gacc_ref[...] = part

    @pl.when(j > 0)
    def _():
        gacc_ref[...] += part

    @pl.when(j == pl.num_programs(1) - 1)
    def _():
        _gate_tables(gacc_ref, bg_ref, g1_ref, g2_ref, g3_ref, tmp_ref, last_ref)


def _mlstm_pre(rest, conv_w, conv_b, wq_bd, wk_bd, wv_bd, wg, bg, n_seq, seq_len):
    n_tok = rest.shape[0]
    n_ct = D_MODEL // MXU_DIM
    tile = pl.BlockSpec((seq_len, MXU_DIM), lambda b, j: (b, j))
    wspec = pl.BlockSpec((None, MXU_DIM, MXU_DIM), lambda b, j: (j, 0, 0))
    gspec = pl.BlockSpec((None, seq_len, LANES), lambda b, j: (b, 0, 0))
    act = jax.ShapeDtypeStruct((n_tok, D_MODEL), BF16)
    gate = jax.ShapeDtypeStruct((n_seq, seq_len, LANES), F32)
    return pl.pallas_call(
        _mlpre_kernel,
        grid=(n_seq, n_ct),
        in_specs=[
            tile,
            pl.BlockSpec((3, MXU_DIM), lambda b, j: (0, j)),
            pl.BlockSpec((1, MXU_DIM), lambda b, j: (0, j)),
            wspec, wspec, wspec, wspec,
            pl.BlockSpec((3, MXU_DIM, LANES), lambda b, j: (0, j, 0)),
            pl.BlockSpec((1, LANES), lambda b, j: (0, 0)),
        ],
        out_specs=[tile, pl.BlockSpec((MXU_DIM, seq_len), lambda b, j: (j, b)), tile, tile,
                   gspec, gspec, gspec],
        out_shape=[act, jax.ShapeDtypeStruct((D_MODEL, n_tok), BF16), act, act,
                   gate, gate, gate],
        scratch_shapes=[pltpu.VMEM((seq_len, LANES), F32),
                        pltpu.VMEM((3, seq_len, LANES), F32),
                        pltpu.VMEM((4, seq_len // CHUNK, LANES), F32)],
        compiler_params=_cparams(("parallel", "arbitrary")),
    )(rest, conv_w, conv_b, wq_bd, jnp.swapaxes(wk_bd, 1, 2), wk_bd, wv_bd, wg, bg)


def _mlstm_kernel(q_ref, kt_ref, v_ref, gc1_ref, gc2_ref, gr_ref, o_ref,
                  vx_ref, qk_ref, h_ref):
    seq_len = q_ref.shape[0]
    n_chunks = seq_len // CHUNK
    dh = ML_HEAD_DIM
    r = lax.broadcasted_iota(jnp.int32, (CHUNK, CHUNK), 0)
    c = lax.broadcasted_iota(jnp.int32, (CHUNK, CHUNK), 1)
    vx_ref[:, :dh] = v_ref[...]
    vx_ref[:, dh:] = jnp.ones((seq_len, LANES), vx_ref.dtype)

    def lane_replicated(ref, rows, col):
        return jnp.broadcast_to(ref[rows, col:col + 1], (CHUNK, LANES))

    def chunk_step(direction, ci, c_st):
        mask = (c <= r) if direction == 0 else (c >= r)
        rows = slice(ci * CHUNK, (ci + 1) * CHUNK)
        qc = q_ref[rows, :]
        ktc = kt_ref[:, rows]
        vxc = vx_ref[rows, :]
        neg_mm = lane_replicated(gc1_ref, rows, 2 * direction)
        e_mj = lane_replicated(gc1_ref, rows, 2 * direction + 1)
        inter = lane_replicated(gc2_ref, rows, 2 * direction)
        g_row = gr_ref[2 * direction:2 * direction + 1, rows]
        w_row = gr_ref[2 * direction + 1:2 * direction + 2, rows]
        keep = gc2_ref[ci * CHUNK:ci * CHUNK + 1, 2 * direction + 1:2 * direction + 2]
        if ci in first_visit:
            qk = qk_ref[ci]
        else:
            qk = jnp.dot(qc, ktc, preferred_element_type=F32)
            qk_ref[ci] = qk
        sw = qk * jnp.exp(jnp.where(mask, neg_mm + g_row, -jnp.inf))
        intra = jnp.dot(sw.astype(BF16), vxc, preferred_element_type=F32)
        carry_in = jnp.dot(qc, c_st.astype(BF16), preferred_element_type=F32)
        den = intra[:, dh:] + inter * carry_in[:, dh:]
        rdiv = 1.0 / jnp.maximum(jnp.abs(den), e_mj)
        h = jnp.concatenate(
            [(intra[:, s * LANES:(s + 1) * LANES] + inter * carry_in[:, s * LANES:(s + 1) * LANES])
             * rdiv for s in range(dh // LANES)], axis=1)
        if ci in first_visit:
            h = h_ref[rows, :] + h
            mu = jnp.mean(h, axis=-1, keepdims=True)
            hc = h - mu
            var = jnp.mean(hc * hc, axis=-1, keepdims=True)
            o_ref[rows, :] = (hc * lax.rsqrt(var + LN_EPS)).astype(o_ref.dtype)
        else:
            h_ref[rows, :] = h
            first_visit.add(ci)
        kw = (ktc.astype(F32) * w_row).astype(BF16)
        return keep * c_st + jnp.dot(kw, vxc, preferred_element_type=F32)

    first_visit = set()
    c_f = jnp.zeros((dh, dh + LANES), F32)
    c_b = jnp.zeros((dh, dh + LANES), F32)
    for step in range(n_chunks):
        c_f = chunk_step(0, step, c_f)
        c_b = chunk_step(1, n_chunks - 1 - step, c_b)


def _mlstm(q, kt, v, gc1, gc2, gr3, n_seq, seq_len):
    n_tok = q.shape[0]
    tile = pl.BlockSpec((seq_len, ML_HEAD_DIM), lambda b, h: (b, h))
    col = pl.BlockSpec((None, None, seq_len, 4), lambda b, h: (b, h, 0, 0))
    return pl.pallas_call(
        _mlstm_kernel,
        grid=(n_seq, ML_HEADS),
        in_specs=[
            tile,
            pl.BlockSpec((ML_HEAD_DIM, seq_len), lambda b, h: (h, b)),
            tile, col, col,
            pl.BlockSpec((None, None, 4, seq_len), lambda b, h: (b, h, 0, 0)),
        ],
        out_specs=tile,
        out_shape=jax.ShapeDtypeStruct((n_tok, D_MODEL), BF16),
        scratch_shapes=[
            pltpu.VMEM((seq_len, ML_HEAD_DIM + LANES), BF16),
            pltpu.VMEM((seq_len // CHUNK, CHUNK, CHUNK), F32),
            pltpu.VMEM((seq_len, ML_HEAD_DIM), F32),
        ],
        compiler_params=_cparams(("parallel", "parallel")),
    )(q, kt, v, gc1, gc2, gr3)


def _layer_norm(y, g, b):
    mu = jnp.mean(y, axis=-1, keepdims=True)
    yc = y - mu
    var = jnp.mean(yc * yc, axis=-1, keepdims=True)
    return yc * lax.rsqrt(var + LN_EPS) * g + b


def _route(logits):
    lane = lax.broadcasted_iota(jnp.int32, logits.shape, 1)
    big = jnp.int32(4 * LANES)
    gl = jnp.where(lane < N_GROUPS, logits, NEG_BIG)
    gmax = jnp.max(gl, axis=-1, keepdims=True)
    gsum = jnp.sum(jnp.where(lane < N_GROUPS, jnp.exp(gl - gmax), 0.0), axis=-1, keepdims=True)
    g_sel = jnp.min(jnp.where(gl == gmax, lane, big), axis=-1, keepdims=True)
    g_prob = 1.0 / gsum
    lo = N_GROUPS + EXPERTS_PER_GROUP * g_sel
    in_group = jnp.logical_and(lane >= lo, lane < lo + EXPERTS_PER_GROUP)
    el = jnp.where(in_group, logits, NEG_BIG)
    emax = jnp.max(el, axis=-1, keepdims=True)
    esum = jnp.sum(jnp.where(in_group, jnp.exp(el - emax), 0.0), axis=-1, keepdims=True)
    i1 = jnp.min(jnp.where(el == emax, lane, big), axis=-1, keepdims=True)
    el2 = jnp.where(lane == i1, NEG_BIG, el)
    emax2 = jnp.max(el2, axis=-1, keepdims=True)
    i2 = jnp.min(jnp.where(el2 == emax2, lane, big), axis=-1, keepdims=True)
    p1 = 1.0 / esum
    p2 = jnp.exp(emax2 - emax) / esum
    psum = p1 + p2
    gate1 = g_prob * p1 / psum
    gate2 = g_prob * p2 / psum
    e1 = (i1 - N_GROUPS).astype(F32)
    e2 = (i2 - N_GROUPS).astype(F32)
    return jnp.where(lane == 0, e1, jnp.where(lane == 1, e2, jnp.where(lane == 2, gate1, gate2)))


def _post_kernel(attn_ref, hn_ref, xc_ref, z_ref, ga_ref, gb_ref, x_ref,
                 wpa_ref, wpb_ref, wout_ref, gn_ref, skip_ref, lg_ref, lb_ref, wr_ref, br_ref,
                 x1t_ref, route_ref, *, sub):
    tm = x_ref.shape[0]
    for s in range(tm // sub):
        rows = slice(s * sub, (s + 1) * sub)
        z = z_ref[rows, :].astype(F32)
        ml = ((hn_ref[rows, :].astype(F32) * gn_ref[...]
               + skip_ref[...] * xc_ref[rows, :].astype(F32)) * (z * _sigmoid(z)))
        a_out = jnp.dot(attn_ref[rows, :], wpa_ref[...], preferred_element_type=F32)
        m_out = jnp.dot(ml.astype(BF16), wpb_ref[...], preferred_element_type=F32)
        mixed = (_sigmoid(ga_ref[rows, :].astype(F32)) * a_out
                 + _sigmoid(gb_ref[rows, :].astype(F32)) * m_out)
        y = DN_ALPHA * x_ref[rows, :] + jnp.dot(mixed.astype(BF16), wout_ref[...],
                                               preferred_element_type=F32)
        x1 = _layer_norm(y, lg_ref[...], lb_ref[...])
        for c in range(ROW_CHUNKS):
            x1t_ref[pl.ds(s * sub * ROW_CHUNKS + c, sub, stride=ROW_CHUNKS), :] = (
                x1[:, c * LANES:(c + 1) * LANES])
        logits = br_ref[...] + jnp.dot(x1.astype(BF16), wr_ref[...], preferred_element_type=F32)
        route_ref[rows, :] = _route(logits)


def _post(attn, hn, xc, rest, x, w_pa, w_pb, w_out, gn_g, skip, ln_g, ln_b, w_route, b_route):
    n_tok = x.shape[0]
    sub = 256
    tm = 2 * sub if n_tok % (2 * sub) == 0 else sub
    row = lambda col: pl.BlockSpec((tm, D_MODEL), lambda i, col=col: (i, col))
    full = lambda shape: pl.BlockSpec(shape, lambda i: tuple(0 for _ in shape))
    vec = full((1, D_MODEL))
    return pl.pallas_call(
        functools.partial(_post_kernel, sub=sub),
        grid=(n_tok // tm,),
        in_specs=[row(0), row(0), row(0), row(1), row(2), row(3), row(0),
                  full((D_MODEL, D_MODEL)), full((D_MODEL, D_MODEL)), full((D_MODEL, D_MODEL)),
                  vec, vec, vec, vec,
                  full((D_MODEL, LANES)), full((1, LANES))],
        out_specs=[pl.BlockSpec((tm * ROW_CHUNKS, LANES), lambda i: (i, 0)),
                   pl.BlockSpec((tm, LANES), lambda i: (i, 0))],
        out_shape=[jax.ShapeDtypeStruct((n_tok * ROW_CHUNKS, LANES), F32),
                   jax.ShapeDtypeStruct((n_tok, LANES), F32)],
        compiler_params=_cparams(("parallel",)),
    )(attn, hn, xc, rest, rest, rest, x, w_pa, w_pb, w_out, gn_g, skip, ln_g, ln_b,
      w_route, b_route)


def _moe_kernel(bexp_ref, nused_ref,
                tok_hbm, dst_hbm, x_hbm, wg_ref, wu_ref, wd_ref, y_hbm,
                tok_smem, dst_smem, xbuf, obuf, idx_sem, gat_sem, sca_sem):
    i = pl.program_id(0)
    n_used = nused_ref[0]
    p = i % 2

    def index_copies(blk):
        slot = blk % 3
        return (pltpu.make_async_copy(tok_hbm.at[blk], tok_smem.at[slot], idx_sem.at[0, slot]),
                pltpu.make_async_copy(dst_hbm.at[blk], dst_smem.at[slot], idx_sem.at[1, slot]))

    def start_indices(blk):
        for cp in index_copies(blk):
            cp.start()

    def wait_indices(blk):
        for cp in index_copies(blk):
            cp.wait()

    def tile_rows(r):
        return pl.ds(r * ROW_CHUNKS, ROW_CHUNKS)

    def start_gather(blk, slot, rows=range(MOE_ROWS)):
        islot = blk % 3

        for row in rows:
            t = tok_smem[islot, row // LANES, row % LANES]
            pltpu.make_async_copy(x_hbm.at[t], xbuf.at[slot, tile_rows(row)],
                                  gat_sem.at[slot]).start(priority=row % DMA_PRIORITIES)

    def wait_gather(slot):
        pltpu.make_async_copy(xbuf.at[slot], xbuf.at[slot], gat_sem.at[slot]).wait()

    def start_scatter(blk, slot, rows):
        islot = blk % 3

        for row in rows:
            t = dst_smem[islot, row // LANES, row % LANES]
            pltpu.make_async_copy(obuf.at[slot, tile_rows(row)], y_hbm.at[t],
                                  sca_sem.at[slot]).start(priority=row % DMA_PRIORITIES)

    def wait_scatter(slot):
        pltpu.make_async_copy(obuf.at[slot], obuf.at[slot], sca_sem.at[slot]).wait()

    @pl.when(i == 0)
    def _():
        start_indices(0)
        wait_indices(0)
        start_gather(0, 0)

        @pl.when(1 < n_used)
        def _():
            start_indices(1)

    @pl.when(i + 1 < n_used)
    def _():
        wait_indices(i + 1)

    @pl.when(i + 2 < n_used)
    def _():
        start_indices(i + 2)

    @pl.when(i < n_used)
    def _():
        wait_gather(p)

        @pl.when(i >= 2)
        def _():
            wait_scatter(p)

        nxt = jnp.minimum(i + 1, n_used - 1)
        half = MOE_ROWS // 2
        for h in range(2):
            rows = range(h * half, (h + 1) * half)
            start_gather(nxt, 1 - p, rows)
            xb = jnp.concatenate(
                [xbuf[p, pl.ds(h * half * ROW_CHUNKS + c, half, stride=ROW_CHUNKS), :]
                 for c in range(ROW_CHUNKS)], axis=1).astype(BF16)
            hg = jnp.dot(xb, wg_ref[...], preferred_element_type=F32)
            hu = jnp.dot(xb, wu_ref[...], preferred_element_type=F32)
            hh = (hg * _sigmoid(hg) * hu).astype(BF16)
            out = jnp.dot(hh, wd_ref[...], preferred_element_type=F32)
            for c in range(ROW_CHUNKS):
                obuf[p, pl.ds(h * half * ROW_CHUNKS + c, half, stride=ROW_CHUNKS), :] = (
                    out[:, c * LANES:(c + 1) * LANES])
            start_scatter(i, p, rows)

        @pl.when(i == n_used - 1)
        def _():
            wait_scatter(p)
            wait_gather(1 - p)

            @pl.when(i >= 1)
            def _():
                wait_scatter(1 - p)


def _moe(x1t, slot_tok, slot_dst, blk_exp, n_used, we_g, we_u, we_d):
    n_tok = x1t.shape[0] // ROW_CHUNKS
    n_blk = blk_exp.shape[0]
    n_out = TOP_K * n_tok + 2 * MOE_ROWS
    wspec_in = pl.BlockSpec((None, D_MODEL, D_EXPERT), lambda i, be, nu: (be[i], 0, 0))
    wspec_out = pl.BlockSpec((None, D_EXPERT, D_MODEL), lambda i, be, nu: (be[i], 0, 0))
    grid_spec = pltpu.PrefetchScalarGridSpec(
        num_scalar_prefetch=2,
        grid=(n_blk,),
        in_specs=[
            pl.BlockSpec(memory_space=pl.ANY),
            pl.BlockSpec(memory_space=pl.ANY),
            pl.BlockSpec(memory_space=pl.ANY),
            wspec_in, wspec_in, wspec_out,
        ],
        out_specs=pl.BlockSpec(memory_space=pl.ANY),
        scratch_shapes=[
            pltpu.SMEM((3, MOE_ROWS // LANES, LANES), jnp.int32),
            pltpu.SMEM((3, MOE_ROWS // LANES, LANES), jnp.int32),
            pltpu.VMEM((2, MOE_ROWS * ROW_CHUNKS, LANES), F32),
            pltpu.VMEM((2, MOE_ROWS * ROW_CHUNKS, LANES), F32),
            pltpu.SemaphoreType.DMA((2, 3)),
            pltpu.SemaphoreType.DMA((2,)),
            pltpu.SemaphoreType.DMA((2,)),
        ],
    )
    y = pl.pallas_call(
        _moe_kernel,
        grid_spec=grid_spec,
        out_shape=jax.ShapeDtypeStruct((n_out, ROW_CHUNKS, LANES), F32),
        compiler_params=_cparams(("arbitrary",)),
    )(blk_exp, n_used, slot_tok.reshape(n_blk, MOE_ROWS // LANES, LANES),
      slot_dst.reshape(n_blk, MOE_ROWS // LANES, LANES),
      x1t.reshape(n_tok, ROW_CHUNKS, LANES), we_g, we_u, we_d)
    return y.reshape(n_out * ROW_CHUNKS, LANES)


def _route_plan(route, n_tok):
    n_assign = n_tok * TOP_K
    eid = route[:, 0:TOP_K].astype(jnp.int32).reshape(n_assign)
    order = jnp.argsort(eid, stable=False).astype(jnp.int32)
    counts = jnp.zeros((N_EXPERTS,), jnp.int32).at[eid].add(1)
    offsets = jnp.cumsum(counts) - counts
    padded = (counts + MOE_ROWS - 1) // MOE_ROWS * MOE_ROWS
    pad_end = jnp.cumsum(padded)
    pad_start = pad_end - padded
    n_blk = (n_assign + MOE_ROWS - 1) // MOE_ROWS + N_EXPERTS
    blk_start = jnp.arange(n_blk, dtype=jnp.int32) * MOE_ROWS
    blk_exp = jnp.minimum(jnp.sum(pad_end[None, :] <= blk_start[:, None], axis=1),
                          N_EXPERTS - 1).astype(jnp.int32)
    row = jnp.arange(MOE_ROWS, dtype=jnp.int32)[None, :]
    j = (blk_start - pad_start[blk_exp])[:, None] + row
    valid = j < counts[blk_exp][:, None]
    src = jnp.clip(offsets[blk_exp][:, None] + j, 0, n_assign - 1)
    assign = order[src]
    tok = assign // TOP_K
    dummy = TOP_K * n_tok + (jnp.arange(n_blk, dtype=jnp.int32) % 2)[:, None] * MOE_ROWS + row
    slot_tok = jnp.where(valid, tok, 0)
    slot_dst = jnp.where(valid, (assign % TOP_K) * n_tok + tok, dummy)
    n_used = (pad_end[-1:] // MOE_ROWS).astype(jnp.int32)
    return slot_tok, slot_dst, blk_exp, n_used


def _combine_rows(x_ref, y0_ref, y1_ref, r_ref, g_ref, b_ref, tm):
    def chunk(ref, c):
        return ref[pl.ds(c, tm, stride=ROW_CHUNKS), :]

    route = r_ref[...]
    gate0 = jnp.broadcast_to(route[:, TOP_K:TOP_K + 1], (tm, LANES))
    gate1 = jnp.broadcast_to(route[:, TOP_K + 1:TOP_K + 2], (tm, LANES))
    y = jnp.concatenate(
        [DN_ALPHA * chunk(x_ref, c) + (gate0 * chunk(y0_ref, c) + gate1 * chunk(y1_ref, c))
         for c in range(ROW_CHUNKS)], axis=1)
    return _layer_norm(y, g_ref[...], b_ref[...])


def _combine_kernel(x_ref, y0_ref, y1_ref, r_ref, g_ref, b_ref, o_ref):
    o_ref[...] = _combine_rows(x_ref, y0_ref, y1_ref, r_ref, g_ref, b_ref, o_ref.shape[0])


def _combine(x1t, y, route, ln_g, ln_b, first_tok, n_out):
    n_tok = x1t.shape[0] // ROW_CHUNKS
    tm = 256
    n_t = n_tok // tm
    t0 = first_tok // tm
    vec = pl.BlockSpec((1, D_MODEL), lambda i: (0, 0))
    tiles = lambda off: pl.BlockSpec((tm * ROW_CHUNKS, LANES), lambda i, off=off: (i + off, 0))
    return pl.pallas_call(
        _combine_kernel,
        grid=(n_out // tm,),
        in_specs=[tiles(t0), tiles(t0), tiles(t0 + n_t),
                  pl.BlockSpec((tm, LANES), lambda i: (i + t0, 0)), vec, vec],
        out_specs=pl.BlockSpec((tm, D_MODEL), lambda i: (i, 0)),
        out_shape=jax.ShapeDtypeStruct((n_out, D_MODEL), F32),
        compiler_params=_cparams(("parallel",)),
    )(x1t, y, y, route, ln_g, ln_b)


def _rope_tables(seq_len):
    half = DA_HEAD_DIM // 2
    inv = ROPE_THETA ** (-jnp.arange(half, dtype=F32) * (2.0 / DA_HEAD_DIM))
    ang = jnp.arange(seq_len, dtype=F32)[:, None] * inv[None, :]
    cos = jnp.tile(jnp.cos(ang), (1, LANES // half))
    sin = jnp.tile(jnp.sin(ang), (1, LANES // half))
    lane = jnp.arange(LANES)
    sign = jnp.where((lane % DA_HEAD_DIM) < half, -1.0, 1.0).astype(F32)
    return cos, sin * sign[None, :]


def _block_diag_tiles(w):
    per_tile = MXU_DIM // ML_PROJ_BLOCK
    n_tiles = w.shape[0] // per_tile
    w4 = w.reshape(n_tiles, per_tile, ML_PROJ_BLOCK, ML_PROJ_BLOCK)
    eye = jnp.eye(per_tile, dtype=w.dtype)
    bd = jnp.einsum('jgio,gh->jgiho', w4, eye)
    return bd.reshape(n_tiles, MXU_DIM, MXU_DIM).astype(BF16)


def _gate_perm():
    idx = []
    for h in range(ML_HEADS):
        for d in range(2):
            for kind in range(2):
                idx.append(d * 2 * ML_HEADS + kind * ML_HEADS + h)
    return jnp.array(idx, dtype=jnp.int32)


def _layer(src, n_seq, seq_len, lambda_init, cos, sin, p):
    qk_w = DA_HEADS * 2 * DA_HEAD_DIM
    v_w = DA_HEADS * DA_V_DIM
    x, qkv, rest = _inproj(src, p['w_in'].astype(BF16), cos, sin, seq_len, n_rope=2 * qk_w,
                           n_scaled=qk_w, n_qkv=2 * qk_w + v_w,
                           scale=DA_HEAD_DIM ** -0.5 * math.log2(math.e))
    n_tok = x.shape[0]

    lam = (jnp.exp(jnp.sum(p['lq1'] * p['lk1'])) - jnp.exp(jnp.sum(p['lq2'] * p['lk2']))
           + lambda_init)
    scalars = jnp.stack([lam, jnp.asarray(1.0 - lambda_init, F32)]).astype(F32)
    attn = _attention(qkv, scalars, p['subln_g'].reshape(1, DA_V_DIM), n_seq, seq_len)

    perm = _gate_perm()
    n_gate = 4 * ML_HEADS
    wg = p['w_gate'][:, perm].reshape(3, D_MODEL, n_gate)
    wg = jnp.pad(wg, ((0, 0), (0, 0), (0, LANES - n_gate))).astype(BF16)
    bg = jnp.pad(p['b_gate'][perm], (0, LANES - n_gate)).reshape(1, LANES)
    q, kt, v, xc, g1, g2, g3 = _mlstm_pre(
        rest, p['conv_w'], p['conv_b'].reshape(1, D_MODEL), _block_diag_tiles(p['wq']),
        _block_diag_tiles(p['wk']), _block_diag_tiles(p['wv']), wg, bg, n_seq, seq_len)

    def per_head(t, perm):
        t = t[:, :, :n_gate].reshape(n_seq, seq_len, ML_HEADS, 4)
        return jnp.transpose(t, perm)

    hn = _mlstm(q, kt, v, per_head(g1, (0, 2, 1, 3)), per_head(g2, (0, 2, 1, 3)),
                per_head(g3, (0, 2, 3, 1)), n_seq, seq_len)

    n_route = N_GROUPS + N_EXPERTS
    w_route = jnp.concatenate([p['rg_w'], p['re_w']], axis=1)
    w_route = jnp.pad(w_route, ((0, 0), (0, LANES - n_route)))
    w_route = w_route.astype(BF16)
    b_route = jnp.pad(jnp.concatenate([p['rg_b'], p['re_b']]), (0, LANES - n_route))
    vec = lambda a: a.reshape(1, D_MODEL)
    x1, route = _post(attn, hn, xc, rest, x, p['w_pa'].astype(BF16), p['w_pb'].astype(BF16),
                      p['w_out'].astype(BF16), vec(p['gn_g']), vec(p['skip']), vec(p['ln1_g']),
                      vec(p['ln1_b']), w_route, b_route.reshape(1, LANES))

    plan = _route_plan(route, n_tok)
    y = _moe(x1, *plan, p['we_g'].astype(BF16), p['we_u'].astype(BF16), p['we_d'].astype(BF16))
    return x1, y, route, vec(p['ln2_g']), vec(p['ln2_b'])


def kernel(x_prompt, x_sample, w_in, da_lambda_q1, da_lambda_k1, da_lambda_q2, da_lambda_k2, da_subln_g, ml_conv_w, ml_conv_b, ml_wq, ml_wk, ml_wv, ml_w_gate, ml_b_gate, ml_skip, ml_gn_g, w_pa, w_pb, w_out, ln1_g, ln1_b, router_group_w, router_group_b, router_expert_w, router_expert_b, w_e_gate, w_e_up, w_e_down, ln2_g, ln2_b):
    n_p, seq_len, d = x_prompt.shape
    n_s = x_sample.shape[0]
    assert x_sample.shape[1] == seq_len and d == D_MODEL and seq_len % CHUNK == 0
    n_seq = n_p + n_s
    cos, sin = _rope_tables(seq_len)
    stacked = dict(w_in=w_in, lq1=da_lambda_q1, lk1=da_lambda_k1, lq2=da_lambda_q2,
                   lk2=da_lambda_k2, subln_g=da_subln_g, conv_w=ml_conv_w, conv_b=ml_conv_b,
                   wq=ml_wq, wk=ml_wk, wv=ml_wv, w_gate=ml_w_gate, b_gate=ml_b_gate,
                   skip=ml_skip, gn_g=ml_gn_g, w_pa=w_pa, w_pb=w_pb, w_out=w_out,
                   ln1_g=ln1_g, ln1_b=ln1_b, rg_w=router_group_w, rg_b=router_group_b,
                   re_w=router_expert_w, re_b=router_expert_b, we_g=w_e_gate, we_u=w_e_up,
                   we_d=w_e_down, ln2_g=ln2_g, ln2_b=ln2_b)
    src = [x_prompt.reshape(n_p * seq_len, d), x_sample.reshape(n_s * seq_len, d)]
    for l in range(w_in.shape[0]):
        lambda_init = 0.8 - 0.6 * math.exp(-0.3 * l)
        src = _layer(src, n_seq, seq_len, lambda_init, cos, sin,
                     {k: a[l] for k, a in stacked.items()})
    y_p = _combine(*src, 0, n_p * seq_len).reshape(n_p, seq_len, d)
    y_s = _combine(*src, n_p * seq_len, n_s * seq_len).reshape(n_s, seq_len, d)
    return (y_p, y_s)
```

```python
import functools
import math

import jax
import jax.numpy as jnp
from jax import lax
from jax.experimental import pallas as pl
from jax.experimental.pallas import tpu as pltpu

F32 = jnp.float32
BF16 = jnp.bfloat16

D_MODEL = 1024
DEPTH = 4
DA_HEADS = 8
DA_HEAD_DIM = 64
DA_V_DIM = 128
ROPE_THETA = 10000.0
ML_HEADS = 4
ML_HEAD_DIM = 256
ML_PROJ_BLOCK = 4
N_GROUPS = 4
EXPERTS_PER_GROUP = 8
N_EXPERTS = 32
TOP_K = 2
D_EXPERT = 512
DN_ALPHA = (2 * DEPTH) ** 0.25
LN_EPS = 1e-5

LANES = 128
MXU_DIM = 256
ROW_CHUNKS = D_MODEL // LANES
CHUNK = 128
MOE_ROWS = 512
ATTN_HEADS_PER_STEP = 2
DMA_PRIORITIES = 2
VMEM_LIMIT = 56 * 1024 * 1024

NEG_BIG = -jnp.inf


def _cparams(sem):
    return pltpu.CompilerParams(dimension_semantics=sem, vmem_limit_bytes=VMEM_LIMIT)


def _sigmoid(x):
    return 0.5 * jnp.tanh(0.5 * x) + 0.5


def _inproj_kernel(*refs, fused, n_first, n_rope, n_scaled, n_qkv, scale, cw):
    if fused:
        (x1t_ref, y0_ref, y1_ref, r_ref, g_ref, b_ref, w_ref, cos_ref, sin_ref,
         x_out_ref, qkv_ref, rest_ref) = refs
        x = _combine_rows(x1t_ref, y0_ref, y1_ref, r_ref, g_ref, b_ref, qkv_ref.shape[0])
        x_out_ref[...] = x
    else:
        xa_ref, xb_ref, w_ref, cos_ref, sin_ref, x_out_ref, qkv_ref, rest_ref = refs
        x = jnp.where(pl.program_id(0) < n_first, xa_ref[...], xb_ref[...])
        x_out_ref[...] = x
    xb = x.astype(BF16)
    tm = qkv_ref.shape[0]
    n_cols = w_ref.shape[1]
    cos = cos_ref[...]
    sin = sin_ref[...]
    lane = lax.broadcasted_iota(jnp.int32, (tm, LANES), 1)
    first_half = (lane % DA_HEAD_DIM) < (DA_HEAD_DIM // 2)
    for c in range(n_cols // cw):
        acc = jnp.dot(xb, w_ref[:, c * cw:(c + 1) * cw], preferred_element_type=F32)
        for s in range(cw // LANES):
            col = c * cw + s * LANES
            t = acc[:, s * LANES:(s + 1) * LANES]
            if col < n_rope:
                rot = jnp.where(first_half, pltpu.roll(t, LANES - DA_HEAD_DIM // 2, 1),
                                pltpu.roll(t, DA_HEAD_DIM // 2, 1))
                t = t * cos + rot * sin
                if col < n_scaled:
                    t = t * scale
            if col < n_qkv:
                qkv_ref[:, col:col + LANES] = t.astype(qkv_ref.dtype)
            else:
                rest_ref[:, col - n_qkv:col - n_qkv + LANES] = t.astype(rest_ref.dtype)


def _inproj(src, w, cos, sin, seq_len, *, n_rope, n_scaled, n_qkv, scale):
    fused = isinstance(src, tuple)
    d, n_cols = w.shape
    tm = min(512, seq_len)
    n_first = 0 if fused else src[0].shape[0] // tm
    n_tok = src[0].shape[0] // ROW_CHUNKS if fused else src[0].shape[0] + src[1].shape[0]
    n_t = n_tok // tm
    per_seq = seq_len // tm
    kern = functools.partial(_inproj_kernel, fused=fused, n_first=n_first, n_rope=n_rope,
                             n_scaled=n_scaled, n_qkv=n_qkv, scale=scale, cw=512)
    table = pl.BlockSpec((tm, LANES), lambda i: (i % per_seq, 0))
    wspec = pl.BlockSpec((d, n_cols), lambda i: (0, 0), pipeline_mode=pl.Buffered(1))
    rows = lambda n: pl.BlockSpec((tm, n), lambda i: (i, 0))
    outs = [rows(n_qkv), rows(n_cols - n_qkv)]
    out_shapes = [jax.ShapeDtypeStruct((n_tok, n_qkv), BF16),
                  jax.ShapeDtypeStruct((n_tok, n_cols - n_qkv), BF16)]
    if fused:
        x1t, y, route, ln_g, ln_b = src
        tiles = lambda off: pl.BlockSpec((tm * ROW_CHUNKS, LANES), lambda i, off=off: (i + off, 0))
        vec = pl.BlockSpec((1, d), lambda i: (0, 0))
        in_specs = [tiles(0), tiles(0), tiles(n_t), rows(LANES), vec, vec, wspec, table, table]
        operands = (x1t, y, y, route, ln_g, ln_b, w, cos, sin)
    else:
        in_specs = [pl.BlockSpec((tm, d), lambda i: (jnp.minimum(i, n_first - 1), 0)),
                    pl.BlockSpec((tm, d), lambda i: (jnp.maximum(i - n_first, 0), 0)),
                    wspec, table, table]
        operands = (src[0], src[1], w, cos, sin)
    return pl.pallas_call(
        kern, grid=(n_t,),
        in_specs=in_specs,
        out_specs=[rows(d)] + outs,
        out_shape=[jax.ShapeDtypeStruct((n_tok, d), F32)] + out_shapes,
        compiler_params=_cparams(("parallel",)),
    )(*operands)


def _attn_kernel(sc_ref, q_ref, k_ref, v_ref, g_ref, o_ref, vx_ref, *, bq):
    seq_len = q_ref.shape[0]
    lam = sc_ref[0]
    out_scale = sc_ref[1]
    g = g_ref[...] * out_scale
    lane = lax.broadcasted_iota(jnp.int32, (bq, LANES), 1)
    is_first = lane < DA_HEAD_DIM
    dn = (((1,), (1,)), ((), ()))

    def softmax_av(qm, k, vx):
        s = lax.dot_general(qm, k, dn, preferred_element_type=F32)
        p = jnp.exp2(s - jnp.max(s, axis=-1, keepdims=True)).astype(BF16)
        ox = jnp.dot(p, vx, preferred_element_type=F32)
        return ox[:, :DA_V_DIM] / ox[:, DA_V_DIM:]

    for hd in range(q_ref.shape[1] // LANES):
        cols = slice(hd * LANES, (hd + 1) * LANES)
        k = k_ref[:, cols]
        vx_ref[hd, :, :DA_V_DIM] = v_ref[:, cols]
        vx_ref[hd, :, DA_V_DIM:] = jnp.ones((seq_len, DA_V_DIM), vx_ref.dtype)
        vx = vx_ref[hd]
        for i in range(seq_len // bq):
            rows = slice(i * bq, (i + 1) * bq)
            qb = q_ref[rows, cols]
            zero = jnp.zeros_like(qb)
            o = (softmax_av(jnp.where(is_first, qb, zero), k, vx)
                 - lam * softmax_av(jnp.where(is_first, zero, qb), k, vx))
            o = o * lax.rsqrt(jnp.mean(o * o, axis=-1, keepdims=True) + LN_EPS)
            o_ref[rows, cols] = (o * g).astype(o_ref.dtype)


def _attention(qkv, scalars, subln_g, n_seq, seq_len):
    n_tok = qkv.shape[0]
    bq = min(128, seq_len)
    hw = ATTN_HEADS_PER_STEP * LANES
    return pl.pallas_call(
        functools.partial(_attn_kernel, bq=bq),
        grid=(n_seq, DA_HEADS // ATTN_HEADS_PER_STEP),
        in_specs=[
            pl.BlockSpec(memory_space=pltpu.SMEM),
            pl.BlockSpec((seq_len, hw), lambda b, h: (b, h)),
            pl.BlockSpec((seq_len, hw), lambda b, h: (b, DA_HEADS // ATTN_HEADS_PER_STEP + h)),
            pl.BlockSpec((seq_len, hw),
                         lambda b, h: (b, 2 * DA_HEADS // ATTN_HEADS_PER_STEP + h)),
            pl.BlockSpec((1, DA_V_DIM), lambda b, h: (0, 0)),
        ],
        out_specs=pl.BlockSpec((seq_len, hw), lambda b, h: (b, h)),
        out_shape=jax.ShapeDtypeStruct((n_tok, DA_HEADS * DA_V_DIM), BF16),
        scratch_shapes=[pltpu.VMEM((ATTN_HEADS_PER_STEP, seq_len, 2 * DA_V_DIM), BF16)],
        compiler_params=_cparams(("parallel", "parallel")),
    )(scalars, qkv, qkv, qkv, subln_g)


def _split3(x):
    x1 = x.astype(BF16)
    r1 = x - x1.astype(F32)
    x2 = r1.astype(BF16)
    x3 = (r1 - x2.astype(F32)).astype(BF16)
    return x1, x2, x3


def _gate_tables(gacc_ref, bg_ref, g1_ref, g2_ref, g3_ref, tmp_ref, last_ref):
    seq_len = gacc_ref.shape[0]
    n_chunks = seq_len // CHUNK
    r = lax.broadcasted_iota(jnp.int32, (CHUNK, CHUNK), 0)
    c = lax.broadcasted_iota(jnp.int32, (CHUNK, CHUNK), 1)
    tri = jnp.where(c <= r, 1.0, 0.0).astype(BF16)
    lane = lax.broadcasted_iota(jnp.int32, (CHUNK, LANES), 1)
    row = lax.broadcasted_iota(jnp.int32, (CHUNK, LANES), 0)
    is_kind0 = (lane % 2) == 0
    is_bwd = ((lane // 2) % 2) == 1
    is_bwd_row = is_bwd[0:1, :]
    bg = bg_ref[...]

    def first_pass(ci, carry):
        rows = pl.ds(pl.multiple_of(ci * CHUNK, CHUNK), CHUNK)
        pre = gacc_ref[rows, :] + bg
        lf = jnp.minimum(pre, 0.0) - jnp.log1p(jnp.exp(-jnp.abs(pre)))
        l1, l2, l3 = _split3(lf)
        pref = (jnp.dot(tri, l1, preferred_element_type=F32)
                + jnp.dot(tri, l2, preferred_element_type=F32)
                + jnp.dot(tri, l3, preferred_element_type=F32))
        suff = pref[CHUNK - 1:CHUNK, :] - pref + lf
        cum = jnp.where(is_bwd, suff, pref)
        b0 = pltpu.roll(cum, LANES - 1, 1)
        g = pre - b0
        mx_f = g
        mx_b = g
        s = 1
        while s < CHUNK:
            mx_f = jnp.maximum(mx_f, jnp.where(row >= s, pltpu.roll(mx_f, s, 0), -jnp.inf))
            mx_b = jnp.maximum(mx_b, jnp.where(row < CHUNK - s,
                                               pltpu.roll(mx_b, CHUNK - s, 0), -jnp.inf))
            s *= 2
        mx = jnp.where(is_bwd, mx_b, mx_f)
        tmp_ref[0, rows, :] = g
        tmp_ref[1, rows, :] = mx
        tmp_ref[2, rows, :] = b0
        last_ref[0, pl.ds(ci, 1), :] = jnp.where(is_bwd_row, b0[0:1, :], b0[CHUNK - 1:CHUNK, :])
        last_ref[1, pl.ds(ci, 1), :] = jnp.where(is_bwd_row, mx[0:1, :], mx[CHUNK - 1:CHUNK, :])
        return carry

    lax.fori_loop(0, n_chunks, first_pass, 0)

    m_f = jnp.zeros((1, LANES), F32)
    m_b = jnp.zeros((1, LANES), F32)
    for t in range(n_chunks):
        cf, cb = t, n_chunks - 1 - t
        last_ref[2, cf:cf + 1, :] = m_f
        last_ref[3, cb:cb + 1, :] = m_b
        m_f = last_ref[0, cf:cf + 1, :] + jnp.maximum(m_f, last_ref[1, cf:cf + 1, :])
        m_b = last_ref[0, cb:cb + 1, :] + jnp.maximum(m_b, last_ref[1, cb:cb + 1, :])

    def second_pass(ci, carry):
        rows = pl.ds(pl.multiple_of(ci * CHUNK, CHUNK), CHUNK)
        g = tmp_ref[0, rows, :]
        mx = tmp_ref[1, rows, :]
        b0 = tmp_ref[2, rows, :]
        m_st = jnp.where(is_bwd_row, last_ref[3, pl.ds(ci, 1), :], last_ref[2, pl.ds(ci, 1), :])
        mm = jnp.maximum(mx, m_st)
        m_up = jnp.maximum(m_st, last_ref[1, pl.ds(ci, 1), :])
        keep = jnp.broadcast_to(jnp.exp(m_st - m_up), (CHUNK, LANES))
        g1_ref[rows, :] = jnp.where(is_kind0, -mm, pltpu.roll(jnp.exp(-mm - b0), 1, 1))
        g2_ref[rows, :] = jnp.where(is_kind0, jnp.exp(m_st - mm), pltpu.roll(keep, 1, 1))
        g3_ref[rows, :] = jnp.where(is_kind0, g, pltpu.roll(jnp.exp(g - m_up), 1, 1))
        return carry

    lax.fori_loop(0, n_chunks, second_pass, 0)


def _mlpre_kernel(xm_ref, cw_ref, cb_ref, wq_ref, wkt_ref, wk_ref, wv_ref, wg_ref, bg_ref,
                  q_ref, kt_ref, v_ref, xc_ref, g1_ref, g2_ref, g3_ref,
                  gacc_ref, tmp_ref, last_ref):
    j = pl.program_id(1)
    seq_len = xm_ref.shape[0]
    xm = xm_ref[...].astype(F32)
    row = lax.broadcasted_iota(jnp.int32, xm.shape, 0)
    prev = jnp.where(row == 0, 0.0, pltpu.roll(xm, 1, 0))
    nxt = jnp.where(row == seq_len - 1, 0.0, pltpu.roll(xm, seq_len - 1, 0))
    xc = cb_ref[...] + prev * cw_ref[0:1, :] + xm * cw_ref[1:2, :] + nxt * cw_ref[2:3, :]
    xc = xc * _sigmoid(xc)
    xcb = xc.astype(BF16)
    xc_ref[...] = xcb
    q = jnp.dot(xcb, wq_ref[...], preferred_element_type=F32)
    k = jnp.dot(xcb, wk_ref[...], preferred_element_type=F32)
    v = jnp.dot(xm_ref[...], wv_ref[...], preferred_element_type=F32)
    qb = q.astype(BF16)
    kb = k.astype(BF16)
    vb = v.astype(BF16)
    q_ref[...] = (q * (ML_HEAD_DIM ** -0.5)).astype(BF16)
    kt_ref[...] = lax.dot_general(wkt_ref[...], xcb, (((1,), (1,)), ((), ())),
                                  preferred_element_type=F32).astype(BF16)
    v_ref[...] = vb
    part = (jnp.dot(qb, wg_ref[0], preferred_element_type=F32)
            + jnp.dot(kb, wg_ref[1], preferred_element_type=F32)
            + jnp.dot(vb, wg_ref[2], preferred_element_type=F32))

    @pl.when(j == 0)
    def _():
        gacc_ref[...] = part

    @pl.when(j > 0)
    def _():
        gacc_ref[...] += part

    @pl.when(j == pl.num_programs(1) - 1)
    def _():
        _gate_tables(gacc_ref, bg_ref, g1_ref, g2_ref, g3_ref, tmp_ref, last_ref)


def _mlstm_pre(rest, conv_w, conv_b, wq_bd, wk_bd, wv_bd, wg, bg, n_seq, seq_len):
    n_tok = rest.shape[0]
    n_ct = D_MODEL // MXU_DIM
    tile = pl.BlockSpec((seq_len, MXU_DIM), lambda b, j: (b, j))
    wspec = pl.BlockSpec((None, MXU_DIM, MXU_DIM), lambda b, j: (j, 0, 0))
    gspec = pl.BlockSpec((None, seq_len, LANES), lambda b, j: (b, 0, 0))
    act = jax.ShapeDtypeStruct((n_tok, D_MODEL), BF16)
    gate = jax.ShapeDtypeStruct((n_seq, seq_len, LANES), F32)
    return pl.pallas_call(
        _mlpre_kernel,
        grid=(n_seq, n_ct),
        in_specs=[
            tile,
            pl.BlockSpec((3, MXU_DIM), lambda b, j: (0, j)),
            pl.BlockSpec((1, MXU_DIM), lambda b, j: (0, j)),
            wspec, wspec, wspec, wspec,
            pl.BlockSpec((3, MXU_DIM, LANES), lambda b, j: (0, j, 0)),
            pl.BlockSpec((1, LANES), lambda b, j: (0, 0)),
        ],
        out_specs=[tile, pl.BlockSpec((MXU_DIM, seq_len), lambda b, j: (j, b)), tile, tile,
                   gspec, gspec, gspec],
        out_shape=[act, jax.ShapeDtypeStruct((D_MODEL, n_tok), BF16), act, act,
                   gate, gate, gate],
        scratch_shapes=[pltpu.VMEM((seq_len, LANES), F32),
                        pltpu.VMEM((3, seq_len, LANES), F32),
                        pltpu.VMEM((4, seq_len // CHUNK, LANES), F32)],
        compiler_params=_cparams(("parallel", "arbitrary")),
    )(rest, conv_w, conv_b, wq_bd, jnp.swapaxes(wk_bd, 1, 2), wk_bd, wv_bd, wg, bg)


def _mlstm_kernel(q_ref, kt_ref, v_ref, gc1_ref, gc2_ref, gr_ref, o_ref,
                  vx_ref, qk_ref, h_ref):
    seq_len = q_ref.shape[0]
    n_chunks = seq_len // CHUNK
    dh = ML_HEAD_DIM
    r = lax.broadcasted_iota(jnp.int32, (CHUNK, CHUNK), 0)
    c = lax.broadcasted_iota(jnp.int32, (CHUNK, CHUNK), 1)
    vx_ref[:, :dh] = v_ref[...]
    vx_ref[:, dh:] = jnp.ones((seq_len, LANES), vx_ref.dtype)

    def lane_replicated(ref, rows, col):
        return jnp.broadcast_to(ref[rows, col:col + 1], (CHUNK, LANES))

    def chunk_step(direction, ci, c_st):
        mask = (c <= r) if direction == 0 else (c >= r)
        rows = slice(ci * CHUNK, (ci + 1) * CHUNK)
        qc = q_ref[rows, :]
        ktc = kt_ref[:, rows]
        vxc = vx_ref[rows, :]
        neg_mm = lane_replicated(gc1_ref, rows, 2 * direction)
        e_mj = lane_replicated(gc1_ref, rows, 2 * direction + 1)
        inter = lane_replicated(gc2_ref, rows, 2 * direction)
        g_row = gr_ref[2 * direction:2 * direction + 1, rows]
        w_row = gr_ref[2 * direction + 1:2 * direction + 2, rows]
        keep = gc2_ref[ci * CHUNK:ci * CHUNK + 1, 2 * direction + 1:2 * direction + 2]
        if ci in first_visit:
            qk = qk_ref[ci]
        else:
            qk = jnp.dot(qc, ktc, preferred_element_type=F32)
            qk_ref[ci] = qk
        sw = qk * jnp.exp(jnp.where(mask, neg_mm + g_row, -jnp.inf))
        intra = jnp.dot(sw.astype(BF16), vxc, preferred_element_type=F32)
        carry_in = jnp.dot(qc, c_st.astype(BF16), preferred_element_type=F32)
        den = intra[:, dh:] + inter * carry_in[:, dh:]
        rdiv = 1.0 / jnp.maximum(jnp.abs(den), e_mj)
        h = jnp.concatenate(
            [(intra[:, s * LANES:(s + 1) * LANES] + inter * carry_in[:, s * LANES:(s + 1) * LANES])
             * rdiv for s in range(dh // LANES)], axis=1)
        if ci in first_visit:
            h = h_ref[rows, :] + h
            mu = jnp.mean(h, axis=-1, keepdims=True)
            hc = h - mu
            var = jnp.mean(hc * hc, axis=-1, keepdims=True)
            o_ref[rows, :] = (hc * lax.rsqrt(var + LN_EPS)).astype(o_ref.dtype)
        else:
            h_ref[rows, :] = h
            first_visit.add(ci)
        kw = (ktc.astype(F32) * w_row).astype(BF16)
        return keep * c_st + jnp.dot(kw, vxc, preferred_element_type=F32)

    first_visit = set()
    c_f = jnp.zeros((dh, dh + LANES), F32)
    c_b = jnp.zeros((dh, dh + LANES), F32)
    for step in range(n_chunks):
        c_f = chunk_step(0, step, c_f)
        c_b = chunk_step(1, n_chunks - 1 - step, c_b)


def _mlstm(q, kt, v, gc1, gc2, gr3, n_seq, seq_len):
    n_tok = q.shape[0]
    tile = pl.BlockSpec((seq_len, ML_HEAD_DIM), lambda b, h: (b, h))
    col = pl.BlockSpec((None, None, seq_len, 4), lambda b, h: (b, h, 0, 0))
    return pl.pallas_call(
        _mlstm_kernel,
        grid=(n_seq, ML_HEADS),
        in_specs=[
            tile,
            pl.BlockSpec((ML_HEAD_DIM, seq_len), lambda b, h: (h, b)),
            tile, col, col,
            pl.BlockSpec((None, None, 4, seq_len), lambda b, h: (b, h, 0, 0)),
        ],
        out_specs=tile,
        out_shape=jax.ShapeDtypeStruct((n_tok, D_MODEL), BF16),
        scratch_shapes=[
            pltpu.VMEM((seq_len, ML_HEAD_DIM + LANES), BF16),
            pltpu.VMEM((seq_len // CHUNK, CHUNK, CHUNK), F32),
            pltpu.VMEM((seq_len, ML_HEAD_DIM), F32),
        ],
        compiler_params=_cparams(("parallel", "parallel")),
    )(q, kt, v, gc1, gc2, gr3)


def _layer_norm(y, g, b):
    mu = jnp.mean(y, axis=-1, keepdims=True)
    yc = y - mu
    var = jnp.mean(yc * yc, axis=-1, keepdims=True)
    return yc * lax.rsqrt(var + LN_EPS) * g + b


def _route(logits):
    lane = lax.broadcasted_iota(jnp.int32, logits.shape, 1)
    big = jnp.int32(4 * LANES)
    gl = jnp.where(lane < N_GROUPS, logits, NEG_BIG)
    gmax = jnp.max(gl, axis=-1, keepdims=True)
    gsum = jnp.sum(jnp.where(lane < N_GROUPS, jnp.exp(gl - gmax), 0.0), axis=-1, keepdims=True)
    g_sel = jnp.min(jnp.where(gl == gmax, lane, big), axis=-1, keepdims=True)
    g_prob = 1.0 / gsum
    lo = N_GROUPS + EXPERTS_PER_GROUP * g_sel
    in_group = jnp.logical_and(lane >= lo, lane < lo + EXPERTS_PER_GROUP)
    el = jnp.where(in_group, logits, NEG_BIG)
    emax = jnp.max(el, axis=-1, keepdims=True)
    esum = jnp.sum(jnp.where(in_group, jnp.exp(el - emax), 0.0), axis=-1, keepdims=True)
    i1 = jnp.min(jnp.where(el == emax, lane, big), axis=-1, keepdims=True)
    el2 = jnp.where(lane == i1, NEG_BIG, el)
    emax2 = jnp.max(el2, axis=-1, keepdims=True)
    i2 = jnp.min(jnp.where(el2 == emax2, lane, big), axis=-1, keepdims=True)
    p1 = 1.0 / esum
    p2 = jnp.exp(emax2 - emax) / esum
    psum = p1 + p2
    gate1 = g_prob * p1 / psum
    gate2 = g_prob * p2 / psum
    e1 = (i1 - N_GROUPS).astype(F32)
    e2 = (i2 - N_GROUPS).astype(F32)
    return jnp.where(lane == 0, e1, jnp.where(lane == 1, e2, jnp.where(lane == 2, gate1, gate2)))


def _post_kernel(attn_ref, hn_ref, xc_ref, z_ref, ga_ref, gb_ref, x_ref,
                 wpa_ref, wpb_ref, wout_ref, gn_ref, skip_ref, lg_ref, lb_ref, wr_ref, br_ref,
                 x1t_ref, route_ref, *, sub):
    tm = x_ref.shape[0]
    for s in range(tm // sub):
        rows = slice(s * sub, (s + 1) * sub)
        z = z_ref[rows, :].astype(F32)
        ml = ((hn_ref[rows, :].astype(F32) * gn_ref[...]
               + skip_ref[...] * xc_ref[rows, :].astype(F32)) * (z * _sigmoid(z)))
        a_out = jnp.dot(attn_ref[rows, :], wpa_ref[...], preferred_element_type=F32)
        m_out = jnp.dot(ml.astype(BF16), wpb_ref[...], preferred_element_type=F32)
        mixed = (_sigmoid(ga_ref[rows, :].astype(F32)) * a_out
                 + _sigmoid(gb_ref[rows, :].astype(F32)) * m_out)
        y = DN_ALPHA * x_ref[rows, :] + jnp.dot(mixed.astype(BF16), wout_ref[...],
                                               preferred_element_type=F32)
        x1 = _layer_norm(y, lg_ref[...], lb_ref[...])
        for c in range(ROW_CHUNKS):
            x1t_ref[pl.ds(s * sub * ROW_CHUNKS + c, sub, stride=ROW_CHUNKS), :] = (
                x1[:, c * LANES:(c + 1) * LANES])
        logits = br_ref[...] + jnp.dot(x1.astype(BF16), wr_ref[...], preferred_element_type=F32)
        route_ref[rows, :] = _route(logits)


def _post(attn, hn, xc, rest, x, w_pa, w_pb, w_out, gn_g, skip, ln_g, ln_b, w_route, b_route):
    n_tok = x.shape[0]
    sub = 256
    tm = 2 * sub if n_tok % (2 * sub) == 0 else sub
    row = lambda col: pl.BlockSpec((tm, D_MODEL), lambda i, col=col: (i, col))
    full = lambda shape: pl.BlockSpec(shape, lambda i: tuple(0 for _ in shape))
    vec = full((1, D_MODEL))
    return pl.pallas_call(
        functools.partial(_post_kernel, sub=sub),
        grid=(n_tok // tm,),
        in_specs=[row(0), row(0), row(0), row(1), row(2), row(3), row(0),
                  full((D_MODEL, D_MODEL)), full((D_MODEL, D_MODEL)), full((D_MODEL, D_MODEL)),
                  vec, vec, vec, vec,
                  full((D_MODEL, LANES)), full((1, LANES))],
        out_specs=[pl.BlockSpec((tm * ROW_CHUNKS, LANES), lambda i: (i, 0)),
                   pl.BlockSpec((tm, LANES), lambda i: (i, 0))],
        out_shape=[jax.ShapeDtypeStruct((n_tok * ROW_CHUNKS, LANES), F32),
                   jax.ShapeDtypeStruct((n_tok, LANES), F32)],
        compiler_params=_cparams(("parallel",)),
    )(attn, hn, xc, rest, rest, rest, x, w_pa, w_pb, w_out, gn_g, skip, ln_g, ln_b,
      w_route, b_route)


def _moe_kernel(bexp_ref, nused_ref,
                tok_hbm, dst_hbm, x_hbm, wg_ref, wu_ref, wd_ref, y_hbm,
                tok_smem, dst_smem, xbuf, obuf, idx_sem, gat_sem, sca_sem):
    i = pl.program_id(0)
    n_used = nused_ref[0]
    p = i % 2

    def index_copies(blk):
        slot = blk % 3
        return (pltpu.make_async_copy(tok_hbm.at[blk], tok_smem.at[slot], idx_sem.at[0, slot]),
                pltpu.make_async_copy(dst_hbm.at[blk], dst_smem.at[slot], idx_sem.at[1, slot]))

    def start_indices(blk):
        for cp in index_copies(blk):
            cp.start()

    def wait_indices(blk):
        for cp in index_copies(blk):
            cp.wait()

    def tile_rows(r):
        return pl.ds(r * ROW_CHUNKS, ROW_CHUNKS)

    def start_gather(blk, slot, rows=range(MOE_ROWS)):
        islot = blk % 3

        for row in rows:
            t = tok_smem[islot, row // LANES, row % LANES]
            pltpu.make_async_copy(x_hbm.at[t], xbuf.at[slot, tile_rows(row)],
                                  gat_sem.at[slot]).start(priority=row % DMA_PRIORITIES)

    def wait_gather(slot):
        pltpu.make_async_copy(xbuf.at[slot], xbuf.at[slot], gat_sem.at[slot]).wait()

    def start_scatter(blk, slot, rows):
        islot = blk % 3

        for row in rows:
            t = dst_smem[islot, row // LANES, row % LANES]
            pltpu.make_async_copy(obuf.at[slot, tile_rows(row)], y_hbm.at[t],
                                  sca_sem.at[slot]).start(priority=row % DMA_PRIORITIES)

    def wait_scatter(slot):
        pltpu.make_async_copy(obuf.at[slot], obuf.at[slot], sca_sem.at[slot]).wait()

    @pl.when(i == 0)
    def _():
        start_indices(0)
        wait_indices(0)
        start_gather(0, 0)

        @pl.when(1 < n_used)
        def _():
            start_indices(1)

    @pl.when(i + 1 < n_used)
    def _():
        wait_indices(i + 1)

    @pl.when(i + 2 < n_used)
    def _():
        start_indices(i + 2)

    @pl.when(i < n_used)
    def _():
        wait_gather(p)

        @pl.when(i >= 2)
        def _():
            wait_scatter(p)

        nxt = jnp.minimum(i + 1, n_used - 1)
        half = MOE_ROWS // 2
        for h in range(2):
            rows = range(h * half, (h + 1) * half)
            start_gather(nxt, 1 - p, rows)
            xb = jnp.concatenate(
                [xbuf[p, pl.ds(h * half * ROW_CHUNKS + c, half, stride=ROW_CHUNKS), :]
                 for c in range(ROW_CHUNKS)], axis=1).astype(BF16)
            hg = jnp.dot(xb, wg_ref[...], preferred_element_type=F32)
            hu = jnp.dot(xb, wu_ref[...], preferred_element_type=F32)
            hh = (hg * _sigmoid(hg) * hu).astype(BF16)
            out = jnp.dot(hh, wd_ref[...], preferred_element_type=F32)
            for c in range(ROW_CHUNKS):
                obuf[p, pl.ds(h * half * ROW_CHUNKS + c, half, stride=ROW_CHUNKS), :] = (
                    out[:, c * LANES:(c + 1) * LANES])
            start_scatter(i, p, rows)

        @pl.when(i == n_used - 1)
        def _():
            wait_scatter(p)
            wait_gather(1 - p)

            @pl.when(i >= 1)
            def _():
                wait_scatter(1 - p)


def _moe(x1t, slot_tok, slot_dst, blk_exp, n_used, we_g, we_u, we_d):
    n_tok = x1t.shape[0] // ROW_CHUNKS
    n_blk = blk_exp.shape[0]
    n_out = TOP_K * n_tok + 2 * MOE_ROWS
    wspec_in = pl.BlockSpec((None, D_MODEL, D_EXPERT), lambda i, be, nu: (be[i], 0, 0))
    wspec_out = pl.BlockSpec((None, D_EXPERT, D_MODEL), lambda i, be, nu: (be[i], 0, 0))
    grid_spec = pltpu.PrefetchScalarGridSpec(
        num_scalar_prefetch=2,
        grid=(n_blk,),
        in_specs=[
            pl.BlockSpec(memory_space=pl.ANY),
            pl.BlockSpec(memory_space=pl.ANY),
            pl.BlockSpec(memory_space=pl.ANY),
            wspec_in, wspec_in, wspec_out,
        ],
        out_specs=pl.BlockSpec(memory_space=pl.ANY),
        scratch_shapes=[
            pltpu.SMEM((3, MOE_ROWS // LANES, LANES), jnp.int32),
            pltpu.SMEM((3, MOE_ROWS // LANES, LANES), jnp.int32),
            pltpu.VMEM((2, MOE_ROWS * ROW_CHUNKS, LANES), F32),
            pltpu.VMEM((2, MOE_ROWS * ROW_CHUNKS, LANES), F32),
            pltpu.SemaphoreType.DMA((2, 3)),
            pltpu.SemaphoreType.DMA((2,)),
            pltpu.SemaphoreType.DMA((2,)),
        ],
    )
    y = pl.pallas_call(
        _moe_kernel,
        grid_spec=grid_spec,
        out_shape=jax.ShapeDtypeStruct((n_out, ROW_CHUNKS, LANES), F32),
        compiler_params=_cparams(("arbitrary",)),
    )(blk_exp, n_used, slot_tok.reshape(n_blk, MOE_ROWS // LANES, LANES),
      slot_dst.reshape(n_blk, MOE_ROWS // LANES, LANES),
      x1t.reshape(n_tok, ROW_CHUNKS, LANES), we_g, we_u, we_d)
    return y.reshape(n_out * ROW_CHUNKS, LANES)


def _route_plan(route, n_tok):
    n_assign = n_tok * TOP_K
    eid = route[:, 0:TOP_K].astype(jnp.int32).reshape(n_assign)
    order = jnp.argsort(eid, stable=False).astype(jnp.int32)
    counts = jnp.zeros((N_EXPERTS,), jnp.int32).at[eid].add(1)
    offsets = jnp.cumsum(counts) - counts
    padded = (counts + MOE_ROWS - 1) // MOE_ROWS * MOE_ROWS
    pad_end = jnp.cumsum(padded)
    pad_start = pad_end - padded
    n_blk = (n_assign + MOE_ROWS - 1) // MOE_ROWS + N_EXPERTS
    blk_start = jnp.arange(n_blk, dtype=jnp.int32) * MOE_ROWS
    blk_exp = jnp.minimum(jnp.sum(pad_end[None, :] <= blk_start[:, None], axis=1),
                          N_EXPERTS - 1).astype(jnp.int32)
    row = jnp.arange(MOE_ROWS, dtype=jnp.int32)[None, :]
    j = (blk_start - pad_start[blk_exp])[:, None] + row
    valid = j < counts[blk_exp][:, None]
    src = jnp.clip(offsets[blk_exp][:, None] + j, 0, n_assign - 1)
    assign = order[src]
    tok = assign // TOP_K
    dummy = TOP_K * n_tok + (jnp.arange(n_blk, dtype=jnp.int32) % 2)[:, None] * MOE_ROWS + row
    slot_tok = jnp.where(valid, tok, 0)
    slot_dst = jnp.where(valid, (assign % TOP_K) * n_tok + tok, dummy)
    n_used = (pad_end[-1:] // MOE_ROWS).astype(jnp.int32)
    return slot_tok, slot_dst, blk_exp, n_used


def _combine_rows(x_ref, y0_ref, y1_ref, r_ref, g_ref, b_ref, tm):
    def chunk(ref, c):
        return ref[pl.ds(c, tm, stride=ROW_CHUNKS), :]

    route = r_ref[...]
    gate0 = jnp.broadcast_to(route[:, TOP_K:TOP_K + 1], (tm, LANES))
    gate1 = jnp.broadcast_to(route[:, TOP_K + 1:TOP_K + 2], (tm, LANES))
    y = jnp.concatenate(
        [DN_ALPHA * chunk(x_ref, c) + (gate0 * chunk(y0_ref, c) + gate1 * chunk(y1_ref, c))
         for c in range(ROW_CHUNKS)], axis=1)
    return _layer_norm(y, g_ref[...], b_ref[...])


def _combine_kernel(x_ref, y0_ref, y1_ref, r_ref, g_ref, b_ref, o_ref):
    o_ref[...] = _combine_rows(x_ref, y0_ref, y1_ref, r_ref, g_ref, b_ref, o_ref.shape[0])


def _combine(x1t, y, route, ln_g, ln_b, first_tok, n_out):
    n_tok = x1t.shape[0] // ROW_CHUNKS
    tm = 256
    n_t = n_tok // tm
    t0 = first_tok // tm
    vec = pl.BlockSpec((1, D_MODEL), lambda i: (0, 0))
    tiles = lambda off: pl.BlockSpec((tm * ROW_CHUNKS, LANES), lambda i, off=off: (i + off, 0))
    return pl.pallas_call(
        _combine_kernel,
        grid=(n_out // tm,),
        in_specs=[tiles(t0), tiles(t0), tiles(t0 + n_t),
                  pl.BlockSpec((tm, LANES), lambda i: (i + t0, 0)), vec, vec],
        out_specs=pl.BlockSpec((tm, D_MODEL), lambda i: (i, 0)),
        out_shape=jax.ShapeDtypeStruct((n_out, D_MODEL), F32),
        compiler_params=_cparams(("parallel",)),
    )(x1t, y, y, route, ln_g, ln_b)


def _rope_tables(seq_len):
    half = DA_HEAD_DIM // 2
    inv = ROPE_THETA ** (-jnp.arange(half, dtype=F32) * (2.0 / DA_HEAD_DIM))
    ang = jnp.arange(seq_len, dtype=F32)[:, None] * inv[None, :]
    cos = jnp.tile(jnp.cos(ang), (1, LANES // half))
    sin = jnp.tile(jnp.sin(ang), (1, LANES // half))
    lane = jnp.arange(LANES)
    sign = jnp.where((lane % DA_HEAD_DIM) < half, -1.0, 1.0).astype(F32)
    return cos, sin * sign[None, :]


def _block_diag_tiles(w):
    per_tile = MXU_DIM // ML_PROJ_BLOCK
    n_tiles = w.shape[0] // per_tile
    w4 = w.reshape(n_tiles, per_tile, ML_PROJ_BLOCK, ML_PROJ_BLOCK)
    eye = jnp.eye(per_tile, dtype=w.dtype)
    bd = jnp.einsum('jgio,gh->jgiho', w4, eye)
    return bd.reshape(n_tiles, MXU_DIM, MXU_DIM).astype(BF16)


def _gate_perm():
    idx = []
    for h in range(ML_HEADS):
        for d in range(2):
            for kind in range(2):
                idx.append(d * 2 * ML_HEADS + kind * ML_HEADS + h)
    return jnp.array(idx, dtype=jnp.int32)


def _layer(src, n_seq, seq_len, lambda_init, cos, sin, p):
    qk_w = DA_HEADS * 2 * DA_HEAD_DIM
    v_w = DA_HEADS * DA_V_DIM
    x, qkv, rest = _inproj(src, p['w_in'].astype(BF16), cos, sin, seq_len, n_rope=2 * qk_w,
                           n_scaled=qk_w, n_qkv=2 * qk_w + v_w,
                           scale=DA_HEAD_DIM ** -0.5 * math.log2(math.e))
    n_tok = x.shape[0]

    lam = (jnp.exp(jnp.sum(p['lq1'] * p['lk1'])) - jnp.exp(jnp.sum(p['lq2'] * p['lk2']))
           + lambda_init)
    scalars = jnp.stack([lam, jnp.asarray(1.0 - lambda_init, F32)]).astype(F32)
    attn = _attention(qkv, scalars, p['subln_g'].reshape(1, DA_V_DIM), n_seq, seq_len)

    perm = _gate_perm()
    n_gate = 4 * ML_HEADS
    wg = p['w_gate'][:, perm].reshape(3, D_MODEL, n_gate)
    wg = jnp.pad(wg, ((0, 0), (0, 0), (0, LANES - n_gate))).astype(BF16)
    bg = jnp.pad(p['b_gate'][perm], (0, LANES - n_gate)).reshape(1, LANES)
    q, kt, v, xc, g1, g2, g3 = _mlstm_pre(
        rest, p['conv_w'], p['conv_b'].reshape(1, D_MODEL), _block_diag_tiles(p['wq']),
        _block_diag_tiles(p['wk']), _block_diag_tiles(p['wv']), wg, bg, n_seq, seq_len)

    def per_head(t, perm):
        t = t[:, :, :n_gate].reshape(n_seq, seq_len, ML_HEADS, 4)
        return jnp.transpose(t, perm)

    hn = _mlstm(q, kt, v, per_head(g1, (0, 2, 1, 3)), per_head(g2, (0, 2, 1, 3)),
                per_head(g3, (0, 2, 3, 1)), n_seq, seq_len)

    n_route = N_GROUPS + N_EXPERTS
    w_route = jnp.concatenate([p['rg_w'], p['re_w']], axis=1)
    w_route = jnp.pad(w_route, ((0, 0), (0, LANES - n_route)))
    w_route = w_route.astype(BF16)
    b_route = jnp.pad(jnp.concatenate([p['rg_b'], p['re_b']]), (0, LANES - n_route))
    vec = lambda a: a.reshape(1, D_MODEL)
    x1, route = _post(attn, hn, xc, rest, x, p['w_pa'].astype(BF16), p['w_pb'].astype(BF16),
                      p['w_out'].astype(BF16), vec(p['gn_g']), vec(p['skip']), vec(p['ln1_g']),
                      vec(p['ln1_b']), w_route, b_route.reshape(1, LANES))

    plan = _route_plan(route, n_tok)
    y = _moe(x1, *plan, p['we_g'].astype(BF16), p['we_u'].astype(BF16), p['we_d'].astype(BF16))
    return x1, y, route, vec(p['ln2_g']), vec(p['ln2_b'])


def kernel(x_prompt, x_sample, w_in, da_lambda_q1, da_lambda_k1, da_lambda_q2, da_lambda_k2, da_subln_g, ml_conv_w, ml_conv_b, ml_wq, ml_wk, ml_wv, ml_w_gate, ml_b_gate, ml_skip, ml_gn_g, w_pa, w_pb, w_out, ln1_g, ln1_b, router_group_w, router_group_b, router_expert_w, router_expert_b, w_e_gate, w_e_up, w_e_down, ln2_g, ln2_b):
    n_p, seq_len, d = x_prompt.shape
    n_s = x_sample.shape[0]
    assert x_sample.shape[1] == seq_len and d == D_MODEL and seq_len % CHUNK == 0
    n_seq = n_p + n_s
    cos, sin = _rope_tables(seq_len)
    stacked = dict(w_in=w_in, lq1=da_lambda_q1, lk1=da_lambda_k1, lq2=da_lambda_q2,
                   lk2=da_lambda_k2, subln_g=da_subln_g, conv_w=ml_conv_w, conv_b=ml_conv_b,
                   wq=ml_wq, wk=ml_wk, wv=ml_wv, w_gate=ml_w_gate, b_gate=ml_b_gate,
                   skip=ml_skip, gn_g=ml_gn_g, w_pa=w_pa, w_pb=w_pb, w_out=w_out,
                   ln1_g=ln1_g, ln1_b=ln1_b, rg_w=router_group_w, rg_b=router_group_b,
                   re_w=router_expert_w, re_b=router_expert_b, we_g=w_e_gate, we_u=w_e_up,
                   we_d=w_e_down, ln2_g=ln2_g, ln2_b=ln2_b)
    src = [x_prompt.reshape(n_p * seq_len, d), x_sample.reshape(n_s * seq_len, d)]
    for l in range(w_in.shape[0]):
        lambda_init = 0.8 - 0.6 * math.exp(-0.3 * l)
        src = _layer(src, n_seq, seq_len, lambda_init, cos, sin,
                     {k: a[l] for k, a in stacked.items()})
    y_p = _combine(*src, 0, n_p * seq_len).reshape(n_p, seq_len, d)
    y_s = _combine(*src, n_p * seq_len, n_s * seq_len).reshape(n_s, seq_len, d)
    return (y_p, y_s)
```
